```python
import math
import jax
import jax.numpy as jnp
from jax import lax
import numpy as np

D_MODEL = 1024
BATCH = 1
SEQ = 16384
DEPTH = 2
DEC_BATCH = 32
DEC_SEQ = 4
PAST_LEN = 16384
PAGE_SIZE = 128

A_GROUPS = ((128, 1), (512, 4), (2048, 16))
N_A_GROUPS = len(A_GROUPS)
A_HEAD_DIM = 32
A_HEADS = D_MODEL // (8 * A_HEAD_DIM)
A_WIDTH = A_HEADS * A_HEAD_DIM
ROPE_THETA = 10000.0
QUERY_BLOCK = 128
B_WIDTH = D_MODEL // 4
B_CONV = 31
C_HEAD_DIM = 64
C_WIDTH = 3 * D_MODEL // 8
C_HEADS = C_WIDTH // C_HEAD_DIM
C_DECAY_RANK = 64
C_ICLR_RANK = 64
C_GATE_RANK = 128
C_SPLITS = (C_WIDTH, 2 * C_WIDTH, 3 * C_WIDTH, 3 * C_WIDTH + C_DECAY_RANK,
            3 * C_WIDTH + C_DECAY_RANK + C_ICLR_RANK)
DECAY_SCALE = math.exp(-0.5)
D_FF = 11 * D_MODEL // 4
F_CONV = 3
IN_A = N_A_GROUPS * 3 * A_WIDTH
IN_B = 2 * B_WIDTH
IN_C = 3 * C_WIDTH + C_DECAY_RANK + C_ICLR_RANK + C_GATE_RANK
IN_GATE = 3 * D_MODEL
IN_WIDTH = IN_A + IN_B + IN_C + IN_GATE
RMS_EPS = 1e-6
LN_EPS = 1e-5
GN_EPS = C_HEAD_DIM * 1e-5

kernel_name = 'hybrid_dilated_conformer_rwkv7_decoder_step'


def rms_norm(x, g):
    x32 = x.astype(jnp.float32)
    y = x32 * lax.rsqrt(jnp.mean(x32 * x32, axis=-1, keepdims=True) + RMS_EPS)
    return (y * g.astype(jnp.float32)).astype(x.dtype)


def layer_norm(x, g, b):
    x32 = x.astype(jnp.float32)
    mu = jnp.mean(x32, axis=-1, keepdims=True)
    var = jnp.mean(jnp.square(x32 - mu), axis=-1, keepdims=True)
    y = (x32 - mu) * lax.rsqrt(var + LN_EPS) * g.astype(jnp.float32) + b.astype(jnp.float32)
    return y.astype(x.dtype)


def rope(x, pos):
    d = x.shape[-1]
    half = d // 2
    inv = ROPE_THETA ** (-(jnp.arange(half, dtype=jnp.float32) * 2.0 / d))
    ang = pos.astype(jnp.float32)[:, None] * inv[None, :]
    cos = jnp.cos(ang)[None, :, None, :]
    sin = jnp.sin(ang)[None, :, None, :]
    x32 = x.astype(jnp.float32)
    x1, x2 = x32[..., :half], x32[..., half:]
    return jnp.concatenate([x1 * cos - x2 * sin, x2 * cos + x1 * sin], axis=-1).astype(x.dtype)


def causal_depthwise_conv(x_ext, w, b):
    c = x_ext.shape[-1]
    out = lax.conv_general_dilated(x_ext, w[:, None, :].astype(x_ext.dtype), window_strides=(1,),
                                   padding='VALID', dimension_numbers=('NWC', 'WIO', 'NWC'),
                                   feature_group_count=c)
    return out + b.astype(x_ext.dtype)


def dilated_window_attention(q, k_ext, v_ext, window, dilation, min_valid):
    bsz, tq, h, d = q.shape
    n_keys = window // dilation + 1
    qb = QUERY_BLOCK if tq % QUERY_BLOCK == 0 else tq
    n_blocks = tq // qb
    rel = jnp.arange(qb)[:, None] - dilation * jnp.arange(n_keys)[None, :]
    scale = 1.0 / math.sqrt(d)

    def one_block(blk):
        start = blk * qb
        qs = lax.dynamic_slice_in_dim(q, start, qb, axis=1).astype(jnp.float32)
        idx = window + start + rel
        ks = jnp.take(k_ext, idx, axis=1).astype(jnp.float32)
        vs = jnp.take(v_ext, idx, axis=1).astype(jnp.float32)
        s = jnp.einsum('bqhd,bqjhd->bqhj', qs, ks) * scale
        s = jnp.where((idx >= min_valid)[None, :, None, :], s, -jnp.inf)
        m = jnp.max(s, axis=-1, keepdims=True)
        p = jnp.exp(s - m)
        l = jnp.sum(p, axis=-1, keepdims=True)
        o = jnp.einsum('bqhj,bqjhd->bqhd', p / l, vs)
        return o, (m + jnp.log(l))[..., 0]

    o, lse = lax.map(one_block, jnp.arange(n_blocks))
    o = jnp.moveaxis(o, 0, 1).reshape(bsz, tq, h, d)
    lse = jnp.moveaxis(lse, 0, 1).reshape(bsz, tq, h)
    return o, lse


def rwkv7_time_mix(xs, s0, p):
    bsz, t, _ = xs.shape
    f32 = lambda z: z.astype(jnp.float32)
    r, k, v, wl, al, gl = jnp.split(f32(xs), list(C_SPLITS), axis=-1)
    log_w = -DECAY_SCALE * jax.nn.sigmoid(f32(p['c_w0']) + jnp.tanh(wl) @ f32(p['c_w2']))
    a = jax.nn.sigmoid(f32(p['c_a0']) + al @ f32(p['c_a2']))
    g = jax.nn.sigmoid(gl) @ f32(p['c_g2'])
    heads = lambda z: z.reshape(bsz, t, C_HEADS, C_HEAD_DIM)
    kk = heads(k * f32(p['c_kk']))
    kk = kk * lax.rsqrt(jnp.maximum(jnp.sum(kk * kk, axis=-1, keepdims=True), 1e-24))
    k = heads(k * (1.0 + (a - 1.0) * f32(p['c_ka'])))
    r, v, a, w = heads(r), heads(v), heads(a), jnp.exp(heads(log_w))

    def step(s, inp):
        r_t, w_t, k_t, v_t, kk_t, b_t = inp
        sa = jnp.einsum('bhvk,bhk->bhv', s, kk_t)
        s = s * w_t[:, :, None, :] - sa[..., None] * b_t[:, :, None, :] + v_t[..., None] * k_t[:, :, None, :]
        return s, jnp.einsum('bhvk,bhk->bhv', s, r_t)

    seq = tuple(jnp.moveaxis(z, 1, 0) for z in (r, w, k, v, kk, kk * a))
    s_final, y = lax.scan(step, f32(s0), seq)
    y = jnp.moveaxis(y, 0, 1)
    mu = jnp.mean(y, axis=-1, keepdims=True)
    var = jnp.mean(jnp.square(y - mu), axis=-1, keepdims=True)
    y = ((y - mu) * lax.rsqrt(var + GN_EPS)).reshape(bsz, t, C_WIDTH)
    y = y * f32(p['c_gn_g']) + f32(p['c_gn_b'])
    bonus = jnp.sum(r * k * f32(p['c_rk']), axis=-1, keepdims=True) * v
    y = (y + bonus.reshape(bsz, t, C_WIDTH)) * g
    return y.astype(xs.dtype), s_final


def trunk_layer(x, pos, a_ctx, a_min_valid, b_ctx, c_shift, c_state, f_ctx, p):
    bsz, t, _ = x.shape
    dt = x.dtype
    h = rms_norm(x, p['norm_mix_pre'])
    proj = h @ p['w_in']
    pa, pb, pc, pg = jnp.split(proj, [IN_A, IN_A + IN_B, IN_A + IN_B + IN_C], axis=-1)

    pa = pa.reshape(bsz, t, N_A_GROUPS, 3, A_HEADS, A_HEAD_DIM)
    outs, lses, kv_rows = [], [], []
    for gi, (window, dilation) in enumerate(A_GROUPS):
        q = rope(pa[:, :, gi, 0], pos)
        kv = jnp.stack([rope(pa[:, :, gi, 1], pos), pa[:, :, gi, 2]], axis=2)
        kv_ext = jnp.concatenate([a_ctx[gi].astype(dt), kv], axis=1)
        o, lse = dilated_window_attention(q, kv_ext[:, :, 0], kv_ext[:, :, 1], window, dilation,
                                          a_min_valid[gi])
        outs.append(o)
        lses.append(lse)
        kv_rows.append(kv)
    alpha = jax.nn.softmax(jnp.stack(lses), axis=0)[..., None]
    o_a = jnp.sum(alpha * jnp.stack(outs), axis=0).reshape(bsz, t, A_WIDTH).astype(dt)

    glu = pb[..., :B_WIDTH] * jax.nn.sigmoid(pb[..., B_WIDTH:])
    b_ext = jnp.concatenate([b_ctx.astype(dt), glu], axis=1)
    cb = causal_depthwise_conv(b_ext, p['b_dw_w'], p['b_dw_b'])
    o_b = jax.nn.silu(layer_norm(cb, p['b_ln_g'], p['b_ln_b']))

    prev = jnp.concatenate([c_shift[:, None].astype(dt), pc[:, :-1]], axis=1)
    xs = pc + p['c_mu'] * (prev - pc)
    o_c, c_state_new = rwkv7_time_mix(xs, c_state, p)

    gates = jax.nn.sigmoid(pg).reshape(bsz, t, 3, D_MODEL)
    merged = (gates[:, :, 0] * (o_a @ p['w_br_a']) + gates[:, :, 1] * (o_b @ p['w_br_b'])
              + gates[:, :, 2] * (o_c @ p['w_br_c']))
    x = x + rms_norm(merged @ p['w_out'], p['norm_mix_post'])

    u = rms_norm(x, p['norm_ffn_pre']) @ p['f_up']
    f_ext = jnp.concatenate([f_ctx.astype(dt), u], axis=1)
    cu = causal_depthwise_conv(f_ext, p['f_dw_w'], p['f_dw_b'])
    f = (jax.nn.gelu(cu[..., :D_FF]) * cu[..., D_FF:]) @ p['f_down']
    x = x + rms_norm(f, p['norm_ffn_post'])
    return (x, kv_rows, b_ext[:, -(B_CONV - 1):], pc[:, -1], c_state_new.astype(dt),
            f_ext[:, -(F_CONV - 1):])


def setup_inputs(seed: int = 0) -> dict:
    key = jax.random.key(seed)
    keys = iter(jax.random.split(key, 40))

    def nrm(shape, scale):
        return scale * jax.random.normal(next(keys), shape, jnp.float32)

    def gain(shape):
        return 1.0 + 0.05 * jax.random.normal(next(keys), shape, jnp.float32)

    kv_rows = [min(window, PAST_LEN) for window, _ in A_GROUPS]
    kv_tail = (2, A_HEADS, A_HEAD_DIM)
    return {
        'x_prompt': nrm((BATCH, SEQ, D_MODEL), 1.0),
        'x_sample': nrm((DEC_BATCH, DEC_SEQ, D_MODEL), 1.0),
        'cache_a_kv0': nrm((DEPTH, DEC_BATCH, kv_rows[0]) + kv_tail, 1.0),
        'cache_a_kv1': nrm((DEPTH, DEC_BATCH, kv_rows[1]) + kv_tail, 1.0),
        'cache_a_kv2': nrm((DEPTH, DEC_BATCH, kv_rows[2]) + kv_tail, 1.0),
        'state_b_conv': nrm((DEPTH, DEC_BATCH, B_CONV - 1, B_WIDTH), 1.0),
        'state_c_shift': nrm((DEPTH, DEC_BATCH, IN_C), 1.0),
        'state_c_wkv': nrm((DEPTH, DEC_BATCH, C_HEADS, C_HEAD_DIM, C_HEAD_DIM), 0.3),
        'state_f_conv': nrm((DEPTH, DEC_BATCH, F_CONV - 1, 2 * D_FF), 1.0),
        'norm_mix_pre': gain((DEPTH, D_MODEL)),
        'norm_mix_post': gain((DEPTH, D_MODEL)),
        'norm_ffn_pre': gain((DEPTH, D_MODEL)),
        'norm_ffn_post': gain((DEPTH, D_MODEL)),
        'w_in': nrm((DEPTH, D_MODEL, IN_WIDTH), D_MODEL ** -0.5),
        'b_dw_w': nrm((DEPTH, B_CONV, B_WIDTH), B_CONV ** -0.5),
        'b_dw_b': nrm((DEPTH, B_WIDTH), 0.02),
        'b_ln_g': gain((DEPTH, B_WIDTH)),
        'b_ln_b': nrm((DEPTH, B_WIDTH), 0.02),
        'c_mu': jax.random.uniform(next(keys), (DEPTH, IN_C), jnp.float32),
        'c_w0': nrm((DEPTH, C_WIDTH), 0.5),
        'c_w2': nrm((DEPTH, C_DECAY_RANK, C_WIDTH), 0.1),
        'c_a0': nrm((DEPTH, C_WIDTH), 0.1),
        'c_a2': nrm((DEPTH, C_ICLR_RANK, C_WIDTH), C_ICLR_RANK ** -0.5),
        'c_g2': nrm((DEPTH, C_GATE_RANK, C_WIDTH), C_GATE_RANK ** -0.5),
        'c_kk': 0.85 + nrm((DEPTH, C_WIDTH), 0.05),
        'c_ka': gain((DEPTH, C_WIDTH)),
        'c_rk': nrm((DEPTH, C_HEADS, C_HEAD_DIM), 0.1),
        'c_gn_g': gain((DEPTH, C_WIDTH)),
        'c_gn_b': nrm((DEPTH, C_WIDTH), 0.02),
        'w_br_a': nrm((DEPTH, A_WIDTH, D_MODEL), A_WIDTH ** -0.5),
        'w_br_b': nrm((DEPTH, B_WIDTH, D_MODEL), B_WIDTH ** -0.5),
        'w_br_c': nrm((DEPTH, C_WIDTH, D_MODEL), C_WIDTH ** -0.5),
        'w_out': nrm((DEPTH, D_MODEL, D_MODEL), D_MODEL ** -0.5),
        'f_up': nrm((DEPTH, D_MODEL, 2 * D_FF), D_MODEL ** -0.5),
        'f_dw_w': nrm((DEPTH, F_CONV, 2 * D_FF), F_CONV ** -0.5),
        'f_dw_b': nrm((DEPTH, 2 * D_FF), 0.02),
        'f_down': nrm((DEPTH, D_FF, D_MODEL), D_FF ** -0.5),
    }


def reference(x_prompt, x_sample, cache_a_kv0, cache_a_kv1, cache_a_kv2, state_b_conv, state_c_shift,
              state_c_wkv, state_f_conv, norm_mix_pre, norm_mix_post, norm_ffn_pre, norm_ffn_post, w_in,
              b_dw_w, b_dw_b, b_ln_g, b_ln_b, c_mu, c_w0, c_w2, c_a0, c_a2, c_g2, c_kk, c_ka, c_rk,
              c_gn_g, c_gn_b, w_br_a, w_br_b, w_br_c, w_out, f_up, f_dw_w, f_dw_b, f_down):
    bp, tp = x_prompt.shape[0], x_prompt.shape[1]
    ts = x_sample.shape[1]
    dtp = x_prompt.dtype
    pos_p = jnp.arange(tp, dtype=jnp.int32)
    pos_s = PAST_LEN + jnp.arange(ts, dtype=jnp.int32)
    a_caches = (cache_a_kv0, cache_a_kv1, cache_a_kv2)
    y_p, y_s = x_prompt, x_sample
    a_p = [[] for _ in A_GROUPS]
    a_s = [[] for _ in A_GROUPS]
    b_p, b_s, sh_p, sh_s, wkv_p, wkv_s, f_p, f_s = [], [], [], [], [], [], [], []
    for l in range(DEPTH):
        p = {
            'norm_mix_pre': norm_mix_pre[l], 'norm_mix_post': norm_mix_post[l],
            'norm_ffn_pre': norm_ffn_pre[l], 'norm_ffn_post': norm_ffn_post[l],
            'w_in': w_in[l], 'b_dw_w': b_dw_w[l], 'b_dw_b': b_dw_b[l], 'b_ln_g': b_ln_g[l],
            'b_ln_b': b_ln_b[l], 'c_mu': c_mu[l], 'c_w0': c_w0[l], 'c_w2': c_w2[l], 'c_a0': c_a0[l],
            'c_a2': c_a2[l], 'c_g2': c_g2[l], 'c_kk': c_kk[l], 'c_ka': c_ka[l], 'c_rk': c_rk[l],
            'c_gn_g': c_gn_g[l], 'c_gn_b': c_gn_b[l], 'w_br_a': w_br_a[l], 'w_br_b': w_br_b[l],
            'w_br_c': w_br_c[l], 'w_out': w_out[l], 'f_up': f_up[l], 'f_dw_w': f_dw_w[l],
            'f_dw_b': f_dw_b[l], 'f_down': f_down[l],
        }
        ctx_p = [jnp.zeros((bp, window, 2, A_HEADS, A_HEAD_DIM), dtp) for window, _ in A_GROUPS]
        min_p = [window for window, _ in A_GROUPS]
        y_p, kv_new, b_new, sh_new, st_new, f_new = trunk_layer(
            y_p, pos_p, ctx_p, min_p,
            jnp.zeros((bp, B_CONV - 1, B_WIDTH), dtp), jnp.zeros((bp, IN_C), dtp),
            jnp.zeros((bp, C_HEADS, C_HEAD_DIM, C_HEAD_DIM), dtp),
            jnp.zeros((bp, F_CONV - 1, 2 * D_FF), dtp), p)
        for gi, (window, _) in enumerate(A_GROUPS):
            a_p[gi].append(kv_new[gi][:, tp - min(window, tp):])
        b_p.append(b_new)
        sh_p.append(sh_new)
        wkv_p.append(st_new)
        f_p.append(f_new)
        ctx_s, min_s = [], []
        for gi, (window, _) in enumerate(A_GROUPS):
            buf = a_caches[gi][l]
            n_rows = buf.shape[1]
            ctx_s.append(jnp.pad(buf, ((0, 0), (window - n_rows, 0), (0, 0), (0, 0), (0, 0))))
            min_s.append(window - n_rows)
        y_s, kv_new, b_new, sh_new, st_new, f_new = trunk_layer(
            y_s, pos_s, ctx_s, min_s, state_b_conv[l], state_c_shift[l], state_c_wkv[l],
            state_f_conv[l], p)
        for gi in range(N_A_GROUPS):
            a_s[gi].append(kv_new[gi])
        b_s.append(b_new)
        sh_s.append(sh_new)
        wkv_s.append(st_new)
        f_s.append(f_new)
    new_a_kv0_prompt = jnp.stack(a_p[0])
    new_a_kv1_prompt = jnp.stack(a_p[1])
    new_a_kv2_prompt = jnp.stack(a_p[2])
    new_a_kv0_sample = jnp.stack(a_s[0])
    new_a_kv1_sample = jnp.stack(a_s[1])
    new_a_kv2_sample = jnp.stack(a_s[2])
    return (y_p, y_s, new_a_kv0_prompt, new_a_kv1_prompt, new_a_kv2_prompt,
            new_a_kv0_sample, new_a_kv1_sample, new_a_kv2_sample,
            jnp.stack(b_p), jnp.stack(b_s), jnp.stack(sh_p), jnp.stack(sh_s),
            jnp.stack(wkv_p), jnp.stack(wkv_s), jnp.stack(f_p), jnp.stack(f_s))
```

```python
import functools
import math

import jax
import jax.numpy as jnp
from jax import lax
from jax.experimental import pallas as pl
from jax.experimental.pallas import tpu as pltpu

F32 = jnp.float32
BF16 = jnp.bfloat16

D_MODEL = 1024
PAST_LEN = 16384
A_GROUPS = ((128, 1), (512, 4), (2048, 16))
N_A_GROUPS = len(A_GROUPS)
A_HEAD_DIM = 32
A_HEADS = 4
A_WIDTH = A_HEADS * A_HEAD_DIM
ROPE_THETA = 10000.0
ATT_BLOCK = 128
B_WIDTH = 256
B_CONV = 31
C_HEAD_DIM = 64
C_WIDTH = 384
C_HEADS = 6
C_DECAY_RANK = 64
C_ICLR_RANK = 64
C_GATE_RANK = 128
DECAY_SCALE = math.exp(-0.5)
D_FF = 2816
F_CONV = 3
IN_A = N_A_GROUPS * 3 * A_WIDTH
IN_B = 2 * B_WIDTH
IN_C = 3 * C_WIDTH + C_DECAY_RANK + C_ICLR_RANK + C_GATE_RANK
IN_GATE = 3 * D_MODEL
IN_WIDTH = IN_A + IN_B + IN_C + IN_GATE
RMS_EPS = 1e-6
LN_EPS = 1e-5
GN_EPS = C_HEAD_DIM * 1e-5
NEG_BIG = -1e30

LANES = 128
SUBLANES = 8
VMEM_LIMIT = 56 * 1024 * 1024

RWKV_CHUNK = 64
FF_CHUNK = 256


def _cparams(*sem):
    return pltpu.CompilerParams(dimension_semantics=sem, vmem_limit_bytes=VMEM_LIMIT)


def _split_bf16(x, n):
    pieces = []
    rem = x
    for _ in range(n):
        p = rem.astype(BF16)
        pieces.append(p)
        rem = rem - p.astype(F32)
    return pieces


_NN = (((1,), (0,)), ((), ()))
_NT = (((1,), (1,)), ((), ()))


def _dg(a, b, dims):
    return lax.dot_general(a, b, dims, preferred_element_type=F32)


def _mm(a, b, dims=_NN, passes=1):
    if passes == 1:
        return _dg(a.astype(BF16), b.astype(BF16), dims)
    a_hi, a_lo = _split_bf16(a, 2)
    b_hi, b_lo = _split_bf16(b, 2)
    return _dg(a_hi, b_hi, dims) + (_dg(a_hi, b_lo, dims) + _dg(a_lo, b_hi, dims))


def _mm_exact_rhs(a, b_bf16, pieces=3):
    out = None
    for p in _split_bf16(a, pieces):
        t = _dg(p, b_bf16, _NN)
        out = t if out is None else out + t
    return out


def _rms(x, g):
    ms = jnp.mean(x * x, axis=-1, keepdims=True)
    return x * lax.rsqrt(ms + RMS_EPS) * g


def _sigmoid(x):
    return 1.0 / (1.0 + jnp.exp(-x))


def _rope_table_kernel(inv_ref, cos_ref, sin_ref, *, tm, pos0, nb):
    row = pl.program_id(0) * tm + lax.broadcasted_iota(jnp.int32, (tm, LANES), 0)
    pos = (pos0 + row // nb).astype(F32)
    ang = pos * inv_ref[...]
    lane = lax.broadcasted_iota(jnp.int32, (tm, LANES), 1)
    sign = jnp.where((lane % A_HEAD_DIM) < A_HEAD_DIM // 2, -1.0, 1.0)
    cos_ref[...] = jnp.cos(ang)
    sin_ref[...] = jnp.sin(ang) * sign


def _rope_tables(m, pos0, nb):
    half = A_HEAD_DIM // 2
    inv = ROPE_THETA ** (-(jnp.arange(half, dtype=F32) * 2.0 / A_HEAD_DIM))
    inv_lane = jnp.tile(inv, 2 * A_HEADS)[None, :]
    tm = min(m, 512)
    return pl.pallas_call(
        functools.partial(_rope_table_kernel, tm=tm, pos0=pos0, nb=nb),
        grid=(m // tm,),
        in_specs=[pl.BlockSpec((1, LANES), lambda i: (0, 0))],
        out_specs=[pl.BlockSpec((tm, LANES), lambda i: (i, 0))] * 2,
        out_shape=[jax.ShapeDtypeStruct((m, LANES), F32)] * 2,
        compiler_params=_cparams("parallel"),
        name="rope_tables",
    )(inv_lane)


def _in_proj_kernel(x_ref, g_ref, w_ref, cos_ref, sin_ref, qkv_ref, glu_ref, pc_ref, gate_ref):
    h = _rms(x_ref[...], g_ref[...]).astype(BF16)
    cos = cos_ref[...]
    sin = sin_ref[...]
    lane = lax.broadcasted_iota(jnp.int32, cos.shape, 1)
    first_half = (lane % A_HEAD_DIM) < A_HEAD_DIM // 2
    pa = _dg(h, w_ref[:, 0:IN_A], _NN)
    for blk in range(IN_A // LANES):
        y = pa[:, blk * LANES:(blk + 1) * LANES]
        if blk % 3 != 2:
            swapped = jnp.where(first_half, pltpu.roll(y, LANES - A_HEAD_DIM // 2, 1),
                                pltpu.roll(y, A_HEAD_DIM // 2, 1))
            y = y * cos + swapped * sin
        qkv_ref[:, blk * LANES:(blk + 1) * LANES] = y
    pb = _dg(h, w_ref[:, IN_A:IN_A + IN_B], _NN)
    glu_ref[...] = pb[:, :B_WIDTH] * _sigmoid(pb[:, B_WIDTH:])
    pc_ref[...] = _dg(h, w_ref[:, IN_A + IN_B:IN_A + IN_B + IN_C], _NN)
    gate_ref[...] = _sigmoid(_dg(h, w_ref[:, IN_A + IN_B + IN_C:], _NN))


def _in_proj(x, gain, w_bf16, cos, sin):
    m = x.shape[0]
    tm = min(m, 256)
    row = lambda i: (i, 0)
    fixed = lambda i: (0, 0)
    widths = (IN_A, B_WIDTH, IN_C, IN_GATE)
    return pl.pallas_call(
        _in_proj_kernel,
        grid=(m // tm,),
        in_specs=[pl.BlockSpec((tm, D_MODEL), row), pl.BlockSpec((1, D_MODEL), fixed),
                  pl.BlockSpec((D_MODEL, IN_WIDTH), fixed),
                  pl.BlockSpec((tm, LANES), row), pl.BlockSpec((tm, LANES), row)],
        out_specs=[pl.BlockSpec((tm, w), row) for w in widths],
        out_shape=[jax.ShapeDtypeStruct((m, w), F32) for w in widths],
        compiler_params=_cparams("parallel"),
        name="in_proj",
    )(x, gain, w_bf16, cos, sin)


def _softmax_heads(q, k_bf16, v_bf16, valid):
    lane_head = lax.broadcasted_iota(jnp.int32, q.shape, 1) // A_HEAD_DIM
    scale = 1.0 / math.sqrt(A_HEAD_DIM)
    o_all = jnp.zeros(q.shape, F32)
    lse_all = jnp.zeros(q.shape, F32)
    for h in range(A_HEADS):
        hm = lane_head == h
        qh = jnp.where(hm, q, 0.0).astype(BF16)
        s = _dg(qh, k_bf16, _NT) * scale
        s = jnp.where(valid, s, NEG_BIG)
        mx = jnp.max(s, axis=-1, keepdims=True)
        p = jnp.exp(s - mx)
        l = jnp.sum(p, axis=-1, keepdims=True)
        o = _dg(p.astype(BF16), v_bf16, _NN) / l
        o_all = jnp.where(hm, o, o_all)
        lse_all = jnp.where(hm, mx + jnp.log(l), lse_all)
    return o_all, lse_all


def _attn_prompt_kernel(q_ref, kp_ref, kc_ref, vp_ref, vc_ref, o_ref, lse_ref):
    i = pl.program_id(1)
    k = jnp.concatenate([kp_ref[...], kc_ref[...]], axis=0).astype(BF16)
    v = jnp.concatenate([vp_ref[...], vc_ref[...]], axis=0).astype(BF16)
    qi = lax.broadcasted_iota(jnp.int32, (ATT_BLOCK, 2 * ATT_BLOCK), 0)
    kj = lax.broadcasted_iota(jnp.int32, (ATT_BLOCK, 2 * ATT_BLOCK), 1)
    delta = qi + ATT_BLOCK - kj
    valid = (delta >= 0) & (delta <= ATT_BLOCK) & ((kj >= ATT_BLOCK) | (i > 0))
    o, lse = _softmax_heads(q_ref[...], k, v, valid)
    o_ref[...] = o
    lse_ref[...] = lse


def _attn_prompt(qkv, group):
    t = qkv.shape[0]
    _, d = A_GROUPS[group]
    nblk = t // (d * ATT_BLOCK)
    view = qkv.reshape(t // d, d * IN_A)
    cb = IN_A // LANES
    col = 3 * group
    blk = (ATT_BLOCK, LANES)
    cur = lambda off: (lambda r, i: (i, r * cb + col + off))
    prev = lambda off: (lambda r, i: (jnp.maximum(i - 1, 0), r * cb + col + off))
    o, lse = pl.pallas_call(
        _attn_prompt_kernel,
        grid=(d, nblk),
        in_specs=[pl.BlockSpec(blk, cur(0)), pl.BlockSpec(blk, prev(1)), pl.BlockSpec(blk, cur(1)),
                  pl.BlockSpec(blk, prev(2)), pl.BlockSpec(blk, cur(2))],
        out_specs=[pl.BlockSpec(blk, lambda r, i: (i, r))] * 2,
        out_shape=[jax.ShapeDtypeStruct((t // d, d * LANES), F32)] * 2,
        compiler_params=_cparams("parallel", "parallel"),
        name=f"attn_prompt_g{group}",
    )(view, view, view, view, view)
    return o.reshape(t, LANES), lse.reshape(t, LANES)


def _attn_sample_kernel(qkv_ref, c0_ref, c1_ref, c2_ref, o_ref, lse_ref, *, ts):
    qkv = qkv_ref[0]
    rows = qkv.shape[0]
    pad = jnp.zeros((LANES - rows, LANES), F32)
    for g, (window, d) in enumerate(A_GROUPS):
        base = 3 * A_WIDTH * g
        q = qkv[:, base:base + LANES]
        k_new = jnp.concatenate([qkv[:, base + LANES:base + 2 * LANES], pad], axis=0)
        v_new = jnp.concatenate([qkv[:, base + 2 * LANES:base + 3 * LANES], pad], axis=0)
        c_ref = (c0_ref, c1_ref, c2_ref)[g]
        if d == 1:
            kc = [c_ref[0, :, 0:LANES]]
            vc = [c_ref[0, :, LANES:2 * LANES]]
        else:
            kc = [c_ref[0, :, 2 * LANES * r:2 * LANES * r + LANES] for r in range(ts)]
            vc = [c_ref[0, :, 2 * LANES * r + LANES:2 * LANES * (r + 1)] for r in range(ts)]
        n_cache = len(kc) * kc[0].shape[0]
        k = jnp.concatenate(kc + [k_new], axis=0).astype(BF16)
        v = jnp.concatenate(vc + [v_new], axis=0).astype(BF16)
        n = n_cache + LANES
        qi = lax.broadcasted_iota(jnp.int32, (rows, n), 0)
        c = lax.broadcasted_iota(jnp.int32, (rows, n), 1)
        new = c - n_cache
        if d == 1:
            valid = ((c < n_cache) & (c >= qi)) | ((new >= 0) & (new <= qi))
        else:
            valid = ((c < n_cache) & ((c // kc[0].shape[0]) == qi)) | (new == qi)
        o, lse = _softmax_heads(q, k, v, valid)
        o_ref[0, :, g * LANES:(g + 1) * LANES] = o
        lse_ref[0, :, g * LANES:(g + 1) * LANES] = lse


def _attn_sample(qkv_bm, caches, ts):
    bsz, rows, _ = qkv_bm.shape
    views = []
    for (window, d), c in zip(A_GROUPS, caches):
        assert c.shape[1] == window and window // d == ATT_BLOCK and (d == 1 or d >= ts)
        views.append(c.reshape(bsz, window // d, d * 2 * A_WIDTH))
    specs = [pl.BlockSpec((1, rows, IN_A), lambda b: (b, 0, 0))]
    for (window, d), v in zip(A_GROUPS, views):
        width = 2 * A_WIDTH * (1 if d == 1 else ts)
        specs.append(pl.BlockSpec((1, window // d, width), lambda b: (b, 0, 0)))
    out = jax.ShapeDtypeStruct((bsz, rows, N_A_GROUPS * LANES), F32)
    return pl.pallas_call(
        functools.partial(_attn_sample_kernel, ts=ts),
        grid=(bsz,),
        in_specs=specs,
        out_specs=[pl.BlockSpec((1, rows, N_A_GROUPS * LANES), lambda b: (b, 0, 0))] * 2,
        out_shape=[out, out],
        compiler_params=_cparams("parallel"),
        name="attn_sample",
    )(qkv_bm, *views)


def _conv_b_body(ext_ref, w_ref, b_ref, g_ref, beta_ref, o_ref, *, tm, nb, base):
    acc = jnp.zeros((tm, B_WIDTH), F32) + b_ref[...]
    for j in range(B_CONV):
        acc = acc + w_ref[j:j + 1, :] * ext_ref[pl.ds(base + j * nb, tm), :]
    mu = jnp.mean(acc, axis=-1, keepdims=True)
    cen = acc - mu
    var = jnp.mean(cen * cen, axis=-1, keepdims=True)
    y = cen * lax.rsqrt(var + LN_EPS) * g_ref[...] + beta_ref[...]
    o_ref[...] = y * _sigmoid(y)


def _conv_b_prompt_kernel(halo_ref, cur_ref, w_ref, b_ref, g_ref, beta_ref, o_ref, ext_ref, *, tm, halo):
    ext_ref[0:halo, :] = jnp.where(pl.program_id(0) > 0, halo_ref[...], 0.0)
    ext_ref[halo:halo + tm, :] = cur_ref[...]
    _conv_b_body(ext_ref, w_ref, b_ref, g_ref, beta_ref, o_ref, tm=tm, nb=1, base=halo - (B_CONV - 1))


def _conv_b_sample_kernel(ctx_ref, cur_ref, w_ref, b_ref, g_ref, beta_ref, o_ref, ext_ref, *, tm, nb):
    n_ctx = (B_CONV - 1) * nb
    ext_ref[0:n_ctx, :] = ctx_ref[...]
    ext_ref[n_ctx:n_ctx + tm, :] = cur_ref[...]
    _conv_b_body(ext_ref, w_ref, b_ref, g_ref, beta_ref, o_ref, tm=tm, nb=nb, base=0)


def _conv_b(glu, ctx, w, b, g, beta, nb):
    m = glu.shape[0]
    fixed = lambda i: (0, 0)
    small = [pl.BlockSpec((B_CONV, B_WIDTH), fixed)] + [pl.BlockSpec((1, B_WIDTH), fixed)] * 3
    if ctx is None:
        tm = min(m, 512)
        halo = 32
        kern = functools.partial(_conv_b_prompt_kernel, tm=tm, halo=halo)
        first = pl.BlockSpec((halo, B_WIDTH), lambda i: (jnp.maximum(i * (tm // halo) - 1, 0), 0))
        lead, ext_rows = glu, halo + tm
    else:
        tm = m
        kern = functools.partial(_conv_b_sample_kernel, tm=tm, nb=nb)
        first = pl.BlockSpec(ctx.shape, fixed)
        lead, ext_rows = ctx, ctx.shape[0] + tm
    return pl.pallas_call(
        kern,
        grid=(m // tm,),
        in_specs=[first, pl.BlockSpec((tm, B_WIDTH), lambda i: (i, 0))] + small,
        out_specs=pl.BlockSpec((tm, B_WIDTH), lambda i: (i, 0)),
        out_shape=jax.ShapeDtypeStruct((m, B_WIDTH), F32),
        scratch_shapes=[pltpu.VMEM((ext_rows, B_WIDTH), F32)],
        compiler_params=_cparams("parallel"),
        name="conv_b",
    )(lead, glu, w, b, g, beta)


def _rwkv_pre_kernel(prev_ref, pc_ref, mu_ref, w2_ref, a2_ref, g2_ref, w0_ref, a0_ref, kkw_ref, ka_ref,
                     rk_ref, seg_ref, r_o, lw_o, k_o, v_o, kk_o, b_o, g_o, bonus_o, ext_ref,
                     *, tm, nb, off, zero_first):
    prev_rows = prev_ref[...]
    if zero_first:
        prev_rows = jnp.where(pl.program_id(0) > 0, prev_rows, 0.0)
    ext_ref[0:off, :] = prev_rows
    ext_ref[off:off + tm, :] = pc_ref[...]
    pc = pc_ref[...]
    prev = ext_ref[pl.ds(off - nb, tm), :]
    xs = pc + mu_ref[...] * (prev - pc)
    r = xs[:, 0:C_WIDTH]
    k = xs[:, C_WIDTH:2 * C_WIDTH]
    v = xs[:, 2 * C_WIDTH:3 * C_WIDTH]
    wa = xs[:, 3 * C_WIDTH:3 * C_WIDTH + LANES]
    gl = xs[:, 3 * C_WIDTH + LANES:]
    seg = seg_ref[...]
    lw = -DECAY_SCALE * _sigmoid(w0_ref[...] + _mm(jnp.tanh(wa), w2_ref[...], passes=3))
    a = _sigmoid(a0_ref[...] + _mm(wa, a2_ref[...], passes=3))
    g = _mm(_sigmoid(gl), g2_ref[...], passes=3)
    kk = k * kkw_ref[...]
    ss = _mm_exact_rhs(kk * kk, seg, 2)
    kk = kk * lax.rsqrt(jnp.maximum(ss, 1e-24))
    k2 = k * (1.0 + (a - 1.0) * ka_ref[...])
    r_o[...] = r
    lw_o[...] = lw
    k_o[...] = k2
    v_o[...] = v
    kk_o[...] = kk
    b_o[...] = kk * a
    g_o[...] = g
    bonus_o[...] = _mm_exact_rhs(r * k2 * rk_ref[...], seg, 2) * v


def _rwkv_pre(pc, ctx, p, nb):
    m = pc.shape[0]
    tm = min(m, 512)
    off = max(SUBLANES, nb)
    fixed = lambda i: (0, 0)
    row = lambda i: (i, 0)
    if ctx is None:
        lead = pc
        first = pl.BlockSpec((off, IN_C), lambda i: (jnp.maximum(i * (tm // off) - 1, 0), 0))
    else:
        assert m == tm
        lead = ctx
        first = pl.BlockSpec((off, IN_C), fixed)
    vec = pl.BlockSpec((1, C_WIDTH), fixed)
    out = jax.ShapeDtypeStruct((m, C_WIDTH), F32)
    return pl.pallas_call(
        functools.partial(_rwkv_pre_kernel, tm=tm, nb=nb, off=off, zero_first=ctx is None),
        grid=(m // tm,),
        in_specs=[first, pl.BlockSpec((tm, IN_C), row), pl.BlockSpec((1, IN_C), fixed),
                  pl.BlockSpec((LANES, C_WIDTH), fixed), pl.BlockSpec((LANES, C_WIDTH), fixed),
                  pl.BlockSpec((C_GATE_RANK, C_WIDTH), fixed), vec, vec, vec, vec, vec,
                  pl.BlockSpec((C_WIDTH, C_WIDTH), fixed)],
        out_specs=[pl.BlockSpec((tm, C_WIDTH), row)] * 8,
        out_shape=[out] * 8,
        scratch_shapes=[pltpu.VMEM((off + tm, IN_C), F32)],
        compiler_params=_cparams("parallel"),
        name="rwkv_pre",
    )(lead, pc, p['c_mu'], p['c_w2p'], p['c_a2p'], p['c_g2'], p['c_w0'], p['c_a0'], p['c_kk'], p['c_ka'],
      p['c_rk'], p['seg'])


RWKV_PASSES = 3


def _unit_lower_inverse(a_strict, c):
    ri = lax.broadcasted_iota(jnp.int32, (c, c), 0)
    ci = lax.broadcasted_iota(jnp.int32, (c, c), 1)
    inv = jnp.where(ri == ci, 1.0, 0.0).astype(F32)
    s = 1
    while s < c:
        same = (ri // (2 * s)) == (ci // (2 * s))
        off = jnp.where(same & ((ri % (2 * s)) >= s) & ((ci % (2 * s)) < s), a_strict, 0.0)
        inv = inv - _mm(inv, _mm(off, inv, passes=RWKV_PASSES), passes=RWKV_PASSES)
        s *= 2
    return inv


def _rwkv_chunk_kernel(r_ref, lw_ref, k_ref, v_ref, kk_ref, b_ref, g_ref, bonus_ref, s0_ref, gng_ref, gnb_ref,
                       o_ref, s_out_ref, state_ref, *, c):
    ci = pl.program_id(1)

    @pl.when(ci == 0)
    def _():
        state_ref[...] = s0_ref[0]

    ri = lax.broadcasted_iota(jnp.int32, (c, c), 0)
    cj = lax.broadcasted_iota(jnp.int32, (c, c), 1)
    incl = ri >= cj
    strict = ri > cj
    tri = jnp.where(incl, 1.0, 0.0).astype(BF16)

    lw_all = lw_ref[0]
    cum_all = _dg(tri, _split_bf16(lw_all, 3)[0], _NN)
    for piece in _split_bf16(lw_all, 3)[1:]:
        cum_all = cum_all + _dg(tri, piece, _NN)

    P = RWKV_PASSES
    for h in range(C_HEADS):
        sl = slice(h * C_HEAD_DIM, (h + 1) * C_HEAD_DIM)
        r = r_ref[0, :, sl]
        k = k_ref[0, :, sl]
        v = v_ref[0, :, sl]
        kk = kk_ref[0, :, sl]
        b = b_ref[0, :, sl]
        lw = lw_all[:, sl]
        cum = cum_all[:, sl]
        cend = cum[c - 1:c, :]
        w_inv = jnp.exp(-cum)
        w_end = jnp.exp(cend - cum)
        kap = kk * jnp.exp(cum - lw)
        rho = r * jnp.exp(cum)
        bt = b * w_inv
        kt = k * w_inv
        s0 = state_ref[h]

        a_kb = jnp.where(strict, _mm(kap, bt, _NT, P), 0.0)
        a_kk = jnp.where(strict, _mm(kap, kt, _NT, P), 0.0)
        a_rb = jnp.where(incl, _mm(rho, bt, _NT, P), 0.0)
        a_rk = jnp.where(incl, _mm(rho, kt, _NT, P), 0.0)
        t_inv = _unit_lower_inverse(a_kb, c)
        kap_p = _mm(t_inv, kap, passes=P)
        v_p = _mm(t_inv, _mm(a_kk, v, passes=P), passes=P)
        u = -_mm(kap_p, s0, _NT, P) - v_p
        s_new = (s0 * jnp.exp(cend) + _mm(u.T, b * w_end, passes=P) + _mm(v.T, k * w_end, passes=P))
        y = _mm(rho, s0, _NT, P) + _mm(a_rb, u, passes=P) + _mm(a_rk, v, passes=P)
        state_ref[h] = s_new

        mu = jnp.mean(y, axis=-1, keepdims=True)
        cen = y - mu
        var = jnp.mean(cen * cen, axis=-1, keepdims=True)
        yn = cen * lax.rsqrt(var + GN_EPS) * gng_ref[:, sl] + gnb_ref[:, sl]
        o_ref[0, :, sl] = (yn + bonus_ref[0, :, sl]) * g_ref[0, :, sl]

    @pl.when(ci == pl.num_programs(1) - 1)
    def _():
        s_out_ref[0] = state_ref[...]


def _rwkv_chunks(feats, s0, gn_g, gn_b):
    bsz, t, _ = feats[0].shape
    c = RWKV_CHUNK
    seq = pl.BlockSpec((1, c, C_WIDTH), lambda b, i: (b, i, 0))
    st = pl.BlockSpec((1, C_HEADS, C_HEAD_DIM, C_HEAD_DIM), lambda b, i: (b, 0, 0, 0))
    vec = pl.BlockSpec((1, C_WIDTH), lambda b, i: (0, 0))
    return pl.pallas_call(
        functools.partial(_rwkv_chunk_kernel, c=c),
        grid=(bsz, t // c),
        in_specs=[seq] * 8 + [st, vec, vec],
        out_specs=[seq, st],
        out_shape=[jax.ShapeDtypeStruct((bsz, t, C_WIDTH), F32),
                   jax.ShapeDtypeStruct((bsz, C_HEADS, C_HEAD_DIM, C_HEAD_DIM), F32)],
        scratch_shapes=[pltpu.VMEM((C_HEADS, C_HEAD_DIM, C_HEAD_DIM), F32)],
        compiler_params=_cparams("parallel", "arbitrary"),
        name="rwkv_chunks",
    )(*feats, s0, gn_g, gn_b)


def _merge_kernel(o0, l0, o1, l1, o2, l2, ob_ref, oc_ref, gate_ref, x_ref, wa_ref, wb_ref, wc_ref, wo_ref,
                  gain_ref, out_ref):
    ls = [l0[...], l1[...], l2[...]]
    mx = jnp.maximum(jnp.maximum(ls[0], ls[1]), ls[2])
    es = [jnp.exp(l - mx) for l in ls]
    den = es[0] + es[1] + es[2]
    o_a = (es[0] * o0[...] + es[1] * o1[...] + es[2] * o2[...]) / den
    merged = (gate_ref[:, 0:D_MODEL] * _mm(o_a, wa_ref[...])
              + gate_ref[:, D_MODEL:2 * D_MODEL] * _mm(ob_ref[...], wb_ref[...])
              + gate_ref[:, 2 * D_MODEL:] * _mm(oc_ref[...], wc_ref[...]))
    z = _mm(merged, wo_ref[...])
    out_ref[...] = x_ref[...] + _rms(z, gain_ref[...])


def _merge(att, o_b, o_c, gates, x, p):
    m = x.shape[0]
    tm = min(m, 512)
    row = lambda i: (i, 0)
    fixed = lambda i: (0, 0)
    rows = lambda w: pl.BlockSpec((tm, w), row)
    full = lambda a: pl.BlockSpec(a.shape, fixed)
    ws = (p['w_br_a'], p['w_br_b'], p['w_br_c'], p['w_out'], p['norm_mix_post'])
    return pl.pallas_call(
        _merge_kernel,
        grid=(m // tm,),
        in_specs=[rows(LANES)] * 6 + [rows(B_WIDTH), rows(C_WIDTH), rows(IN_GATE), rows(D_MODEL)]
                 + [full(a) for a in ws],
        out_specs=rows(D_MODEL),
        out_shape=jax.ShapeDtypeStruct((m, D_MODEL), F32),
        compiler_params=_cparams("parallel"),
        name="merge_out",
    )(*att, o_b, o_c, gates, x, *ws)


def _gelu_tanh(x):
    return 0.5 * x * (1.0 + jnp.tanh(math.sqrt(2.0 / math.pi) * (x + 0.044715 * (x * x * x))))


def _ffn_kernel(x_ref, g1_ref, fa_ref, fb_ref, wa_ref, wb_ref, ba_ref, bb_ref, ca_ref, cb_ref, fd_ref, g2_ref,
                out_ref, ta_ref, tb_ref, h_ref, acc_ref, exta_ref, extb_ref, cara_ref, carb_ref,
                *, tm, nb, cr):
    i = pl.program_id(0)
    j = pl.program_id(1)

    @pl.when(j == 0)
    def _():
        h_ref[...] = _rms(x_ref[...], g1_ref[...]).astype(BF16)
        acc_ref[...] = jnp.zeros_like(acc_ref)

    h = h_ref[...]
    cus = []
    for f_ref, w_ref, b_ref, c_ref, t_ref, ext_ref, car_ref in (
            (fa_ref, wa_ref, ba_ref, ca_ref, ta_ref, exta_ref, cara_ref),
            (fb_ref, wb_ref, bb_ref, cb_ref, tb_ref, extb_ref, carb_ref)):
        u = _dg(h, f_ref[...], _NN)

        @pl.when(i == 0)
        def _():
            ext_ref[0:cr, :] = c_ref[...]

        @pl.when(i > 0)
        def _():
            ext_ref[0:cr, :] = car_ref[j]

        ext_ref[cr:cr + tm, :] = u
        tail = u[tm - cr:, :]
        car_ref[j] = tail
        t_ref[...] = tail
        cus.append(w_ref[0:1, :] * ext_ref[pl.ds(cr - 2 * nb, tm), :]
                   + w_ref[1:2, :] * ext_ref[pl.ds(cr - nb, tm), :]
                   + w_ref[2:3, :] * u + b_ref[...])
    f = _gelu_tanh(cus[0]) * cus[1]
    acc_ref[...] += _dg(f.astype(BF16), fd_ref[...], _NN)

    @pl.when(j == pl.num_programs(1) - 1)
    def _():
        out_ref[...] = x_ref[...] + _rms(acc_ref[...], g2_ref[...])


def _ffn(x, ctx, p, nb):
    m = x.shape[0]
    tm = min(m, 1024)
    cr = ctx.shape[0]
    fc = FF_CHUNK
    nf = D_FF // fc
    row = lambda i, j: (i, 0)
    fixed = lambda i, j: (0, 0)
    lo = lambda i, j: (0, j)
    hi = lambda i, j: (0, nf + j)
    col = lambda rows: [pl.BlockSpec((rows, fc), lo), pl.BlockSpec((rows, fc), hi)]
    tail = jax.ShapeDtypeStruct(((m // tm) * cr, D_FF), F32)
    tail_spec = pl.BlockSpec((cr, fc), lambda i, j: (i, j))
    x2, ta, tb = pl.pallas_call(
        functools.partial(_ffn_kernel, tm=tm, nb=nb, cr=cr),
        grid=(m // tm, nf),
        in_specs=[pl.BlockSpec((tm, D_MODEL), row), pl.BlockSpec((1, D_MODEL), fixed)]
                 + col(D_MODEL) + col(F_CONV) + col(1) + col(cr)
                 + [pl.BlockSpec((fc, D_MODEL), lambda i, j: (j, 0)), pl.BlockSpec((1, D_MODEL), fixed)],
        out_specs=[pl.BlockSpec((tm, D_MODEL), row), tail_spec, tail_spec],
        out_shape=[jax.ShapeDtypeStruct((m, D_MODEL), F32), tail, tail],
        scratch_shapes=[pltpu.VMEM((tm, D_MODEL), BF16), pltpu.VMEM((tm, D_MODEL), F32),
                        pltpu.VMEM((cr + tm, fc), F32), pltpu.VMEM((cr + tm, fc), F32),
                        pltpu.VMEM((nf, cr, fc), F32), pltpu.VMEM((nf, cr, fc), F32)],
        compiler_params=_cparams("arbitrary", "arbitrary"),
        name="ffn",
    )(x, p['norm_ffn_pre'], p['f_up'], p['f_up'], p['f_dw_w'], p['f_dw_w'], p['f_dw_b'], p['f_dw_b'],
      ctx, ctx, p['f_down'], p['norm_ffn_post'])
    return x2, ta[-cr:], tb[-cr:]


def _to_time_major(a):
    a = jnp.swapaxes(a, 0, 1)
    return a.reshape((a.shape[0] * a.shape[1],) + a.shape[2:])


def _to_batch_major(a, ts):
    return jnp.swapaxes(a.reshape(ts, a.shape[0] // ts, a.shape[1]), 0, 1)


def _layer_prompt(x, tables, p):
    t = x.shape[0]
    qkv, glu, pc, gates = _in_proj(x, p['norm_mix_pre'], p['w_in'], *tables)
    att = []
    for g in range(N_A_GROUPS):
        att.extend(_attn_prompt(qkv, g))
    o_b = _conv_b(glu, None, p['b_dw_w'], p['b_dw_b'], p['b_ln_g'], p['b_ln_b'], 1)
    feats = _rwkv_pre(pc, None, p, 1)
    s0 = jnp.zeros((1, C_HEADS, C_HEAD_DIM, C_HEAD_DIM), F32)
    o_c, s_new = _rwkv_chunks([f[None] for f in feats], s0, p['c_gn_g'], p['c_gn_b'])
    x1 = _merge(att, o_b, o_c[0], gates, x, p)
    x2, ta, tb = _ffn(x1, jnp.zeros((SUBLANES, 2 * D_FF), F32), p, 1)
    kv = [qkv[t - min(w, t):, 3 * A_WIDTH * g + A_WIDTH:3 * A_WIDTH * (g + 1)]
          .reshape(1, min(w, t), 2, A_HEADS, A_HEAD_DIM) for g, (w, _) in enumerate(A_GROUPS)]
    f_tail = jnp.concatenate([ta, tb], axis=1)[-(F_CONV - 1):][None]
    return x2, kv, glu[t - (B_CONV - 1):][None], pc[t - 1:], s_new, f_tail


def _layer_sample(x, tables, caches, b_ctx, c_shift, c_state, f_ctx, p, ts):
    nb = x.shape[0] // ts
    qkv, glu, pc, gates = _in_proj(x, p['norm_mix_pre'], p['w_in'], *tables)
    qkv_bm = _to_batch_major(qkv, ts)
    qkv_pad = jnp.pad(qkv_bm, ((0, 0), (0, SUBLANES - ts), (0, 0)))
    o, lse = _attn_sample(qkv_pad, caches, ts)
    att = []
    for g in range(N_A_GROUPS):
        att.append(_to_time_major(o[:, :ts, g * LANES:(g + 1) * LANES]))
        att.append(_to_time_major(lse[:, :ts, g * LANES:(g + 1) * LANES]))
    o_b = _conv_b(glu, _to_time_major(b_ctx), p['b_dw_w'], p['b_dw_b'], p['b_ln_g'], p['b_ln_b'], nb)
    feats = _rwkv_pre(pc, c_shift, p, nb)
    feats_bm = [jnp.pad(_to_batch_major(f, ts), ((0, 0), (0, RWKV_CHUNK - ts), (0, 0))) for f in feats]
    o_c, s_new = _rwkv_chunks(feats_bm, c_state, p['c_gn_g'], p['c_gn_b'])
    x1 = _merge(att, o_b, _to_time_major(o_c[:, :ts]), gates, x, p)
    x2, ta, tb = _ffn(x1, _to_time_major(f_ctx), p, nb)
    kv = [_to_batch_major(qkv[:, 3 * A_WIDTH * g + A_WIDTH:3 * A_WIDTH * (g + 1)], ts)
          .reshape(nb, ts, 2, A_HEADS, A_HEAD_DIM) for g in range(N_A_GROUPS)]
    b_new = jnp.concatenate([b_ctx, _to_batch_major(glu, ts)], axis=1)[:, -(B_CONV - 1):]
    f_new = jnp.concatenate([f_ctx, _to_batch_major(jnp.concatenate([ta, tb], axis=1), F_CONV - 1)], axis=1)
    return x2, kv, b_new, pc[(ts - 1) * nb:], s_new, f_new[:, -(F_CONV - 1):]


def kernel(x_prompt, x_sample, cache_a_kv0, cache_a_kv1, cache_a_kv2, state_b_conv, state_c_shift, state_c_wkv, state_f_conv, norm_mix_pre, norm_mix_post, norm_ffn_pre, norm_ffn_post, w_in, b_dw_w, b_dw_b, b_ln_g, b_ln_b, c_mu, c_w0, c_w2, c_a0, c_a2, c_g2, c_kk, c_ka, c_rk, c_gn_g, c_gn_b, w_br_a, w_br_b, w_br_c, w_out, f_up, f_dw_w, f_dw_b, f_down):
    depth = w_in.shape[0]
    bp, tp, _ = x_prompt.shape
    bs, ts, _ = x_sample.shape
    assert bp == 1 and ts <= SUBLANES
    caches = (cache_a_kv0, cache_a_kv1, cache_a_kv2)

    head_id = jnp.arange(C_WIDTH) // C_HEAD_DIM
    seg = (head_id[:, None] == head_id[None, :]).astype(BF16)
    zpad = jnp.zeros((LANES - C_DECAY_RANK, C_WIDTH), F32)

    tab_p = _rope_tables(tp, 0, 1)
    tab_s = _rope_tables(ts * bs, PAST_LEN, bs)

    y_p = x_prompt[0]
    y_s = _to_time_major(x_sample)
    outs_p, outs_s = [], []
    for l in range(depth):
        vec = lambda a: a[l][None, :]
        p = {
            'norm_mix_pre': vec(norm_mix_pre), 'norm_mix_post': vec(norm_mix_post),
            'norm_ffn_pre': vec(norm_ffn_pre), 'norm_ffn_post': vec(norm_ffn_post),
            'w_in': w_in[l].astype(BF16),
            'b_dw_w': b_dw_w[l], 'b_dw_b': vec(b_dw_b), 'b_ln_g': vec(b_ln_g), 'b_ln_b': vec(b_ln_b),
            'c_mu': vec(c_mu), 'c_w0': vec(c_w0), 'c_a0': vec(c_a0),
            'c_w2p': jnp.concatenate([c_w2[l], zpad], axis=0),
            'c_a2p': jnp.concatenate([zpad, c_a2[l]], axis=0),
            'c_g2': c_g2[l], 'c_kk': vec(c_kk), 'c_ka': vec(c_ka),
            'c_rk': c_rk[l].reshape(1, C_WIDTH), 'c_gn_g': vec(c_gn_g), 'c_gn_b': vec(c_gn_b),
            'seg': seg,
            'w_br_a': w_br_a[l].astype(BF16), 'w_br_b': w_br_b[l].astype(BF16),
            'w_br_c': w_br_c[l].astype(BF16), 'w_out': w_out[l].astype(BF16),
            'f_up': f_up[l].astype(BF16), 'f_dw_w': f_dw_w[l], 'f_dw_b': vec(f_dw_b),
            'f_down': f_down[l].astype(BF16),
        }
        y_p, *rest_p = _layer_prompt(y_p, tab_p, p)
        outs_p.append(rest_p)
        y_s, *rest_s = _layer_sample(y_s, tab_s, [c[l] for c in caches], state_b_conv[l], state_c_shift[l],
                                     state_c_wkv[l], state_f_conv[l], p, ts)
        outs_s.append(rest_s)

    stack = lambda outs, f: jnp.stack([f(o) for o in outs])
    res = [y_p[None], _to_batch_major(y_s, ts)]
    res += [stack(outs_p, lambda o, g=g: o[0][g]) for g in range(N_A_GROUPS)]
    res += [stack(outs_s, lambda o, g=g: o[0][g]) for g in range(N_A_GROUPS)]
    for idx in range(1, 5):
        res += [stack(outs_p, lambda o: o[idx]), stack(outs_s, lambda o: o[idx])]
    return tuple(res)
```

```python
import functools
import math

import jax
import jax.numpy as jnp
from jax import lax
from jax.experimental import pallas as pl
from jax.experimental.pallas import tpu as pltpu

F32 = jnp.float32
BF16 = jnp.bfloat16

D_MODEL = 1024
PAST_LEN = 16384
A_GROUPS = ((128, 1), (512, 4), (2048, 16))
N_A_GROUPS = len(A_GROUPS)
A_HEAD_DIM = 32
A_HEADS = 4
A_WIDTH = A_HEADS * A_HEAD_DIM
ROPE_THETA = 10000.0
ATT_BLOCK = 128
B_WIDTH = 256
B_CONV = 31
C_HEAD_DIM = 64
C_WIDTH = 384
C_HEADS = 6
C_DECAY_RANK = 64
C_ICLR_RANK = 64
C_GATE_RANK = 128
DECAY_SCALE = math.exp(-0.5)
D_FF = 2816
F_CONV = 3
IN_A = N_A_GROUPS * 3 * A_WIDTH
IN_B = 2 * B_WIDTH
IN_C = 3 * C_WIDTH + C_DECAY_RANK + C_ICLR_RANK + C_GATE_RANK
IN_GATE = 3 * D_MODEL
IN_WIDTH = IN_A + IN_B + IN_C + IN_GATE
RMS_EPS = 1e-6
LN_EPS = 1e-5
GN_EPS = C_HEAD_DIM * 1e-5
NEG_BIG = -1e30

LANES = 128
SUBLANES = 8
VMEM_LIMIT = 56 * 1024 * 1024

RWKV_CHUNK = 64
FF_CHUNK = 256


def _cparams(*sem):
    return pltpu.CompilerParams(dimension_semantics=sem, vmem_limit_bytes=VMEM_LIMIT)


def _split_bf16(x, n):
    pieces = []
    rem = x
    for _ in range(n):
        p = rem.astype(BF16)
        pieces.append(p)
        rem = rem - p.astype(F32)
    return pieces


_NN = (((1,), (0,)), ((), ()))
_NT = (((1,), (1,)), ((), ()))


def _dg(a, b, dims):
    return lax.dot_general(a, b, dims, preferred_element_type=F32)


def _mm(a, b, dims=_NN, passes=1):
    if passes == 1:
        return _dg(a.astype(BF16), b.astype(BF16), dims)
    a_hi, a_lo = _split_bf16(a, 2)
    b_hi, b_lo = _split_bf16(b, 2)
    return _dg(a_hi, b_hi, dims) + (_dg(a_hi, b_lo, dims) + _dg(a_lo, b_hi, dims))


def _mm_exact_rhs(a, b_bf16, pieces=3):
    out = None
    for p in _split_bf16(a, pieces):
        t = _dg(p, b_bf16, _NN)
        out = t if out is None else out + t
    return out


def _mm_exact_rhs_left(a_bf16, b, pieces=3):
    out = None
    for p in _split_bf16(b, pieces):
        t = _dg(a_bf16, p, _NN)
        out = t if out is None else out + t
    return out


def _rms(x, g):
    ms = jnp.mean(x * x, axis=-1, keepdims=True)
    return x * lax.rsqrt(ms + RMS_EPS) * g


def _sigmoid(x):
    return 1.0 / (1.0 + jnp.exp(-x))


def _rope_table_kernel(inv_ref, cos_ref, sin_ref, *, tm, pos0, nb):
    row = pl.program_id(0) * tm + lax.broadcasted_iota(jnp.int32, (tm, LANES), 0)
    pos = (pos0 + row // nb).astype(F32)
    ang = pos * inv_ref[...]
    lane = lax.broadcasted_iota(jnp.int32, (tm, LANES), 1)
    sign = jnp.where((lane % A_HEAD_DIM) < A_HEAD_DIM // 2, -1.0, 1.0)
    cos_ref[...] = jnp.cos(ang)
    sin_ref[...] = jnp.sin(ang) * sign


def _rope_tables(m, pos0, nb):
    half = A_HEAD_DIM // 2
    inv = ROPE_THETA ** (-(jnp.arange(half, dtype=F32) * 2.0 / A_HEAD_DIM))
    inv_lane = jnp.tile(inv, 2 * A_HEADS)[None, :]
    tm = min(m, 512)
    return pl.pallas_call(
        functools.partial(_rope_table_kernel, tm=tm, pos0=pos0, nb=nb),
        grid=(m // tm,),
        in_specs=[pl.BlockSpec((1, LANES), lambda i: (0, 0))],
        out_specs=[pl.BlockSpec((tm, LANES), lambda i: (i, 0))] * 2,
        out_shape=[jax.ShapeDtypeStruct((m, LANES), F32)] * 2,
        compiler_params=_cparams("parallel"),
        name="rope_tables",
    )(inv_lane)


def _in_proj_kernel(x_ref, g_ref, w_ref, cos_ref, sin_ref, qkv_ref, glu_ref, pc_ref, gate_ref):
    h = _rms(x_ref[...], g_ref[...]).astype(BF16)
    cos = cos_ref[...]
    sin = sin_ref[...]
    lane = lax.broadcasted_iota(jnp.int32, cos.shape, 1)
    first_half = (lane % A_HEAD_DIM) < A_HEAD_DIM // 2
    pa = _dg(h, w_ref[:, 0:IN_A], _NN)
    for blk in range(IN_A // LANES):
        y = pa[:, blk * LANES:(blk + 1) * LANES]
        if blk % 3 != 2:
            swapped = jnp.where(first_half, pltpu.roll(y, LANES - A_HEAD_DIM // 2, 1),
                                pltpu.roll(y, A_HEAD_DIM // 2, 1))
            y = y * cos + swapped * sin
        qkv_ref[:, blk * LANES:(blk + 1) * LANES] = y
    pb = _dg(h, w_ref[:, IN_A:IN_A + IN_B], _NN)
    glu_ref[...] = pb[:, :B_WIDTH] * _sigmoid(pb[:, B_WIDTH:])
    pc_ref[...] = _dg(h, w_ref[:, IN_A + IN_B:IN_A + IN_B + IN_C], _NN)
    gate_ref[...] = _sigmoid(_dg(h, w_ref[:, IN_A + IN_B + IN_C:], _NN))


def _in_proj(x, gain, w_bf16, cos, sin):
    m = x.shape[0]
    tm = min(m, 256)
    row = lambda i: (i, 0)
    fixed = lambda i: (0, 0)
    widths = (IN_A, B_WIDTH, IN_C, IN_GATE)
    return pl.pallas_call(
        _in_proj_kernel,
        grid=(m // tm,),
        in_specs=[pl.BlockSpec((tm, D_MODEL), row), pl.BlockSpec((1, D_MODEL), fixed),
                  pl.BlockSpec((D_MODEL, IN_WIDTH), fixed),
                  pl.BlockSpec((tm, LANES), row), pl.BlockSpec((tm, LANES), row)],
        out_specs=[pl.BlockSpec((tm, w), row) for w in widths],
        out_shape=[jax.ShapeDtypeStruct((m, w), F32) for w in widths],
        compiler_params=_cparams("parallel"),
        name="in_proj",
    )(x, gain, w_bf16, cos, sin)


def _softmax_heads(q, k_bf16, v_bf16, valid):
    lane_head = lax.broadcasted_iota(jnp.int32, q.shape, 1) // A_HEAD_DIM
    scale = 1.0 / math.sqrt(A_HEAD_DIM)
    o_all = jnp.zeros(q.shape, F32)
    lse_all = jnp.zeros(q.shape, F32)
    for h in range(A_HEADS):
        hm = lane_head == h
        qh = jnp.where(hm, q, 0.0).astype(BF16)
        s = _dg(qh, k_bf16, _NT) * scale
        s = jnp.where(valid, s, NEG_BIG)
        mx = jnp.max(s, axis=-1, keepdims=True)
        p = jnp.exp(s - mx)
        l = jnp.sum(p, axis=-1, keepdims=True)
        o = _dg(p.astype(BF16), v_bf16, _NN) / l
        o_all = jnp.where(hm, o, o_all)
        lse_all = jnp.where(hm, mx + jnp.log(l), lse_all)
    return o_all, lse_all


def _attn_prompt_kernel(q_ref, kp_ref, kc_ref, vp_ref, vc_ref, o_ref, lse_ref):
    i = pl.program_id(1)
    k = jnp.concatenate([kp_ref[...], kc_ref[...]], axis=0).astype(BF16)
    v = jnp.concatenate([vp_ref[...], vc_ref[...]], axis=0).astype(BF16)
    qi = lax.broadcasted_iota(jnp.int32, (ATT_BLOCK, 2 * ATT_BLOCK), 0)
    kj = lax.broadcasted_iota(jnp.int32, (ATT_BLOCK, 2 * ATT_BLOCK), 1)
    delta = qi + ATT_BLOCK - kj
    valid = (delta >= 0) & (delta <= ATT_BLOCK) & ((kj >= ATT_BLOCK) | (i > 0))
    o, lse = _softmax_heads(q_ref[...], k, v, valid)
    o_ref[...] = o
    lse_ref[...] = lse


def _attn_prompt(qkv, group):
    t = qkv.shape[0]
    _, d = A_GROUPS[group]
    nblk = t // (d * ATT_BLOCK)
    view = qkv.reshape(t // d, d * IN_A)
    cb = IN_A // LANES
    col = 3 * group
    blk = (ATT_BLOCK, LANES)
    cur = lambda off: (lambda r, i: (i, r * cb + col + off))
    prev = lambda off: (lambda r, i: (jnp.maximum(i - 1, 0), r * cb + col + off))
    o, lse = pl.pallas_call(
        _attn_prompt_kernel,
        grid=(d, nblk),
        in_specs=[pl.BlockSpec(blk, cur(0)), pl.BlockSpec(blk, prev(1)), pl.BlockSpec(blk, cur(1)),
                  pl.BlockSpec(blk, prev(2)), pl.BlockSpec(blk, cur(2))],
        out_specs=[pl.BlockSpec(blk, lambda r, i: (i, r))] * 2,
        out_shape=[jax.ShapeDtypeStruct((t // d, d * LANES), F32)] * 2,
        compiler_params=_cparams("parallel", "parallel"),
        name=f"attn_prompt_g{group}",
    )(view, view, view, view, view)
    return o.reshape(t, LANES), lse.reshape(t, LANES)


def _attn_sample_kernel(qkv_ref, c0_ref, c1_ref, c2_ref, o_ref, lse_ref, *, ts):
    qkv = qkv_ref[0]
    rows = qkv.shape[0]
    pad = jnp.zeros((LANES - rows, LANES), F32)
    for g, (window, d) in enumerate(A_GROUPS):
        base = 3 * A_WIDTH * g
        q = qkv[:, base:base + LANES]
        k_new = jnp.concatenate([qkv[:, base + LANES:base + 2 * LANES], pad], axis=0)
        v_new = jnp.concatenate([qkv[:, base + 2 * LANES:base + 3 * LANES], pad], axis=0)
        c_ref = (c0_ref, c1_ref, c2_ref)[g]
        if d == 1:
            kc = [c_ref[0, :, 0:LANES]]
            vc = [c_ref[0, :, LANES:2 * LANES]]
        else:
            kc = [c_ref[0, :, 2 * LANES * r:2 * LANES * r + LANES] for r in range(ts)]
            vc = [c_ref[0, :, 2 * LANES * r + LANES:2 * LANES * (r + 1)] for r in range(ts)]
        n_cache = len(kc) * kc[0].shape[0]
        k = jnp.concatenate(kc + [k_new], axis=0).astype(BF16)
        v = jnp.concatenate(vc + [v_new], axis=0).astype(BF16)
        n = n_cache + LANES
        qi = lax.broadcasted_iota(jnp.int32, (rows, n), 0)
        c = lax.broadcasted_iota(jnp.int32, (rows, n), 1)
        new = c - n_cache
        if d == 1:
            valid = ((c < n_cache) & (c >= qi)) | ((new >= 0) & (new <= qi))
        else:
            valid = ((c < n_cache) & ((c // kc[0].shape[0]) == qi)) | (new == qi)
        o, lse = _softmax_heads(q, k, v, valid)
        o_ref[0, :, g * LANES:(g + 1) * LANES] = o
        lse_ref[0, :, g * LANES:(g + 1) * LANES] = lse


def _attn_sample(qkv_bm, caches, ts):
    bsz, rows, _ = qkv_bm.shape
    views = []
    for (window, d), c in zip(A_GROUPS, caches):
        assert c.shape[1] == window and window // d == ATT_BLOCK and (d == 1 or d >= ts)
        views.append(c.reshape(bsz, window // d, d * 2 * A_WIDTH))
    specs = [pl.BlockSpec((1, rows, IN_A), lambda b: (b, 0, 0))]
    for (window, d), v in zip(A_GROUPS, views):
        width = 2 * A_WIDTH * (1 if d == 1 else ts)
        specs.append(pl.BlockSpec((1, window // d, width), lambda b: (b, 0, 0)))
    out = jax.ShapeDtypeStruct((bsz, rows, N_A_GROUPS * LANES), F32)
    return pl.pallas_call(
        functools.partial(_attn_sample_kernel, ts=ts),
        grid=(bsz,),
        in_specs=specs,
        out_specs=[pl.BlockSpec((1, rows, N_A_GROUPS * LANES), lambda b: (b, 0, 0))] * 2,
        out_shape=[out, out],
        compiler_params=_cparams("parallel"),
        name="attn_sample",
    )(qkv_bm, *views)


def _conv_b_body(ext_ref, w_ref, b_ref, g_ref, beta_ref, o_ref, *, tm, nb, base):
    acc = jnp.zeros((tm, B_WIDTH), F32) + b_ref[...]
    for j in range(B_CONV):
        acc = acc + w_ref[j:j + 1, :] * ext_ref[pl.ds(base + j * nb, tm), :]
    mu = jnp.mean(acc, axis=-1, keepdims=True)
    cen = acc - mu
    var = jnp.mean(cen * cen, axis=-1, keepdims=True)
    y = cen * lax.rsqrt(var + LN_EPS) * g_ref[...] + beta_ref[...]
    o_ref[...] = y * _sigmoid(y)


def _conv_b_prompt_kernel(halo_ref, cur_ref, w_ref, b_ref, g_ref, beta_ref, o_ref, ext_ref, *, tm, halo):
    ext_ref[0:halo, :] = jnp.where(pl.program_id(0) > 0, halo_ref[...], 0.0)
    ext_ref[halo:halo + tm, :] = cur_ref[...]
    _conv_b_body(ext_ref, w_ref, b_ref, g_ref, beta_ref, o_ref, tm=tm, nb=1, base=halo - (B_CONV - 1))


def _conv_b_sample_kernel(ctx_ref, cur_ref, w_ref, b_ref, g_ref, beta_ref, o_ref, ext_ref, *, tm, nb):
    n_ctx = (B_CONV - 1) * nb
    ext_ref[0:n_ctx, :] = ctx_ref[...]
    ext_ref[n_ctx:n_ctx + tm, :] = cur_ref[...]
    _conv_b_body(ext_ref, w_ref, b_ref, g_ref, beta_ref, o_ref, tm=tm, nb=nb, base=0)


def _conv_b(glu, ctx, w, b, g, beta, nb):
    m = glu.shape[0]
    fixed = lambda i: (0, 0)
    small = [pl.BlockSpec((B_CONV, B_WIDTH), fixed)] + [pl.BlockSpec((1, B_WIDTH), fixed)] * 3
    if ctx is None:
        tm = min(m, 512)
        halo = 32
        kern = functools.partial(_conv_b_prompt_kernel, tm=tm, halo=halo)
        first = pl.BlockSpec((halo, B_WIDTH), lambda i: (jnp.maximum(i * (tm // halo) - 1, 0), 0))
        lead, ext_rows = glu, halo + tm
    else:
        tm = m
        kern = functools.partial(_conv_b_sample_kernel, tm=tm, nb=nb)
        first = pl.BlockSpec(ctx.shape, fixed)
        lead, ext_rows = ctx, ctx.shape[0] + tm
    return pl.pallas_call(
        kern,
        grid=(m // tm,),
        in_specs=[first, pl.BlockSpec((tm, B_WIDTH), lambda i: (i, 0))] + small,
        out_specs=pl.BlockSpec((tm, B_WIDTH), lambda i: (i, 0)),
        out_shape=jax.ShapeDtypeStruct((m, B_WIDTH), F32),
        scratch_shapes=[pltpu.VMEM((ext_rows, B_WIDTH), F32)],
        compiler_params=_cparams("parallel"),
        name="conv_b",
    )(lead, glu, w, b, g, beta)


def _rwkv_pre_kernel(prev_ref, pc_ref, mu_ref, w2_ref, a2_ref, g2_ref, w0_ref, a0_ref, kkw_ref, ka_ref,
                     rk_ref, seg_ref, r_o, lw_o, k_o, v_o, kk_o, b_o, g_o, bonus_o, ext_ref,
                     *, tm, nb, off, zero_first):
    prev_rows = prev_ref[...]
    if zero_first:
        prev_rows = jnp.where(pl.program_id(0) > 0, prev_rows, 0.0)
    ext_ref[0:off, :] = prev_rows
    ext_ref[off:off + tm, :] = pc_ref[...]
    pc = pc_ref[...]
    prev = ext_ref[pl.ds(off - nb, tm), :]
    xs = pc + mu_ref[...] * (prev - pc)
    r = xs[:, 0:C_WIDTH]
    k = xs[:, C_WIDTH:2 * C_WIDTH]
    v = xs[:, 2 * C_WIDTH:3 * C_WIDTH]
    wa = xs[:, 3 * C_WIDTH:3 * C_WIDTH + LANES]
    gl = xs[:, 3 * C_WIDTH + LANES:]
    seg = seg_ref[...]
    lw = -DECAY_SCALE * _sigmoid(w0_ref[...] + _mm(jnp.tanh(wa), w2_ref[...], passes=3))
    a = _sigmoid(a0_ref[...] + _mm(wa, a2_ref[...], passes=3))
    g = _mm(_sigmoid(gl), g2_ref[...], passes=3)
    kk = k * kkw_ref[...]
    ss = _mm_exact_rhs(kk * kk, seg, 2)
    kk = kk * lax.rsqrt(jnp.maximum(ss, 1e-24))
    k2 = k * (1.0 + (a - 1.0) * ka_ref[...])
    r_o[...] = r
    lw_o[...] = lw
    k_o[...] = k2
    v_o[...] = v
    kk_o[...] = kk
    b_o[...] = kk * a
    g_o[...] = g
    bonus_o[...] = _mm_exact_rhs(r * k2 * rk_ref[...], seg, 2) * v


def _rwkv_pre(pc, ctx, p, nb):
    m = pc.shape[0]
    tm = min(m, 512)
    off = max(SUBLANES, nb)
    fixed = lambda i: (0, 0)
    row = lambda i: (i, 0)
    if ctx is None:
        lead = pc
        first = pl.BlockSpec((off, IN_C), lambda i: (jnp.maximum(i * (tm // off) - 1, 0), 0))
    else:
        assert m == tm
        lead = ctx
        first = pl.BlockSpec((off, IN_C), fixed)
    vec = pl.BlockSpec((1, C_WIDTH), fixed)
    out = jax.ShapeDtypeStruct((m, C_WIDTH), F32)
    return pl.pallas_call(
        functools.partial(_rwkv_pre_kernel, tm=tm, nb=nb, off=off, zero_first=ctx is None),
        grid=(m // tm,),
        in_specs=[first, pl.BlockSpec((tm, IN_C), row), pl.BlockSpec((1, IN_C), fixed),
                  pl.BlockSpec((LANES, C_WIDTH), fixed), pl.BlockSpec((LANES, C_WIDTH), fixed),
                  pl.BlockSpec((C_GATE_RANK, C_WIDTH), fixed), vec, vec, vec, vec, vec,
                  pl.BlockSpec((C_WIDTH, C_WIDTH), fixed)],
        out_specs=[pl.BlockSpec((tm, C_WIDTH), row)] * 8,
        out_shape=[out] * 8,
        scratch_shapes=[pltpu.VMEM((off + tm, IN_C), F32)],
        compiler_params=_cparams("parallel"),
        name="rwkv_pre",
    )(lead, pc, p['c_mu'], p['c_w2p'], p['c_a2p'], p['c_g2'], p['c_w0'], p['c_a0'], p['c_kk'], p['c_ka'],
      p['c_rk'], p['seg'])


RWKV_PASSES = 1


_BNN = (((2,), (1,)), ((0,), (0,)))
_BNT = (((2,), (2,)), ((0,), (0,)))
_BTN = (((1,), (1,)), ((0,), (0,)))


def _unit_lower_inverse(a_strict):
    g, c, _ = a_strict.shape
    ri = lax.broadcasted_iota(jnp.int32, (c, c), 0)
    ci = lax.broadcasted_iota(jnp.int32, (c, c), 1)
    inv = jnp.broadcast_to(jnp.where(ri == ci, 1.0, 0.0).astype(F32), (g, c, c))
    s = 1
    while s < c:
        same = (ri // (2 * s)) == (ci // (2 * s))
        off = jnp.where(same & ((ri % (2 * s)) >= s) & ((ci % (2 * s)) < s), a_strict, 0.0)
        inv = inv - _mm(inv, _mm(off, inv, _BNN, RWKV_PASSES), _BNN, RWKV_PASSES)
        s *= 2
    return inv


def _rwkv_chunk_kernel(r_ref, lw_ref, k_ref, v_ref, kk_ref, b_ref, g_ref, bonus_ref, s0_ref, gng_ref, gnb_ref,
                       o_ref, s_out_ref, state_ref, *, c, nc):
    bb = r_ref.shape[0]
    n = bb * C_HEADS
    ci = pl.program_id(1)

    @pl.when(ci == 0)
    def _():
        state_ref[...] = s0_ref[...].reshape(n, C_HEAD_DIM, C_HEAD_DIM)

    ri = lax.broadcasted_iota(jnp.int32, (c, c), 0)
    cj = lax.broadcasted_iota(jnp.int32, (c, c), 1)
    incl = ri >= cj
    strict = ri > cj
    tri = jnp.where(incl, 1.0, 0.0).astype(BF16)

    groups = []
    for ch in range(nc):
        rows = slice(ch * c, (ch + 1) * c)
        for b in range(bb):
            lw = lw_ref[b, rows, :]
            cum = _mm_exact_rhs_left(tri, lw)
            cend = cum[c - 1:c, :]
            w_inv = jnp.exp(-cum)
            w_end = jnp.exp(cend - cum)
            kvec = k_ref[b, rows, :]
            bvec = b_ref[b, rows, :]
            groups.append((kk_ref[b, rows, :] * jnp.exp(cum - lw), r_ref[b, rows, :] * jnp.exp(cum),
                           bvec * w_inv, kvec * w_inv, bvec * w_end, kvec * w_end, v_ref[b, rows, :],
                           jnp.exp(cend), bonus_ref[b, rows, :], g_ref[b, rows, :]))

    def heads(idx):
        return jnp.stack([grp[idx][:, h * C_HEAD_DIM:(h + 1) * C_HEAD_DIM]
                          for grp in groups for h in range(C_HEADS)])

    P = RWKV_PASSES
    kap, rho, bt, kt, bend, kend, v, wc = [heads(i) for i in range(8)]
    lhs2 = jnp.concatenate([kap, rho], axis=1)
    xb = _mm(lhs2, bt, _BNT, P)
    xk = _mm(lhs2, kt, _BNT, P)
    a_kb = jnp.where(strict, xb[:, :c], 0.0)
    a_rb = jnp.where(incl, xb[:, c:], 0.0)
    a_kk = jnp.where(strict, xk[:, :c], 0.0)
    a_rk = jnp.where(incl, xk[:, c:], 0.0)
    t_inv = _unit_lower_inverse(a_kb)
    kap_p = _mm(t_inv, kap, _BNN, P)
    v_p = _mm(t_inv, _mm(a_kk, v, _BNN, P), _BNN, P)
    er = lax.broadcasted_iota(jnp.int32, (C_HEAD_DIM, C_HEAD_DIM), 0)
    ec = lax.broadcasted_iota(jnp.int32, (C_HEAD_DIM, C_HEAD_DIM), 1)
    pm = jnp.where(er == ec, wc, 0.0) - _mm(kap_p, bend, _BTN, P)
    q = _mm(v, kend, _BTN, P) - _mm(v_p, bend, _BTN, P)
    rp = rho - _mm(a_rb, kap_p, _BNN, P)
    y0 = _mm(a_rk, v, _BNN, P) - _mm(a_rb, v_p, _BNN, P)

    s = state_ref[...]
    ys = []
    for ch in range(nc):
        sl = slice(ch * n, (ch + 1) * n)
        ys.append(_mm(rp[sl], s, _BNT, P) + y0[sl])
        s = _mm(s, pm[sl], _BNN, P) + q[sl]
    state_ref[...] = s
    y = jnp.concatenate(ys, axis=0) if nc > 1 else ys[0]

    mu = jnp.mean(y, axis=-1, keepdims=True)
    cen = y - mu
    var = jnp.mean(cen * cen, axis=-1, keepdims=True)
    out = cen * lax.rsqrt(var + GN_EPS)
    bonus = heads(8)
    gate = heads(9)
    idx = 0
    for ch in range(nc):
        for b in range(bb):
            for h in range(C_HEADS):
                sl = slice(h * C_HEAD_DIM, (h + 1) * C_HEAD_DIM)
                yn = out[idx] * gng_ref[:, sl] + gnb_ref[:, sl]
                o_ref[b, ch * c:(ch + 1) * c, sl] = (yn + bonus[idx]) * gate[idx]
                idx += 1

    @pl.when(ci == pl.num_programs(1) - 1)
    def _():
        s_out_ref[...] = s.reshape(bb, C_HEADS, C_HEAD_DIM, C_HEAD_DIM)


def _rwkv_chunks(feats, s0, gn_g, gn_b):
    bsz, t, _ = feats[0].shape
    c = RWKV_CHUNK
    bb = 4 if bsz % 4 == 0 else 1
    nc = max(1, min(4 // bb, t // c))
    seq = pl.BlockSpec((bb, nc * c, C_WIDTH), lambda b, i: (b, i, 0))
    st = pl.BlockSpec((bb, C_HEADS, C_HEAD_DIM, C_HEAD_DIM), lambda b, i: (b, 0, 0, 0))
    vec = pl.BlockSpec((1, C_WIDTH), lambda b, i: (0, 0))
    return pl.pallas_call(
        functools.partial(_rwkv_chunk_kernel, c=c, nc=nc),
        grid=(bsz // bb, t // (nc * c)),
        in_specs=[seq] * 8 + [st, vec, vec],
        out_specs=[seq, st],
        out_shape=[jax.ShapeDtypeStruct((bsz, t, C_WIDTH), F32),
                   jax.ShapeDtypeStruct((bsz, C_HEADS, C_HEAD_DIM, C_HEAD_DIM), F32)],
        scratch_shapes=[pltpu.VMEM((bb * C_HEADS, C_HEAD_DIM, C_HEAD_DIM), F32)],
        compiler_params=_cparams("parallel", "arbitrary"),
        name="rwkv_chunks",
    )(*feats, s0, gn_g, gn_b)


def _merge_kernel(o0, l0, o1, l1, o2, l2, ob_ref, oc_ref, gate_ref, x_ref, wa_ref, wb_ref, wc_ref, wo_ref,
                  gain_ref, out_ref):
    ls = [l0[...], l1[...], l2[...]]
    mx = jnp.maximum(jnp.maximum(ls[0], ls[1]), ls[2])
    es = [jnp.exp(l - mx) for l in ls]
    den = es[0] + es[1] + es[2]
    o_a = (es[0] * o0[...] + es[1] * o1[...] + es[2] * o2[...]) / den
    merged = (gate_ref[:, 0:D_MODEL] * _mm(o_a, wa_ref[...])
              + gate_ref[:, D_MODEL:2 * D_MODEL] * _mm(ob_ref[...], wb_ref[...])
              + gate_ref[:, 2 * D_MODEL:] * _mm(oc_ref[...], wc_ref[...]))
    z = _mm(merged, wo_ref[...])
    out_ref[...] = x_ref[...] + _rms(z, gain_ref[...])


def _merge(att, o_b, o_c, gates, x, p):
    m = x.shape[0]
    tm = min(m, 512)
    row = lambda i: (i, 0)
    fixed = lambda i: (0, 0)
    rows = lambda w: pl.BlockSpec((tm, w), row)
    full = lambda a: pl.BlockSpec(a.shape, fixed)
    ws = (p['w_br_a'], p['w_br_b'], p['w_br_c'], p['w_out'], p['norm_mix_post'])
    return pl.pallas_call(
        _merge_kernel,
        grid=(m // tm,),
        in_specs=[rows(LANES)] * 6 + [rows(B_WIDTH), rows(C_WIDTH), rows(IN_GATE), rows(D_MODEL)]
                 + [full(a) for a in ws],
        out_specs=rows(D_MODEL),
        out_shape=jax.ShapeDtypeStruct((m, D_MODEL), F32),
        compiler_params=_cparams("parallel"),
        name="merge_out",
    )(*att, o_b, o_c, gates, x, *ws)


def _gelu_tanh(x):
    return 0.5 * x * (1.0 + jnp.tanh(math.sqrt(2.0 / math.pi) * (x + 0.044715 * (x * x * x))))


def _ffn_kernel(x_ref, g1_ref, fa_ref, fb_ref, wa_ref, wb_ref, ba_ref, bb_ref, ca_ref, cb_ref, fd_ref, g2_ref,
                out_ref, ta_ref, tb_ref, h_ref, acc_ref, exta_ref, extb_ref, cara_ref, carb_ref,
                *, tm, nb, cr):
    i = pl.program_id(0)
    j = pl.program_id(1)

    @pl.when(j == 0)
    def _():
        h_ref[...] = _rms(x_ref[...], g1_ref[...]).astype(BF16)
        acc_ref[...] = jnp.zeros_like(acc_ref)

    h = h_ref[...]
    cus = []
    for f_ref, w_ref, b_ref, c_ref, t_ref, ext_ref, car_ref in (
            (fa_ref, wa_ref, ba_ref, ca_ref, ta_ref, exta_ref, cara_ref),
            (fb_ref, wb_ref, bb_ref, cb_ref, tb_ref, extb_ref, carb_ref)):
        u = _dg(h, f_ref[...], _NN)

        @pl.when(i == 0)
        def _():
            ext_ref[0:cr, :] = c_ref[...]

        @pl.when(i > 0)
        def _():
            ext_ref[0:cr, :] = car_ref[j]

        ext_ref[cr:cr + tm, :] = u
        tail = u[tm - cr:, :]
        car_ref[j] = tail
        t_ref[...] = tail
        cus.append(w_ref[0:1, :] * ext_ref[pl.ds(cr - 2 * nb, tm), :]
                   + w_ref[1:2, :] * ext_ref[pl.ds(cr - nb, tm), :]
                   + w_ref[2:3, :] * u + b_ref[...])
    f = _gelu_tanh(cus[0]) * cus[1]
    acc_ref[...] += _dg(f.astype(BF16), fd_ref[...], _NN)

    @pl.when(j == pl.num_programs(1) - 1)
    def _():
        out_ref[...] = x_ref[...] + _rms(acc_ref[...], g2_ref[...])


def _ffn(x, ctx, p, nb):
    m = x.shape[0]
    tm = min(m, 1024)
    cr = ctx.shape[0]
    fc = FF_CHUNK
    nf = D_FF // fc
    row = lambda i, j: (i, 0)
    fixed = lambda i, j: (0, 0)
    lo = lambda i, j: (0, j)
    hi = lambda i, j: (0, nf + j)
    col = lambda rows: [pl.BlockSpec((rows, fc), lo), pl.BlockSpec((rows, fc), hi)]
    tail = jax.ShapeDtypeStruct(((m // tm) * cr, D_FF), F32)
    tail_spec = pl.BlockSpec((cr, fc), lambda i, j: (i, j))
    x2, ta, tb = pl.pallas_call(
        functools.partial(_ffn_kernel, tm=tm, nb=nb, cr=cr),
        grid=(m // tm, nf),
        in_specs=[pl.BlockSpec((tm, D_MODEL), row), pl.BlockSpec((1, D_MODEL), fixed)]
                 + col(D_MODEL) + col(F_CONV) + col(1) + col(cr)
                 + [pl.BlockSpec((fc, D_MODEL), lambda i, j: (j, 0)), pl.BlockSpec((1, D_MODEL), fixed)],
        out_specs=[pl.BlockSpec((tm, D_MODEL), row), tail_spec, tail_spec],
        out_shape=[jax.ShapeDtypeStruct((m, D_MODEL), F32), tail, tail],
        scratch_shapes=[pltpu.VMEM((tm, D_MODEL), BF16), pltpu.VMEM((tm, D_MODEL), F32),
                        pltpu.VMEM((cr + tm, fc), F32), pltpu.VMEM((cr + tm, fc), F32),
                        pltpu.VMEM((nf, cr, fc), F32), pltpu.VMEM((nf, cr, fc), F32)],
        compiler_params=_cparams("arbitrary", "arbitrary"),
        name="ffn",
    )(x, p['norm_ffn_pre'], p['f_up'], p['f_up'], p['f_dw_w'], p['f_dw_w'], p['f_dw_b'], p['f_dw_b'],
      ctx, ctx, p['f_down'], p['norm_ffn_post'])
    return x2, ta[-cr:], tb[-cr:]


def _to_time_major(a):
    a = jnp.swapaxes(a, 0, 1)
    return a.reshape((a.shape[0] * a.shape[1],) + a.shape[2:])


def _to_batch_major(a, ts):
    return jnp.swapaxes(a.reshape(ts, a.shape[0] // ts, a.shape[1]), 0, 1)


def _layer_prompt(x, tables, p):
    t = x.shape[0]
    qkv, glu, pc, gates = _in_proj(x, p['norm_mix_pre'], p['w_in'], *tables)
    att = []
    for g in range(N_A_GROUPS):
        att.extend(_attn_prompt(qkv, g))
    o_b = _conv_b(glu, None, p['b_dw_w'], p['b_dw_b'], p['b_ln_g'], p['b_ln_b'], 1)
    feats = _rwkv_pre(pc, None, p, 1)
    s0 = jnp.zeros((1, C_HEADS, C_HEAD_DIM, C_HEAD_DIM), F32)
    o_c, s_new = _rwkv_chunks([f[None] for f in feats], s0, p['c_gn_g'], p['c_gn_b'])
    x1 = _merge(att, o_b, o_c[0], gates, x, p)
    x2, ta, tb = _ffn(x1, jnp.zeros((SUBLANES, 2 * D_FF), F32), p, 1)
    kv = [qkv[t - min(w, t):, 3 * A_WIDTH * g + A_WIDTH:3 * A_WIDTH * (g + 1)]
          .reshape(1, min(w, t), 2, A_HEADS, A_HEAD_DIM) for g, (w, _) in enumerate(A_GROUPS)]
    f_tail = jnp.concatenate([ta, tb], axis=1)[-(F_CONV - 1):][None]
    return x2, kv, glu[t - (B_CONV - 1):][None], pc[t - 1:], s_new, f_tail


def _layer_sample(x, tables, caches, b_ctx, c_shift, c_state, f_ctx, p, ts):
    nb = x.shape[0] // ts
    qkv, glu, pc, gates = _in_proj(x, p['norm_mix_pre'], p['w_in'], *tables)
    qkv_bm = _to_batch_major(qkv, ts)
    qkv_pad = jnp.pad(qkv_bm, ((0, 0), (0, SUBLANES - ts), (0, 0)))
    o, lse = _attn_sample(qkv_pad, caches, ts)
    att = []
    for g in range(N_A_GROUPS):
        att.append(_to_time_major(o[:, :ts, g * LANES:(g + 1) * LANES]))
        att.append(_to_time_major(lse[:, :ts, g * LANES:(g + 1) * LANES]))
    o_b = _conv_b(glu, _to_time_major(b_ctx), p['b_dw_w'], p['b_dw_b'], p['b_ln_g'], p['b_ln_b'], nb)
    feats = _rwkv_pre(pc, c_shift, p, nb)
    feats_bm = [jnp.pad(_to_batch_major(f, ts), ((0, 0), (0, RWKV_CHUNK - ts), (0, 0))) for f in feats]
    o_c, s_new = _rwkv_chunks(feats_bm, c_state, p['c_gn_g'], p['c_gn_b'])
    x1 = _merge(att, o_b, _to_time_major(o_c[:, :ts]), gates, x, p)
    x2, ta, tb = _ffn(x1, _to_time_major(f_ctx), p, nb)
    kv = [_to_batch_major(qkv[:, 3 * A_WIDTH * g + A_WIDTH:3 * A_WIDTH * (g + 1)], ts)
          .reshape(nb, ts, 2, A_HEADS, A_HEAD_DIM) for g in range(N_A_GROUPS)]
    b_new = jnp.concatenate([b_ctx, _to_batch_major(glu, ts)], axis=1)[:, -(B_CONV - 1):]
    f_new = jnp.concatenate([f_ctx, _to_batch_major(jnp.concatenate([ta, tb], axis=1), F_CONV - 1)], axis=1)
    return x2, kv, b_new, pc[(ts - 1) * nb:], s_new, f_new[:, -(F_CONV - 1):]


def kernel(x_prompt, x_sample, cache_a_kv0, cache_a_kv1, cache_a_kv2, state_b_conv, state_c_shift, state_c_wkv, state_f_conv, norm_mix_pre, norm_mix_post, norm_ffn_pre, norm_ffn_post, w_in, b_dw_w, b_dw_b, b_ln_g, b_ln_b, c_mu, c_w0, c_w2, c_a0, c_a2, c_g2, c_kk, c_ka, c_rk, c_gn_g, c_gn_b, w_br_a, w_br_b, w_br_c, w_out, f_up, f_dw_w, f_dw_b, f_down):
    depth = w_in.shape[0]
    bp, tp, _ = x_prompt.shape
    bs, ts, _ = x_sample.shape
    assert bp == 1 and ts <= SUBLANES
    caches = (cache_a_kv0, cache_a_kv1, cache_a_kv2)

    head_id = jnp.arange(C_WIDTH) // C_HEAD_DIM
    seg = (head_id[:, None] == head_id[None, :]).astype(BF16)
    zpad = jnp.zeros((LANES - C_DECAY_RANK, C_WIDTH), F32)

    tab_p = _rope_tables(tp, 0, 1)
    tab_s = _rope_tables(ts * bs, PAST_LEN, bs)

    y_p = x_prompt[0]
    y_s = _to_time_major(x_sample)
    outs_p, outs_s = [], []
    for l in range(depth):
        vec = lambda a: a[l][None, :]
        p = {
            'norm_mix_pre': vec(norm_mix_pre), 'norm_mix_post': vec(norm_mix_post),
            'norm_ffn_pre': vec(norm_ffn_pre), 'norm_ffn_post': vec(norm_ffn_post),
            'w_in': w_in[l].astype(BF16),
            'b_dw_w': b_dw_w[l], 'b_dw_b': vec(b_dw_b), 'b_ln_g': vec(b_ln_g), 'b_ln_b': vec(b_ln_b),
            'c_mu': vec(c_mu), 'c_w0': vec(c_w0), 'c_a0': vec(c_a0),
            'c_w2p': jnp.concatenate([c_w2[l], zpad], axis=0),
            'c_a2p': jnp.concatenate([zpad, c_a2[l]], axis=0),
            'c_g2': c_g2[l], 'c_kk': vec(c_kk), 'c_ka': vec(c_ka),
            'c_rk': c_rk[l].reshape(1, C_WIDTH), 'c_gn_g': vec(c_gn_g), 'c_gn_b': vec(c_gn_b),
            'seg': seg,
            'w_br_a': w_br_a[l].astype(BF16), 'w_br_b': w_br_b[l].astype(BF16),
            'w_br_c': w_br_c[l].astype(BF16), 'w_out': w_out[l].astype(BF16),
            'f_up': f_up[l].astype(BF16), 'f_dw_w': f_dw_w[l], 'f_dw_b': vec(f_dw_b),
            'f_down': f_down[l].astype(BF16),
        }
        y_p, *rest_p = _layer_prompt(y_p, tab_p, p)
        outs_p.append(rest_p)
        y_s, *rest_s = _layer_sample(y_s, tab_s, [c[l] for c in caches], state_b_conv[l], state_c_shift[l],
                                     state_c_wkv[l], state_f_conv[l], p, ts)
        outs_s.append(rest_s)

    stack = lambda outs, f: jnp.stack([f(o) for o in outs])
    res = [y_p[None], _to_batch_major(y_s, ts)]
    res += [stack(outs_p, lambda o, g=g: o[0][g]) for g in range(N_A_GROUPS)]
    res += [stack(outs_s, lambda o, g=g: o[0][g]) for g in range(N_A_GROUPS)]
    for idx in range(1, 5):
        res += [stack(outs_p, lambda o: o[idx]), stack(outs_s, lambda o: o[idx])]
    return tuple(res)
```

```python
import functools
import math

import jax
import jax.numpy as jnp
from jax import lax
from jax.experimental import pallas as pl
from jax.experimental.pallas import tpu as pltpu

F32 = jnp.float32
BF16 = jnp.bfloat16

D_MODEL = 1024
PAST_LEN = 16384
A_GROUPS = ((128, 1), (512, 4), (2048, 16))
N_A_GROUPS = len(A_GROUPS)
A_HEAD_DIM = 32
A_HEADS = 4
A_WIDTH = A_HEADS * A_HEAD_DIM
ROPE_THETA = 10000.0
ATT_BLOCK = 128
B_WIDTH = 256
B_CONV = 31
C_HEAD_DIM = 64
C_WIDTH = 384
C_HEADS = 6
C_DECAY_RANK = 64
C_ICLR_RANK = 64
C_GATE_RANK = 128
DECAY_SCALE = math.exp(-0.5)
D_FF = 2816
F_CONV = 3
IN_A = N_A_GROUPS * 3 * A_WIDTH
IN_B = 2 * B_WIDTH
IN_C = 3 * C_WIDTH + C_DECAY_RANK + C_ICLR_RANK + C_GATE_RANK
IN_GATE = 3 * D_MODEL
IN_WIDTH = IN_A + IN_B + IN_C + IN_GATE
RMS_EPS = 1e-6
LN_EPS = 1e-5
GN_EPS = C_HEAD_DIM * 1e-5
NEG_BIG = -1e30

LANES = 128
SUBLANES = 8
VMEM_LIMIT = 56 * 1024 * 1024

RWKV_CHUNK = 64
FF_CHUNK = 256


def _cparams(*sem):
    return pltpu.CompilerParams(dimension_semantics=sem, vmem_limit_bytes=VMEM_LIMIT)


def _split_bf16(x, n):
    pieces = []
    rem = x
    for _ in range(n):
        p = rem.astype(BF16)
        pieces.append(p)
        rem = rem - p.astype(F32)
    return pieces


_NN = (((1,), (0,)), ((), ()))
_NT = (((1,), (1,)), ((), ()))


def _dg(a, b, dims):
    return lax.dot_general(a, b, dims, preferred_element_type=F32)


def _mm(a, b, dims=_NN, passes=1):
    if passes == 1:
        return _dg(a.astype(BF16), b.astype(BF16), dims)
    a_hi, a_lo = _split_bf16(a, 2)
    b_hi, b_lo = _split_bf16(b, 2)
    return _dg(a_hi, b_hi, dims) + (_dg(a_hi, b_lo, dims) + _dg(a_lo, b_hi, dims))


def _mm_exact_rhs(a, b_bf16, pieces=3):
    out = None
    for p in _split_bf16(a, pieces):
        t = _dg(p, b_bf16, _NN)
        out = t if out is None else out + t
    return out


def _mm_exact_rhs_left(a_bf16, b, pieces=3):
    out = None
    for p in _split_bf16(b, pieces):
        t = _dg(a_bf16, p, _NN)
        out = t if out is None else out + t
    return out


def _rms(x, g):
    ms = jnp.mean(x * x, axis=-1, keepdims=True)
    return x * lax.rsqrt(ms + RMS_EPS) * g


def _sigmoid(x):
    return 1.0 / (1.0 + jnp.exp(-x))


def _rope_table_kernel(inv_ref, cos_ref, sin_ref, *, tm, pos0, nb):
    row = pl.program_id(0) * tm + lax.broadcasted_iota(jnp.int32, (tm, LANES), 0)
    pos = (pos0 + row // nb).astype(F32)
    ang = pos * inv_ref[...]
    lane = lax.broadcasted_iota(jnp.int32, (tm, LANES), 1)
    sign = jnp.where((lane % A_HEAD_DIM) < A_HEAD_DIM // 2, -1.0, 1.0)
    cos_ref[...] = jnp.cos(ang)
    sin_ref[...] = jnp.sin(ang) * sign


def _rope_tables(m, pos0, nb):
    half = A_HEAD_DIM // 2
    inv = ROPE_THETA ** (-(jnp.arange(half, dtype=F32) * 2.0 / A_HEAD_DIM))
    inv_lane = jnp.tile(inv, 2 * A_HEADS)[None, :]
    tm = min(m, 512)
    return pl.pallas_call(
        functools.partial(_rope_table_kernel, tm=tm, pos0=pos0, nb=nb),
        grid=(m // tm,),
        in_specs=[pl.BlockSpec((1, LANES), lambda i: (0, 0))],
        out_specs=[pl.BlockSpec((tm, LANES), lambda i: (i, 0))] * 2,
        out_shape=[jax.ShapeDtypeStruct((m, LANES), F32)] * 2,
        compiler_params=_cparams("parallel"),
        name="rope_tables",
    )(inv_lane)


def _in_proj_kernel(x_ref, g_ref, w_ref, cos_ref, sin_ref, qkv_ref, glu_ref, pc_ref, gate_ref):
    h = _rms(x_ref[...], g_ref[...]).astype(BF16)
    cos = cos_ref[...]
    sin = sin_ref[...]
    lane = lax.broadcasted_iota(jnp.int32, cos.shape, 1)
    first_half = (lane % A_HEAD_DIM) < A_HEAD_DIM // 2
    pa = _dg(h, w_ref[:, 0:IN_A], _NN)
    for blk in range(IN_A // LANES):
        y = pa[:, blk * LANES:(blk + 1) * LANES]
        if blk % 3 != 2:
            swapped = jnp.where(first_half, pltpu.roll(y, LANES - A_HEAD_DIM // 2, 1),
                                pltpu.roll(y, A_HEAD_DIM // 2, 1))
            y = y * cos + swapped * sin
        qkv_ref[:, blk * LANES:(blk + 1) * LANES] = y
    pb = _dg(h, w_ref[:, IN_A:IN_A + IN_B], _NN)
    glu_ref[...] = pb[:, :B_WIDTH] * _sigmoid(pb[:, B_WIDTH:])
    pc_ref[...] = _dg(h, w_ref[:, IN_A + IN_B:IN_A + IN_B + IN_C], _NN)
    gate_ref[...] = _sigmoid(_dg(h, w_ref[:, IN_A + IN_B + IN_C:], _NN)).astype(gate_ref.dtype)


def _in_proj(x, gain, w_bf16, cos, sin):
    m = x.shape[0]
    tm = min(m, 256)
    row = lambda i: (i, 0)
    fixed = lambda i: (0, 0)
    widths = (IN_A, B_WIDTH, IN_C, IN_GATE)
    return pl.pallas_call(
        _in_proj_kernel,
        grid=(m // tm,),
        in_specs=[pl.BlockSpec((tm, D_MODEL), row), pl.BlockSpec((1, D_MODEL), fixed),
                  pl.BlockSpec((D_MODEL, IN_WIDTH), fixed),
                  pl.BlockSpec((tm, LANES), row), pl.BlockSpec((tm, LANES), row)],
        out_specs=[pl.BlockSpec((tm, w), row) for w in widths],
        out_shape=[jax.ShapeDtypeStruct((m, w), BF16 if w == IN_GATE else F32) for w in widths],
        compiler_params=_cparams("parallel"),
        name="in_proj",
    )(x, gain, w_bf16, cos, sin)


def _softmax_heads(q, k_bf16, v_bf16, valid):
    lane_head = lax.broadcasted_iota(jnp.int32, q.shape, 1) // A_HEAD_DIM
    scale = 1.0 / math.sqrt(A_HEAD_DIM)
    o_all = jnp.zeros(q.shape, F32)
    lse_all = jnp.zeros(q.shape, F32)
    for h in range(A_HEADS):
        hm = lane_head == h
        qh = jnp.where(hm, q, 0.0).astype(BF16)
        s = _dg(qh, k_bf16, _NT) * scale
        s = jnp.where(valid, s, NEG_BIG)
        mx = jnp.max(s, axis=-1, keepdims=True)
        p = jnp.exp(s - mx)
        l = jnp.sum(p, axis=-1, keepdims=True)
        o = _dg(p.astype(BF16), v_bf16, _NN) / l
        o_all = jnp.where(hm, o, o_all)
        lse_all = jnp.where(hm, mx + jnp.log(l), lse_all)
    return o_all, lse_all


def _attn_prompt_kernel(q_ref, kp_ref, kc_ref, vp_ref, vc_ref, o_ref, lse_ref, *, d):
    i = pl.program_id(0)
    qi = lax.broadcasted_iota(jnp.int32, (ATT_BLOCK, 2 * ATT_BLOCK), 0)
    kj = lax.broadcasted_iota(jnp.int32, (ATT_BLOCK, 2 * ATT_BLOCK), 1)
    delta = qi + ATT_BLOCK - kj
    valid = (delta >= 0) & (delta <= ATT_BLOCK) & ((kj >= ATT_BLOCK) | (i > 0))

    def residue(r, carry):
        rows = pl.ds(r, ATT_BLOCK, stride=d)
        k = jnp.concatenate([kp_ref[rows, :], kc_ref[rows, :]], axis=0).astype(BF16)
        v = jnp.concatenate([vp_ref[rows, :], vc_ref[rows, :]], axis=0).astype(BF16)
        o, lse = _softmax_heads(q_ref[rows, :], k, v, valid)
        o_ref[rows, :] = o
        lse_ref[rows, :] = lse
        return carry

    if d == 1:
        residue(0, 0)
    else:
        lax.fori_loop(0, d, residue, 0)


def _attn_prompt(qkv, group):
    t = qkv.shape[0]
    _, d = A_GROUPS[group]
    col = 3 * group
    blk = (d * ATT_BLOCK, LANES)
    cur = lambda off: (lambda i: (i, col + off))
    prev = lambda off: (lambda i: (jnp.maximum(i - 1, 0), col + off))
    return pl.pallas_call(
        functools.partial(_attn_prompt_kernel, d=d),
        grid=(t // (d * ATT_BLOCK),),
        in_specs=[pl.BlockSpec(blk, cur(0)), pl.BlockSpec(blk, prev(1)), pl.BlockSpec(blk, cur(1)),
                  pl.BlockSpec(blk, prev(2)), pl.BlockSpec(blk, cur(2))],
        out_specs=[pl.BlockSpec(blk, lambda i: (i, 0))] * 2,
        out_shape=[jax.ShapeDtypeStruct((t, LANES), F32)] * 2,
        compiler_params=_cparams("parallel"),
        name=f"attn_prompt_g{group}",
    )(qkv, qkv, qkv, qkv, qkv)


def _attn_sample_kernel(qkv_ref, c0_ref, c1_ref, c2_ref, o_ref, lse_ref, *, ts):
    qkv = qkv_ref[0]
    rows = qkv.shape[0]
    pad = jnp.zeros((LANES - rows, LANES), F32)
    for g, (window, d) in enumerate(A_GROUPS):
        base = 3 * A_WIDTH * g
        q = qkv[:, base:base + LANES]
        k_new = jnp.concatenate([qkv[:, base + LANES:base + 2 * LANES], pad], axis=0)
        v_new = jnp.concatenate([qkv[:, base + 2 * LANES:base + 3 * LANES], pad], axis=0)
        c_ref = (c0_ref, c1_ref, c2_ref)[g]
        k_t = c_ref[0].reshape(A_WIDTH, window).astype(BF16)
        v_t = c_ref[1].reshape(A_WIDTH, window).astype(BF16)
        qi = lax.broadcasted_iota(jnp.int32, (rows, window), 0)
        c = lax.broadcasted_iota(jnp.int32, (rows, window), 1)
        qn = lax.broadcasted_iota(jnp.int32, (rows, LANES), 0)
        cn = lax.broadcasted_iota(jnp.int32, (rows, LANES), 1)
        if d == 1:
            valid_c, valid_n = c >= qi, cn <= qn
        else:
            valid_c, valid_n = (c % d) == qi, cn == qn
        lane_head = lax.broadcasted_iota(jnp.int32, q.shape, 1) // A_HEAD_DIM
        scale = 1.0 / math.sqrt(A_HEAD_DIM)
        o_all = jnp.zeros(q.shape, F32)
        lse_all = jnp.zeros(q.shape, F32)
        k_new = k_new.astype(BF16)
        v_new = v_new.astype(BF16)
        for h in range(A_HEADS):
            hm = lane_head == h
            qh = jnp.where(hm, q, 0.0).astype(BF16)
            s_c = jnp.where(valid_c, _dg(qh, k_t, _NN) * scale, NEG_BIG)
            s_n = jnp.where(valid_n, _dg(qh, k_new, _NT) * scale, NEG_BIG)
            mx = jnp.maximum(jnp.max(s_c, axis=-1, keepdims=True), jnp.max(s_n, axis=-1, keepdims=True))
            p_c = jnp.exp(s_c - mx)
            p_n = jnp.exp(s_n - mx)
            l = jnp.sum(p_c, axis=-1, keepdims=True) + jnp.sum(p_n, axis=-1, keepdims=True)
            o = (_dg(p_c.astype(BF16), v_t, _NT) + _dg(p_n.astype(BF16), v_new, _NN)) / l
            o_all = jnp.where(hm, o, o_all)
            lse_all = jnp.where(hm, mx + jnp.log(l), lse_all)
        o_ref[0, :, g * LANES:(g + 1) * LANES] = o_all
        lse_ref[0, :, g * LANES:(g + 1) * LANES] = lse_all


def _attn_sample(qkv_bm, caches_t, layer, ts):
    bsz, rows, _ = qkv_bm.shape
    specs = [pl.BlockSpec((1, rows, IN_A), lambda b: (b, 0, 0))]
    for (window, d), c in zip(A_GROUPS, caches_t):
        assert c.shape[-1] == window and window % d == 0 and (d == 1 or d >= ts)
        specs.append(pl.BlockSpec((None, None, 2, A_HEADS, A_HEAD_DIM, window),
                                  lambda b: (layer, b, 0, 0, 0, 0)))
    out = jax.ShapeDtypeStruct((bsz, rows, N_A_GROUPS * LANES), F32)
    return pl.pallas_call(
        functools.partial(_attn_sample_kernel, ts=ts),
        grid=(bsz,),
        in_specs=specs,
        out_specs=[pl.BlockSpec((1, rows, N_A_GROUPS * LANES), lambda b: (b, 0, 0))] * 2,
        out_shape=[out, out],
        compiler_params=_cparams("parallel"),
        name="attn_sample",
    )(qkv_bm, *caches_t)


def _conv_b_body(ext_ref, w_ref, b_ref, g_ref, beta_ref, o_ref, *, tm, nb, base):
    acc = jnp.zeros((tm, B_WIDTH), F32) + b_ref[...]
    for j in range(B_CONV):
        acc = acc + w_ref[j:j + 1, :] * ext_ref[pl.ds(base + j * nb, tm), :]
    mu = jnp.mean(acc, axis=-1, keepdims=True)
    cen = acc - mu
    var = jnp.mean(cen * cen, axis=-1, keepdims=True)
    y = cen * lax.rsqrt(var + LN_EPS) * g_ref[...] + beta_ref[...]
    o_ref[...] = y * _sigmoid(y)


def _conv_b_prompt_kernel(halo_ref, cur_ref, w_ref, b_ref, g_ref, beta_ref, o_ref, ext_ref, *, tm, halo):
    ext_ref[0:halo, :] = jnp.where(pl.program_id(0) > 0, halo_ref[...], 0.0)
    ext_ref[halo:halo + tm, :] = cur_ref[...]
    _conv_b_body(ext_ref, w_ref, b_ref, g_ref, beta_ref, o_ref, tm=tm, nb=1, base=halo - (B_CONV - 1))


def _conv_b_sample_kernel(ctx_ref, cur_ref, w_ref, b_ref, g_ref, beta_ref, o_ref, ext_ref, *, tm, nb):
    n_ctx = (B_CONV - 1) * nb
    ext_ref[0:n_ctx, :] = ctx_ref[...]
    ext_ref[n_ctx:n_ctx + tm, :] = cur_ref[...]
    _conv_b_body(ext_ref, w_ref, b_ref, g_ref, beta_ref, o_ref, tm=tm, nb=nb, base=0)


def _conv_b(glu, ctx, w, b, g, beta, nb):
    m = glu.shape[0]
    fixed = lambda i: (0, 0)
    small = [pl.BlockSpec((B_CONV, B_WIDTH), fixed)] + [pl.BlockSpec((1, B_WIDTH), fixed)] * 3
    if ctx is None:
        tm = min(m, 512)
        halo = 32
        kern = functools.partial(_conv_b_prompt_kernel, tm=tm, halo=halo)
        first = pl.BlockSpec((halo, B_WIDTH), lambda i: (jnp.maximum(i * (tm // halo) - 1, 0), 0))
        lead, ext_rows = glu, halo + tm
    else:
        tm = m
        kern = functools.partial(_conv_b_sample_kernel, tm=tm, nb=nb)
        first = pl.BlockSpec(ctx.shape, fixed)
        lead, ext_rows = ctx, ctx.shape[0] + tm
    return pl.pallas_call(
        kern,
        grid=(m // tm,),
        in_specs=[first, pl.BlockSpec((tm, B_WIDTH), lambda i: (i, 0))] + small,
        out_specs=pl.BlockSpec((tm, B_WIDTH), lambda i: (i, 0)),
        out_shape=jax.ShapeDtypeStruct((m, B_WIDTH), F32),
        scratch_shapes=[pltpu.VMEM((ext_rows, B_WIDTH), F32)],
        compiler_params=_cparams("parallel"),
        name="conv_b",
    )(lead, glu, w, b, g, beta)


def _rwkv_pre_kernel(prev_ref, pc_ref, mu_ref, w2_ref, a2_ref, g2_ref, w0_ref, a0_ref, kkw_ref, ka_ref,
                     rk_ref, seg_ref, r_o, lw_o, k_o, v_o, kk_o, b_o, g_o, bonus_o, ext_ref,
                     *, tm, nb, off, zero_first):
    prev_rows = prev_ref[...]
    if zero_first:
        prev_rows = jnp.where(pl.program_id(0) > 0, prev_rows, 0.0)
    ext_ref[0:off, :] = prev_rows
    ext_ref[off:off + tm, :] = pc_ref[...]
    pc = pc_ref[...]
    prev = ext_ref[pl.ds(off - nb, tm), :]
    xs = pc + mu_ref[...] * (prev - pc)
    r = xs[:, 0:C_WIDTH]
    k = xs[:, C_WIDTH:2 * C_WIDTH]
    v = xs[:, 2 * C_WIDTH:3 * C_WIDTH]
    wa = xs[:, 3 * C_WIDTH:3 * C_WIDTH + LANES]
    gl = xs[:, 3 * C_WIDTH + LANES:]
    seg = seg_ref[...]
    lw = -DECAY_SCALE * _sigmoid(w0_ref[...] + _mm(jnp.tanh(wa), w2_ref[...], passes=3))
    a = _sigmoid(a0_ref[...] + _mm(wa, a2_ref[...], passes=3))
    g = _mm(_sigmoid(gl), g2_ref[...], passes=3)
    kk = k * kkw_ref[...]
    ss = _mm_exact_rhs(kk * kk, seg, 2)
    kk = kk * lax.rsqrt(jnp.maximum(ss, 1e-24))
    k2 = k * (1.0 + (a - 1.0) * ka_ref[...])
    r_o[...] = r
    lw_o[...] = lw
    k_o[...] = k2
    v_o[...] = v
    kk_o[...] = kk
    b_o[...] = kk * a
    g_o[...] = g
    bonus_o[...] = _mm_exact_rhs(r * k2 * rk_ref[...], seg, 2) * v


def _rwkv_pre(pc, ctx, p, nb):
    m = pc.shape[0]
    tm = min(m, 512)
    off = max(SUBLANES, nb)
    fixed = lambda i: (0, 0)
    row = lambda i: (i, 0)
    if ctx is None:
        lead = pc
        first = pl.BlockSpec((off, IN_C), lambda i: (jnp.maximum(i * (tm // off) - 1, 0), 0))
    else:
        assert m == tm
        lead = ctx
        first = pl.BlockSpec((off, IN_C), fixed)
    vec = pl.BlockSpec((1, C_WIDTH), fixed)
    out = jax.ShapeDtypeStruct((m, C_WIDTH), F32)
    return pl.pallas_call(
        functools.partial(_rwkv_pre_kernel, tm=tm, nb=nb, off=off, zero_first=ctx is None),
        grid=(m // tm,),
        in_specs=[first, pl.BlockSpec((tm, IN_C), row), pl.BlockSpec((1, IN_C), fixed),
                  pl.BlockSpec((LANES, C_WIDTH), fixed), pl.BlockSpec((LANES, C_WIDTH), fixed),
                  pl.BlockSpec((C_GATE_RANK, C_WIDTH), fixed), vec, vec, vec, vec, vec,
                  pl.BlockSpec((C_WIDTH, C_WIDTH), fixed)],
        out_specs=[pl.BlockSpec((tm, C_WIDTH), row)] * 8,
        out_shape=[out] * 8,
        scratch_shapes=[pltpu.VMEM((off + tm, IN_C), F32)],
        compiler_params=_cparams("parallel"),
        name="rwkv_pre",
    )(lead, pc, p['c_mu'], p['c_w2p'], p['c_a2p'], p['c_g2'], p['c_w0'], p['c_a0'], p['c_kk'], p['c_ka'],
      p['c_rk'], p['seg'])


RWKV_PASSES = 1


_BNN = (((2,), (1,)), ((0,), (0,)))
_BNT = (((2,), (2,)), ((0,), (0,)))
_BTN = (((1,), (1,)), ((0,), (0,)))


def _unit_lower_inverse(a_strict):
    g, c, _ = a_strict.shape
    ri = lax.broadcasted_iota(jnp.int32, (c, c), 0)
    ci = lax.broadcasted_iota(jnp.int32, (c, c), 1)
    inv = jnp.broadcast_to(jnp.where(ri == ci, 1.0, 0.0).astype(F32), (g, c, c))
    s = 1
    while s < c:
        same = (ri // (2 * s)) == (ci // (2 * s))
        off = jnp.where(same & ((ri % (2 * s)) >= s) & ((ci % (2 * s)) < s), a_strict, 0.0)
        inv = inv - _mm(inv, _mm(off, inv, _BNN, RWKV_PASSES), _BNN, RWKV_PASSES)
        s *= 2
    return inv


def _rwkv_chunk_kernel(r_ref, lw_ref, k_ref, v_ref, kk_ref, b_ref, g_ref, bonus_ref, s0_ref, gng_ref, gnb_ref,
                       o_ref, s_out_ref, state_ref, *, c, nc):
    bb = r_ref.shape[0]
    n = bb * C_HEADS
    ci = pl.program_id(1)

    @pl.when(ci == 0)
    def _():
        state_ref[...] = s0_ref[...].reshape(n, C_HEAD_DIM, C_HEAD_DIM)

    ri = lax.broadcasted_iota(jnp.int32, (c, c), 0)
    cj = lax.broadcasted_iota(jnp.int32, (c, c), 1)
    incl = ri >= cj
    strict = ri > cj
    tri = jnp.where(incl, 1.0, 0.0).astype(BF16)

    groups = []
    for ch in range(nc):
        rows = slice(ch * c, (ch + 1) * c)
        for b in range(bb):
            lw = lw_ref[b, rows, :]
            cum = _mm_exact_rhs_left(tri, lw)
            cend = cum[c - 1:c, :]
            w_inv = jnp.exp(-cum)
            w_end = jnp.exp(cend - cum)
            kvec = k_ref[b, rows, :]
            bvec = b_ref[b, rows, :]
            groups.append((kk_ref[b, rows, :] * jnp.exp(cum - lw), r_ref[b, rows, :] * jnp.exp(cum),
                           bvec * w_inv, kvec * w_inv, bvec * w_end, kvec * w_end, v_ref[b, rows, :],
                           jnp.exp(cend), bonus_ref[b, rows, :], g_ref[b, rows, :]))

    def heads(idx):
        return jnp.stack([grp[idx][:, h * C_HEAD_DIM:(h + 1) * C_HEAD_DIM]
                          for grp in groups for h in range(C_HEADS)])

    P = RWKV_PASSES
    kap, rho, bt, kt, bend, kend, v, wc = [heads(i) for i in range(8)]
    lhs2 = jnp.concatenate([kap, rho], axis=1)
    xb = _mm(lhs2, bt, _BNT, P)
    xk = _mm(lhs2, kt, _BNT, P)
    a_kb = jnp.where(strict, xb[:, :c], 0.0)
    a_rb = jnp.where(incl, xb[:, c:], 0.0)
    a_kk = jnp.where(strict, xk[:, :c], 0.0)
    a_rk = jnp.where(incl, xk[:, c:], 0.0)
    t_inv = _unit_lower_inverse(a_kb)
    kap_p = _mm(t_inv, kap, _BNN, P)
    v_p = _mm(t_inv, _mm(a_kk, v, _BNN, P), _BNN, P)
    er = lax.broadcasted_iota(jnp.int32, (C_HEAD_DIM, C_HEAD_DIM), 0)
    ec = lax.broadcasted_iota(jnp.int32, (C_HEAD_DIM, C_HEAD_DIM), 1)
    pm = jnp.where(er == ec, wc, 0.0) - _mm(kap_p, bend, _BTN, P)
    q = _mm(v, kend, _BTN, P) - _mm(v_p, bend, _BTN, P)
    rp = rho - _mm(a_rb, kap_p, _BNN, P)
    y0 = _mm(a_rk, v, _BNN, P) - _mm(a_rb, v_p, _BNN, P)

    s = state_ref[...]
    ys = []
    for ch in range(nc):
        sl = slice(ch * n, (ch + 1) * n)
        ys.append(_mm(rp[sl], s, _BNT, P) + y0[sl])
        s = _mm(s, pm[sl], _BNN, P) + q[sl]
    state_ref[...] = s
    y = jnp.concatenate(ys, axis=0) if nc > 1 else ys[0]

    mu = jnp.mean(y, axis=-1, keepdims=True)
    cen = y - mu
    var = jnp.mean(cen * cen, axis=-1, keepdims=True)
    out = cen * lax.rsqrt(var + GN_EPS)
    bonus = heads(8)
    gate = heads(9)
    idx = 0
    for ch in range(nc):
        for b in range(bb):
            for h in range(C_HEADS):
                sl = slice(h * C_HEAD_DIM, (h + 1) * C_HEAD_DIM)
                yn = out[idx] * gng_ref[:, sl] + gnb_ref[:, sl]
                o_ref[b, ch * c:(ch + 1) * c, sl] = (yn + bonus[idx]) * gate[idx]
                idx += 1

    @pl.when(ci == pl.num_programs(1) - 1)
    def _():
        s_out_ref[...] = s.reshape(bb, C_HEADS, C_HEAD_DIM, C_HEAD_DIM)


def _rwkv_chunks(feats, s0, gn_g, gn_b):
    bsz, t, _ = feats[0].shape
    c = RWKV_CHUNK
    bb = 4 if bsz % 4 == 0 else 1
    nc = max(1, min(4 // bb, t // c))
    seq = pl.BlockSpec((bb, nc * c, C_WIDTH), lambda b, i: (b, i, 0))
    st = pl.BlockSpec((bb, C_HEADS, C_HEAD_DIM, C_HEAD_DIM), lambda b, i: (b, 0, 0, 0))
    vec = pl.BlockSpec((1, C_WIDTH), lambda b, i: (0, 0))
    return pl.pallas_call(
        functools.partial(_rwkv_chunk_kernel, c=c, nc=nc),
        grid=(bsz // bb, t // (nc * c)),
        in_specs=[seq] * 8 + [st, vec, vec],
        out_specs=[seq, st],
        out_shape=[jax.ShapeDtypeStruct((bsz, t, C_WIDTH), F32),
                   jax.ShapeDtypeStruct((bsz, C_HEADS, C_HEAD_DIM, C_HEAD_DIM), F32)],
        scratch_shapes=[pltpu.VMEM((bb * C_HEADS, C_HEAD_DIM, C_HEAD_DIM), F32)],
        compiler_params=_cparams("parallel", "arbitrary"),
        name="rwkv_chunks",
    )(*feats, s0, gn_g, gn_b)


def _merge_kernel(o0, l0, o1, l1, o2, l2, ob_ref, oc_ref, gate_ref, x_ref, wa_ref, wb_ref, wc_ref, wo_ref,
                  gain_ref, out_ref):
    ls = [l0[...], l1[...], l2[...]]
    mx = jnp.maximum(jnp.maximum(ls[0], ls[1]), ls[2])
    es = [jnp.exp(l - mx) for l in ls]
    den = es[0] + es[1] + es[2]
    o_a = (es[0] * o0[...] + es[1] * o1[...] + es[2] * o2[...]) / den
    merged = (gate_ref[:, 0:D_MODEL] * _mm(o_a, wa_ref[...])
              + gate_ref[:, D_MODEL:2 * D_MODEL] * _mm(ob_ref[...], wb_ref[...])
              + gate_ref[:, 2 * D_MODEL:] * _mm(oc_ref[...], wc_ref[...]))
    z = _mm(merged, wo_ref[...])
    out_ref[...] = x_ref[...] + _rms(z, gain_ref[...])


def _merge(att, o_b, o_c, gates, x, p):
    m = x.shape[0]
    tm = min(m, 512)
    row = lambda i: (i, 0)
    fixed = lambda i: (0, 0)
    rows = lambda w: pl.BlockSpec((tm, w), row)
    full = lambda a: pl.BlockSpec(a.shape, fixed)
    ws = (p['w_br_a'], p['w_br_b'], p['w_br_c'], p['w_out'], p['norm_mix_post'])
    return pl.pallas_call(
        _merge_kernel,
        grid=(m // tm,),
        in_specs=[rows(LANES)] * 6 + [rows(B_WIDTH), rows(C_WIDTH), rows(IN_GATE), rows(D_MODEL)]
                 + [full(a) for a in ws],
        out_specs=rows(D_MODEL),
        out_shape=jax.ShapeDtypeStruct((m, D_MODEL), F32),
        compiler_params=_cparams("parallel"),
        name="merge_out",
    )(*att, o_b, o_c, gates, x, *ws)


def _gelu_tanh(x):
    return 0.5 * x * (1.0 + jnp.tanh(math.sqrt(2.0 / math.pi) * (x + 0.044715 * (x * x * x))))


FFN_PIECE = 32


def _ffn_kernel(x_ref, g1_ref, fu_ref, w_ref, b_ref, c_ref, fd_ref, g2_ref, out_ref, t_ref, ext_ref, f_ref,
                *, tm, nb, cr):
    @pl.when(pl.program_id(0) == 0)
    def _():
        ext_ref[0:cr, :] = c_ref[...]

    h = _rms(x_ref[...], g1_ref[...]).astype(BF16)
    n_chunks = D_FF // FF_CHUNK
    split = (n_chunks + 1) // 2 * FF_CHUNK
    z = None
    for jc in range(n_chunks):
        pair = (slice(jc * FF_CHUNK, (jc + 1) * FF_CHUNK),
                slice(D_FF + jc * FF_CHUNK, D_FF + (jc + 1) * FF_CHUNK))
        for cols in pair:
            ext_ref[cr:cr + tm, cols] = _dg(h, fu_ref[:, cols], _NN)
        for r0 in range(0, tm, FFN_PIECE):
            cus = []
            for cols in pair:
                cus.append(w_ref[0:1, cols] * ext_ref[r0 + cr - 2 * nb:r0 + cr - 2 * nb + FFN_PIECE, cols]
                           + w_ref[1:2, cols] * ext_ref[r0 + cr - nb:r0 + cr - nb + FFN_PIECE, cols]
                           + w_ref[2:3, cols] * ext_ref[r0 + cr:r0 + cr + FFN_PIECE, cols] + b_ref[:, cols])
            f_ref[r0:r0 + FFN_PIECE, jc * FF_CHUNK:(jc + 1) * FF_CHUNK] = (
                _gelu_tanh(cus[0]) * cus[1]).astype(BF16)
        if (jc + 1) * FF_CHUNK == split:
            z = _dg(f_ref[:, 0:split], fd_ref[0:split, :], _NN)
    z = z + _dg(f_ref[:, split:], fd_ref[split:, :], _NN)
    tail = ext_ref[tm:tm + cr, :]
    t_ref[...] = tail
    ext_ref[0:cr, :] = tail
    out_ref[...] = x_ref[...] + _rms(z, g2_ref[...])


def _ffn(x, ctx, p, nb):
    m = x.shape[0]
    tm = min(m, 512)
    cr = ctx.shape[0]
    row = lambda i: (i, 0)
    fixed = lambda i: (0, 0)
    once = lambda shape: pl.BlockSpec(shape, fixed, pipeline_mode=pl.Buffered(1))
    x2, tail = pl.pallas_call(
        functools.partial(_ffn_kernel, tm=tm, nb=nb, cr=cr),
        grid=(m // tm,),
        in_specs=[pl.BlockSpec((tm, D_MODEL), row), once((1, D_MODEL)), once((D_MODEL, 2 * D_FF)),
                  once((F_CONV, 2 * D_FF)), once((1, 2 * D_FF)), once((cr, 2 * D_FF)),
                  once((D_FF, D_MODEL)), once((1, D_MODEL))],
        out_specs=[pl.BlockSpec((tm, D_MODEL), row), pl.BlockSpec((cr, 2 * D_FF), row)],
        out_shape=[jax.ShapeDtypeStruct((m, D_MODEL), F32),
                   jax.ShapeDtypeStruct(((m // tm) * cr, 2 * D_FF), F32)],
        scratch_shapes=[pltpu.VMEM((cr + tm, 2 * D_FF), F32), pltpu.VMEM((tm, D_FF), BF16)],
        compiler_params=_cparams("arbitrary"),
        name="ffn",
    )(x, p['norm_ffn_pre'], p['f_up'], p['f_dw_w'], p['f_dw_b'], ctx, p['f_down'], p['norm_ffn_post'])
    return x2, tail[-cr:]


def _to_time_major(a):
    a = jnp.swapaxes(a, 0, 1)
    return a.reshape((a.shape[0] * a.shape[1],) + a.shape[2:])


def _to_batch_major(a, ts):
    return jnp.swapaxes(a.reshape(ts, a.shape[0] // ts, a.shape[1]), 0, 1)


def _layer_prompt(x, tables, p):
    t = x.shape[0]
    qkv, glu, pc, gates = _in_proj(x, p['norm_mix_pre'], p['w_in'], *tables)
    att = []
    for g in range(N_A_GROUPS):
        att.extend(_attn_prompt(qkv, g))
    o_b = _conv_b(glu, None, p['b_dw_w'], p['b_dw_b'], p['b_ln_g'], p['b_ln_b'], 1)
    feats = _rwkv_pre(pc, None, p, 1)
    s0 = jnp.zeros((1, C_HEADS, C_HEAD_DIM, C_HEAD_DIM), F32)
    o_c, s_new = _rwkv_chunks([f[None] for f in feats], s0, p['c_gn_g'], p['c_gn_b'])
    x1 = _merge(att, o_b, o_c[0], gates, x, p)
    x2, tail = _ffn(x1, jnp.zeros((SUBLANES, 2 * D_FF), F32), p, 1)
    kv = [qkv[t - min(w, t):, 3 * A_WIDTH * g + A_WIDTH:3 * A_WIDTH * (g + 1)]
          .reshape(1, min(w, t), 2, A_HEADS, A_HEAD_DIM) for g, (w, _) in enumerate(A_GROUPS)]
    f_tail = tail[-(F_CONV - 1):][None]
    return x2, kv, glu[t - (B_CONV - 1):][None], pc[t - 1:], s_new, f_tail


def _layer_sample(x, tables, caches_t, layer, b_ctx, c_shift, c_state, f_ctx, p, ts):
    nb = x.shape[0] // ts
    qkv, glu, pc, gates = _in_proj(x, p['norm_mix_pre'], p['w_in'], *tables)
    qkv_bm = _to_batch_major(qkv, ts)
    qkv_pad = jnp.pad(qkv_bm, ((0, 0), (0, SUBLANES - ts), (0, 0)))
    o, lse = _attn_sample(qkv_pad, caches_t, layer, ts)
    att = []
    for g in range(N_A_GROUPS):
        att.append(_to_time_major(o[:, :ts, g * LANES:(g + 1) * LANES]))
        att.append(_to_time_major(lse[:, :ts, g * LANES:(g + 1) * LANES]))
    o_b = _conv_b(glu, _to_time_major(b_ctx), p['b_dw_w'], p['b_dw_b'], p['b_ln_g'], p['b_ln_b'], nb)
    feats = _rwkv_pre(pc, c_shift, p, nb)
    feats_bm = [jnp.pad(_to_batch_major(f, ts), ((0, 0), (0, RWKV_CHUNK - ts), (0, 0))) for f in feats]
    o_c, s_new = _rwkv_chunks(feats_bm, c_state, p['c_gn_g'], p['c_gn_b'])
    x1 = _merge(att, o_b, _to_time_major(o_c[:, :ts]), gates, x, p)
    x2, tail = _ffn(x1, _to_time_major(f_ctx), p, nb)
    kv = [_to_batch_major(qkv[:, 3 * A_WIDTH * g + A_WIDTH:3 * A_WIDTH * (g + 1)], ts)
          .reshape(nb, ts, 2, A_HEADS, A_HEAD_DIM) for g in range(N_A_GROUPS)]
    b_new = jnp.concatenate([b_ctx, _to_batch_major(glu, ts)], axis=1)[:, -(B_CONV - 1):]
    f_new = jnp.concatenate([f_ctx, _to_batch_major(tail, F_CONV - 1)], axis=1)
    return x2, kv, b_new, pc[(ts - 1) * nb:], s_new, f_new[:, -(F_CONV - 1):]


def kernel(x_prompt, x_sample, cache_a_kv0, cache_a_kv1, cache_a_kv2, state_b_conv, state_c_shift, state_c_wkv, state_f_conv, norm_mix_pre, norm_mix_post, norm_ffn_pre, norm_ffn_post, w_in, b_dw_w, b_dw_b, b_ln_g, b_ln_b, c_mu, c_w0, c_w2, c_a0, c_a2, c_g2, c_kk, c_ka, c_rk, c_gn_g, c_gn_b, w_br_a, w_br_b, w_br_c, w_out, f_up, f_dw_w, f_dw_b, f_down):
    depth = w_in.shape[0]
    bp, tp, _ = x_prompt.shape
    bs, ts, _ = x_sample.shape
    assert bp == 1 and ts <= SUBLANES
    caches_t = [jnp.transpose(c, (0, 1, 3, 4, 5, 2)) for c in (cache_a_kv0, cache_a_kv1, cache_a_kv2)]

    head_id = jnp.arange(C_WIDTH) // C_HEAD_DIM
    seg = (head_id[:, None] == head_id[None, :]).astype(BF16)
    zpad = jnp.zeros((LANES - C_DECAY_RANK, C_WIDTH), F32)

    tab_p = _rope_tables(tp, 0, 1)
    tab_s = _rope_tables(ts * bs, PAST_LEN, bs)

    y_p = x_prompt[0]
    y_s = _to_time_major(x_sample)
    outs_p, outs_s = [], []
    for l in range(depth):
        vec = lambda a: a[l][None, :]
        p = {
            'norm_mix_pre': vec(norm_mix_pre), 'norm_mix_post': vec(norm_mix_post),
            'norm_ffn_pre': vec(norm_ffn_pre), 'norm_ffn_post': vec(norm_ffn_post),
            'w_in': w_in[l].astype(BF16),
            'b_dw_w': b_dw_w[l], 'b_dw_b': vec(b_dw_b), 'b_ln_g': vec(b_ln_g), 'b_ln_b': vec(b_ln_b),
            'c_mu': vec(c_mu), 'c_w0': vec(c_w0), 'c_a0': vec(c_a0),
            'c_w2p': jnp.concatenate([c_w2[l], zpad], axis=0),
            'c_a2p': jnp.concatenate([zpad, c_a2[l]], axis=0),
            'c_g2': c_g2[l], 'c_kk': vec(c_kk), 'c_ka': vec(c_ka),
            'c_rk': c_rk[l].reshape(1, C_WIDTH), 'c_gn_g': vec(c_gn_g), 'c_gn_b': vec(c_gn_b),
            'seg': seg,
            'w_br_a': w_br_a[l].astype(BF16), 'w_br_b': w_br_b[l].astype(BF16),
            'w_br_c': w_br_c[l].astype(BF16), 'w_out': w_out[l].astype(BF16),
            'f_up': f_up[l].astype(BF16), 'f_dw_w': f_dw_w[l], 'f_dw_b': vec(f_dw_b),
            'f_down': f_down[l].astype(BF16),
        }
        y_p, *rest_p = _layer_prompt(y_p, tab_p, p)
        outs_p.append(rest_p)
        y_s, *rest_s = _layer_sample(y_s, tab_s, caches_t, l, state_b_conv[l], state_c_shift[l],
                                     state_c_wkv[l], state_f_conv[l], p, ts)
        outs_s.append(rest_s)

    stack = lambda outs, f: jnp.stack([f(o) for o in outs])
    res = [y_p[None], _to_batch_major(y_s, ts)]
    res += [stack(outs_p, lambda o, g=g: o[0][g]) for g in range(N_A_GROUPS)]
    res += [stack(outs_s, lambda o, g=g: o[0][g]) for g in range(N_A_GROUPS)]
    for idx in range(1, 5):
        res += [stack(outs_p, lambda o: o[idx]), stack(outs_s, lambda o: o[idx])]
    return tuple(res)
```

```python
import functools
import math

import jax
import jax.numpy as jnp
from jax import lax
from jax.experimental import pallas as pl
from jax.experimental.pallas import tpu as pltpu

F32 = jnp.float32
BF16 = jnp.bfloat16

D_MODEL = 1024
PAST_LEN = 16384
A_GROUPS = ((128, 1), (512, 4), (2048, 16))
N_A_GROUPS = len(A_GROUPS)
A_HEAD_DIM = 32
A_HEADS = 4
A_WIDTH = A_HEADS * A_HEAD_DIM
ROPE_THETA = 10000.0
ATT_BLOCK = 128
B_WIDTH = 256
B_CONV = 31
C_HEAD_DIM = 64
C_WIDTH = 384
C_HEADS = 6
C_DECAY_RANK = 64
C_ICLR_RANK = 64
C_GATE_RANK = 128
DECAY_SCALE = math.exp(-0.5)
D_FF = 2816
F_CONV = 3
IN_A = N_A_GROUPS * 3 * A_WIDTH
IN_B = 2 * B_WIDTH
IN_C = 3 * C_WIDTH + C_DECAY_RANK + C_ICLR_RANK + C_GATE_RANK
IN_GATE = 3 * D_MODEL
IN_WIDTH = IN_A + IN_B + IN_C + IN_GATE
RMS_EPS = 1e-6
LN_EPS = 1e-5
GN_EPS = C_HEAD_DIM * 1e-5
NEG_BIG = -1e30

LANES = 128
SUBLANES = 8
VMEM_LIMIT = 56 * 1024 * 1024

RWKV_CHUNK = 64
FF_CHUNK = 256


def _cparams(*sem):
    return pltpu.CompilerParams(dimension_semantics=sem, vmem_limit_bytes=VMEM_LIMIT)


def _split_bf16(x, n):
    pieces = []
    rem = x
    for _ in range(n):
        p = rem.astype(BF16)
        pieces.append(p)
        rem = rem - p.astype(F32)
    return pieces


_NN = (((1,), (0,)), ((), ()))
_NT = (((1,), (1,)), ((), ()))


def _dg(a, b, dims):
    return lax.dot_general(a, b, dims, preferred_element_type=F32)


def _mm(a, b, dims=_NN, passes=1):
    if passes == 1:
        return _dg(a.astype(BF16), b.astype(BF16), dims)
    a_hi, a_lo = _split_bf16(a, 2)
    b_hi, b_lo = _split_bf16(b, 2)
    return _dg(a_hi, b_hi, dims) + (_dg(a_hi, b_lo, dims) + _dg(a_lo, b_hi, dims))


def _mm_exact_rhs(a, b_bf16, pieces=3):
    out = None
    for p in _split_bf16(a, pieces):
        t = _dg(p, b_bf16, _NN)
        out = t if out is None else out + t
    return out


def _mm_exact_rhs_left(a_bf16, b, pieces=3):
    out = None
    for p in _split_bf16(b, pieces):
        t = _dg(a_bf16, p, _NN)
        out = t if out is None else out + t
    return out


def _rms(x, g):
    ms = jnp.mean(x * x, axis=-1, keepdims=True)
    return x * lax.rsqrt(ms + RMS_EPS) * g


def _sigmoid(x):
    return 1.0 / (1.0 + jnp.exp(-x))


def _rope_table_kernel(inv_ref, cos_ref, sin_ref, *, tm, pos0, nb):
    row = pl.program_id(0) * tm + lax.broadcasted_iota(jnp.int32, (tm, LANES), 0)
    pos = (pos0 + row // nb).astype(F32)
    ang = pos * inv_ref[...]
    lane = lax.broadcasted_iota(jnp.int32, (tm, LANES), 1)
    sign = jnp.where((lane % A_HEAD_DIM) < A_HEAD_DIM // 2, -1.0, 1.0)
    cos_ref[...] = jnp.cos(ang)
    sin_ref[...] = jnp.sin(ang) * sign


def _rope_tables(m, pos0, nb):
    half = A_HEAD_DIM // 2
    inv = ROPE_THETA ** (-(jnp.arange(half, dtype=F32) * 2.0 / A_HEAD_DIM))
    inv_lane = jnp.tile(inv, 2 * A_HEADS)[None, :]
    tm = min(m, 512)
    return pl.pallas_call(
        functools.partial(_rope_table_kernel, tm=tm, pos0=pos0, nb=nb),
        grid=(m // tm,),
        in_specs=[pl.BlockSpec((1, LANES), lambda i: (0, 0))],
        out_specs=[pl.BlockSpec((tm, LANES), lambda i: (i, 0))] * 2,
        out_shape=[jax.ShapeDtypeStruct((m, LANES), F32)] * 2,
        compiler_params=_cparams("parallel"),
        name="rope_tables",
    )(inv_lane)


def _in_proj_kernel(x_ref, g_ref, w_ref, cos_ref, sin_ref, qkv_ref, glu_ref, pc_ref, gate_ref):
    h = _rms(x_ref[...], g_ref[...]).astype(BF16)
    cos = cos_ref[...]
    sin = sin_ref[...]
    lane = lax.broadcasted_iota(jnp.int32, cos.shape, 1)
    first_half = (lane % A_HEAD_DIM) < A_HEAD_DIM // 2
    pa = _dg(h, w_ref[:, 0:IN_A], _NN)
    for blk in range(IN_A // LANES):
        y = pa[:, blk * LANES:(blk + 1) * LANES]
        if blk % 3 != 2:
            swapped = jnp.where(first_half, pltpu.roll(y, LANES - A_HEAD_DIM // 2, 1),
                                pltpu.roll(y, A_HEAD_DIM // 2, 1))
            y = y * cos + swapped * sin
        qkv_ref[:, blk * LANES:(blk + 1) * LANES] = y
    pb = _dg(h, w_ref[:, IN_A:IN_A + IN_B], _NN)
    glu_ref[...] = pb[:, :B_WIDTH] * _sigmoid(pb[:, B_WIDTH:])
    pc_ref[...] = _dg(h, w_ref[:, IN_A + IN_B:IN_A + IN_B + IN_C], _NN)
    gate_ref[...] = _sigmoid(_dg(h, w_ref[:, IN_A + IN_B + IN_C:], _NN)).astype(gate_ref.dtype)


def _layer_weight(w, layer, **kw):
    return pl.BlockSpec((None,) + w.shape[1:], lambda *_: (layer, 0, 0), **kw)


def _in_proj(x, gain, w_bf16, layer, cos, sin):
    m = x.shape[0]
    tm = min(m, 512)
    row = lambda i: (i, 0)
    fixed = lambda i: (0, 0)
    widths = (IN_A, B_WIDTH, IN_C, IN_GATE)
    return pl.pallas_call(
        _in_proj_kernel,
        grid=(m // tm,),
        in_specs=[pl.BlockSpec((tm, D_MODEL), row), pl.BlockSpec((1, D_MODEL), fixed),
                  _layer_weight(w_bf16, layer, pipeline_mode=pl.Buffered(1)),
                  pl.BlockSpec((tm, LANES), row), pl.BlockSpec((tm, LANES), row)],
        out_specs=[pl.BlockSpec((tm, w), row) for w in widths],
        out_shape=[jax.ShapeDtypeStruct((m, w), BF16 if w == IN_GATE else F32) for w in widths],
        compiler_params=_cparams("parallel"),
        name="in_proj",
    )(x, gain, w_bf16, cos, sin)


def _softmax_heads(q, k_bf16, v_bf16, valid):
    lane_head = lax.broadcasted_iota(jnp.int32, q.shape, 1) // A_HEAD_DIM
    scale = 1.0 / math.sqrt(A_HEAD_DIM)
    o_all = jnp.zeros(q.shape, F32)
    lse_all = jnp.zeros(q.shape, F32)
    for h in range(A_HEADS):
        hm = lane_head == h
        qh = jnp.where(hm, q, 0.0).astype(BF16)
        s = _dg(qh, k_bf16, _NT) * scale
        s = jnp.where(valid, s, NEG_BIG)
        mx = jnp.max(s, axis=-1, keepdims=True)
        p = jnp.exp(s - mx)
        l = jnp.sum(p, axis=-1, keepdims=True)
        o = _dg(p.astype(BF16), v_bf16, _NN) / l
        o_all = jnp.where(hm, o, o_all)
        lse_all = jnp.where(hm, mx + jnp.log(l), lse_all)
    return o_all, lse_all


ATT_UNROLL = 4


def _attn_prompt_kernel(q_ref, kh_ref, kc_ref, vh_ref, vc_ref, o_ref, lse_ref, kx_ref, vx_ref, *, d, nsub):
    i = pl.program_id(0)
    hb = d * ATT_BLOCK
    kx_ref[0:hb, :] = kh_ref[...]
    kx_ref[hb:, :] = kc_ref[...]
    vx_ref[0:hb, :] = vh_ref[...]
    vx_ref[hb:, :] = vc_ref[...]
    qi = lax.broadcasted_iota(jnp.int32, (ATT_BLOCK, 2 * ATT_BLOCK), 0)
    kj = lax.broadcasted_iota(jnp.int32, (ATT_BLOCK, 2 * ATT_BLOCK), 1)
    delta = qi + ATT_BLOCK - kj
    band = (delta >= 0) & (delta <= ATT_BLOCK)
    current = kj >= ATT_BLOCK

    def problem(it, carry):
        j = it // d
        start = j * hb + it % d
        valid = band & (current | (i > 0) | (j > 0))
        keys = pl.ds(start, 2 * ATT_BLOCK, stride=d)
        rows = pl.ds(start, ATT_BLOCK, stride=d)
        o, lse = _softmax_heads(q_ref[rows, :], kx_ref[keys, :].astype(BF16), vx_ref[keys, :].astype(BF16), valid)
        o_ref[rows, :] = o
        lse_ref[rows, :] = lse
        return carry

    lax.fori_loop(0, nsub * d, problem, 0, unroll=ATT_UNROLL)


def _attn_prompt(qkv, group):
    t = qkv.shape[0]
    _, d = A_GROUPS[group]
    hb = d * ATT_BLOCK
    nsub = max(1, 8 // d)
    while t % (nsub * hb):
        nsub //= 2
    col = 3 * group
    blk = (nsub * hb, LANES)
    cur = lambda off: (lambda i: (i, col + off))
    halo = lambda off: (lambda i: (jnp.maximum(i * nsub - 1, 0), col + off))
    return pl.pallas_call(
        functools.partial(_attn_prompt_kernel, d=d, nsub=nsub),
        grid=(t // (nsub * hb),),
        in_specs=[pl.BlockSpec(blk, cur(0)), pl.BlockSpec((hb, LANES), halo(1)), pl.BlockSpec(blk, cur(1)),
                  pl.BlockSpec((hb, LANES), halo(2)), pl.BlockSpec(blk, cur(2))],
        out_specs=[pl.BlockSpec(blk, lambda i: (i, 0))] * 2,
        out_shape=[jax.ShapeDtypeStruct((t, LANES), F32)] * 2,
        scratch_shapes=[pltpu.VMEM(((nsub + 1) * hb, LANES), F32)] * 2,
        compiler_params=_cparams("parallel"),
        name=f"attn_prompt_g{group}",
    )(qkv, qkv, qkv, qkv, qkv)


def _attn_sample_kernel(qkv_ref, c0_ref, c1_ref, c2_ref, o_ref, lse_ref, *, ts):
    qkv = qkv_ref[0]
    rows = qkv.shape[0]
    pad = jnp.zeros((LANES - rows, LANES), F32)
    for g, (window, d) in enumerate(A_GROUPS):
        base = 3 * A_WIDTH * g
        q = qkv[:, base:base + LANES]
        k_new = jnp.concatenate([qkv[:, base + LANES:base + 2 * LANES], pad], axis=0)
        v_new = jnp.concatenate([qkv[:, base + 2 * LANES:base + 3 * LANES], pad], axis=0)
        c_ref = (c0_ref, c1_ref, c2_ref)[g]
        k_t = c_ref[0].reshape(A_WIDTH, window).astype(BF16)
        v_t = c_ref[1].reshape(A_WIDTH, window).astype(BF16)
        qi = lax.broadcasted_iota(jnp.int32, (rows, window), 0)
        c = lax.broadcasted_iota(jnp.int32, (rows, window), 1)
        qn = lax.broadcasted_iota(jnp.int32, (rows, LANES), 0)
        cn = lax.broadcasted_iota(jnp.int32, (rows, LANES), 1)
        if d == 1:
            valid_c, valid_n = c >= qi, cn <= qn
        else:
            valid_c, valid_n = (c % d) == qi, cn == qn
        lane_head = lax.broadcasted_iota(jnp.int32, q.shape, 1) // A_HEAD_DIM
        scale = 1.0 / math.sqrt(A_HEAD_DIM)
        o_all = jnp.zeros(q.shape, F32)
        lse_all = jnp.zeros(q.shape, F32)
        k_new = k_new.astype(BF16)
        v_new = v_new.astype(BF16)
        for h in range(A_HEADS):
            hm = lane_head == h
            qh = jnp.where(hm, q, 0.0).astype(BF16)
            s_c = jnp.where(valid_c, _dg(qh, k_t, _NN) * scale, NEG_BIG)
            s_n = jnp.where(valid_n, _dg(qh, k_new, _NT) * scale, NEG_BIG)
            mx = jnp.maximum(jnp.max(s_c, axis=-1, keepdims=True), jnp.max(s_n, axis=-1, keepdims=True))
            p_c = jnp.exp(s_c - mx)
            p_n = jnp.exp(s_n - mx)
            l = jnp.sum(p_c, axis=-1, keepdims=True) + jnp.sum(p_n, axis=-1, keepdims=True)
            o = (_dg(p_c.astype(BF16), v_t, _NT) + _dg(p_n.astype(BF16), v_new, _NN)) / l
            o_all = jnp.where(hm, o, o_all)
            lse_all = jnp.where(hm, mx + jnp.log(l), lse_all)
        o_ref[0, :, g * LANES:(g + 1) * LANES] = o_all
        lse_ref[0, :, g * LANES:(g + 1) * LANES] = lse_all


def _attn_sample(qkv_bm, caches_t, layer, ts):
    bsz, rows, _ = qkv_bm.shape
    specs = [pl.BlockSpec((1, rows, IN_A), lambda b: (b, 0, 0))]
    for (window, d), c in zip(A_GROUPS, caches_t):
        assert c.shape[-1] == window and window % d == 0 and (d == 1 or d >= ts)
        specs.append(pl.BlockSpec((None, None, 2, A_HEADS, A_HEAD_DIM, window),
                                  lambda b: (layer, b, 0, 0, 0, 0)))
    out = jax.ShapeDtypeStruct((bsz, rows, N_A_GROUPS * LANES), F32)
    return pl.pallas_call(
        functools.partial(_attn_sample_kernel, ts=ts),
        grid=(bsz,),
        in_specs=specs,
        out_specs=[pl.BlockSpec((1, rows, N_A_GROUPS * LANES), lambda b: (b, 0, 0))] * 2,
        out_shape=[out, out],
        compiler_params=_cparams("parallel"),
        name="attn_sample",
    )(qkv_bm, *caches_t)


def _conv_b_body(ext_ref, w_ref, b_ref, g_ref, beta_ref, o_ref, *, tm, nb, base, shift_ref=None):
    acc = jnp.zeros((tm, B_WIDTH), F32) + b_ref[...]
    if shift_ref is not None:
        n = shift_ref.shape[1]
        for s in range(1, SUBLANES):
            shift_ref[s] = ext_ref[s:s + n, :]
    for j in range(B_CONV):
        off = base + j * nb
        if shift_ref is None or off % SUBLANES == 0:
            tap = ext_ref[pl.ds(off, tm), :]
        else:
            tap = shift_ref[off % SUBLANES, pl.ds(off - off % SUBLANES, tm), :]
        acc = acc + w_ref[j:j + 1, :] * tap
    mu = jnp.mean(acc, axis=-1, keepdims=True)
    cen = acc - mu
    var = jnp.mean(cen * cen, axis=-1, keepdims=True)
    y = cen * lax.rsqrt(var + LN_EPS) * g_ref[...] + beta_ref[...]
    o_ref[...] = y * _sigmoid(y)


def _conv_b_prompt_kernel(halo_ref, cur_ref, w_ref, b_ref, g_ref, beta_ref, o_ref, ext_ref, shift_ref,
                          *, tm, halo):
    ext_ref[0:halo, :] = jnp.where(pl.program_id(0) > 0, halo_ref[...], 0.0)
    ext_ref[halo:halo + tm, :] = cur_ref[...]
    _conv_b_body(ext_ref, w_ref, b_ref, g_ref, beta_ref, o_ref, tm=tm, nb=1, base=halo - (B_CONV - 1),
                 shift_ref=shift_ref)


def _conv_b_sample_kernel(ctx_ref, cur_ref, w_ref, b_ref, g_ref, beta_ref, o_ref, ext_ref, *, tm, nb):
    n_ctx = (B_CONV - 1) * nb
    ext_ref[0:n_ctx, :] = ctx_ref[...]
    ext_ref[n_ctx:n_ctx + tm, :] = cur_ref[...]
    _conv_b_body(ext_ref, w_ref, b_ref, g_ref, beta_ref, o_ref, tm=tm, nb=nb, base=0)


def _conv_b(glu, ctx, w, b, g, beta, nb):
    m = glu.shape[0]
    fixed = lambda i: (0, 0)
    small = [pl.BlockSpec((B_CONV, B_WIDTH), fixed)] + [pl.BlockSpec((1, B_WIDTH), fixed)] * 3
    if ctx is None:
        tm = min(m, 512)
        halo = 32
        kern = functools.partial(_conv_b_prompt_kernel, tm=tm, halo=halo)
        first = pl.BlockSpec((halo, B_WIDTH), lambda i: (jnp.maximum(i * (tm // halo) - 1, 0), 0))
        lead, ext_rows = glu, halo + tm
        scratch = [pltpu.VMEM((SUBLANES, halo + tm - SUBLANES, B_WIDTH), F32)]
    else:
        tm = m
        kern = functools.partial(_conv_b_sample_kernel, tm=tm, nb=nb)
        first = pl.BlockSpec(ctx.shape, fixed)
        lead, ext_rows = ctx, ctx.shape[0] + tm
        scratch = []
    return pl.pallas_call(
        kern,
        grid=(m // tm,),
        in_specs=[first, pl.BlockSpec((tm, B_WIDTH), lambda i: (i, 0))] + small,
        out_specs=pl.BlockSpec((tm, B_WIDTH), lambda i: (i, 0)),
        out_shape=jax.ShapeDtypeStruct((m, B_WIDTH), F32),
        scratch_shapes=[pltpu.VMEM((ext_rows, B_WIDTH), F32)] + scratch,
        compiler_params=_cparams("parallel"),
        name="conv_b",
    )(lead, glu, w, b, g, beta)


def _rwkv_pre_kernel(prev_ref, pc_ref, mu_ref, w2_ref, a2_ref, g2_ref, w0_ref, a0_ref, kkw_ref, ka_ref,
                     rk_ref, seg_ref, r_o, lw_o, k_o, v_o, kk_o, b_o, g_o, bonus_o, ext_ref,
                     *, tm, nb, off, zero_first):
    prev_rows = prev_ref[...]
    if zero_first:
        prev_rows = jnp.where(pl.program_id(0) > 0, prev_rows, 0.0)
    ext_ref[0:off, :] = prev_rows
    ext_ref[off:off + tm, :] = pc_ref[...]
    pc = pc_ref[...]
    prev = ext_ref[pl.ds(off - nb, tm), :]
    xs = pc + mu_ref[...] * (prev - pc)
    r = xs[:, 0:C_WIDTH]
    k = xs[:, C_WIDTH:2 * C_WIDTH]
    v = xs[:, 2 * C_WIDTH:3 * C_WIDTH]
    wa = xs[:, 3 * C_WIDTH:3 * C_WIDTH + LANES]
    gl = xs[:, 3 * C_WIDTH + LANES:]
    seg = seg_ref[...]
    lw = -DECAY_SCALE * _sigmoid(w0_ref[...] + _mm(jnp.tanh(wa), w2_ref[...], passes=3))
    a = _sigmoid(a0_ref[...] + _mm(wa, a2_ref[...], passes=3))
    g = _mm(_sigmoid(gl), g2_ref[...], passes=3)
    kk = k * kkw_ref[...]
    ss = _mm_exact_rhs(kk * kk, seg, 2)
    kk = kk * lax.rsqrt(jnp.maximum(ss, 1e-24))
    k2 = k * (1.0 + (a - 1.0) * ka_ref[...])
    r_o[...] = r
    lw_o[...] = lw
    k_o[...] = k2
    v_o[...] = v
    kk_o[...] = kk
    b_o[...] = kk * a
    g_o[...] = g
    bonus_o[...] = _mm_exact_rhs(r * k2 * rk_ref[...], seg, 2) * v


def _rwkv_pre(pc, ctx, p, nb):
    m = pc.shape[0]
    tm = min(m, 512)
    off = max(SUBLANES, nb)
    fixed = lambda i: (0, 0)
    row = lambda i: (i, 0)
    if ctx is None:
        lead = pc
        first = pl.BlockSpec((off, IN_C), lambda i: (jnp.maximum(i * (tm // off) - 1, 0), 0))
    else:
        assert m == tm
        lead = ctx
        first = pl.BlockSpec((off, IN_C), fixed)
    vec = pl.BlockSpec((1, C_WIDTH), fixed)
    out = jax.ShapeDtypeStruct((m, C_WIDTH), F32)
    return pl.pallas_call(
        functools.partial(_rwkv_pre_kernel, tm=tm, nb=nb, off=off, zero_first=ctx is None),
        grid=(m // tm,),
        in_specs=[first, pl.BlockSpec((tm, IN_C), row), pl.BlockSpec((1, IN_C), fixed),
                  pl.BlockSpec((LANES, C_WIDTH), fixed), pl.BlockSpec((LANES, C_WIDTH), fixed),
                  pl.BlockSpec((C_GATE_RANK, C_WIDTH), fixed), vec, vec, vec, vec, vec,
                  pl.BlockSpec((C_WIDTH, C_WIDTH), fixed)],
        out_specs=[pl.BlockSpec((tm, C_WIDTH), row)] * 8,
        out_shape=[out] * 8,
        scratch_shapes=[pltpu.VMEM((off + tm, IN_C), F32)],
        compiler_params=_cparams("parallel"),
        name="rwkv_pre",
    )(lead, pc, p['c_mu'], p['c_w2p'], p['c_a2p'], p['c_g2'], p['c_w0'], p['c_a0'], p['c_kk'], p['c_ka'],
      p['c_rk'], p['seg'])


RWKV_PASSES = 1
RWKV_GROUP = 8


_BNN = (((2,), (1,)), ((0,), (0,)))
_BNT = (((2,), (2,)), ((0,), (0,)))
_BTN = (((1,), (1,)), ((0,), (0,)))


def _pair_diag(x):
    first = lax.broadcasted_iota(jnp.int32, x.shape[1:], 1) < x.shape[-1] // 2
    return jnp.concatenate([jnp.where(first, x, 0.0), jnp.where(first, 0.0, x)], axis=1)


def _unit_lower_inverse(a_strict):
    g, c, _ = a_strict.shape
    ri = lax.broadcasted_iota(jnp.int32, (c, 2 * c), 0)
    ci = lax.broadcasted_iota(jnp.int32, (c, 2 * c), 1) % c
    inv = jnp.broadcast_to(jnp.where(ri == ci, 1.0, 0.0).astype(F32), (g, c, 2 * c))
    s = 1
    while s < c:
        same = (ri // (2 * s)) == (ci // (2 * s))
        off = jnp.where(same & ((ri % (2 * s)) >= s) & ((ci % (2 * s)) < s), a_strict, 0.0)
        step = _mm(off, _pair_diag(inv), _BNN, RWKV_PASSES)
        inv = inv - _mm(inv, _pair_diag(step), _BNN, RWKV_PASSES)
        s *= 2
    return inv


def _rwkv_chunk_kernel(r_ref, lw_ref, k_ref, v_ref, kk_ref, b_ref, g_ref, bonus_ref, s0_ref, gng_ref, gnb_ref,
                       o_ref, s_out_ref, state_ref, *, c, nc):
    assert 2 * c == LANES and 2 * C_HEAD_DIM == LANES
    bb = r_ref.shape[0]
    pairs = C_HEADS // 2
    n = bb * pairs
    ci = pl.program_id(1)

    @pl.when(ci == 0)
    def _():
        state_ref[...] = s0_ref[...].reshape(n, C_HEAD_DIM, LANES)

    ri = lax.broadcasted_iota(jnp.int32, (c, LANES), 0)
    cj = lax.broadcasted_iota(jnp.int32, (c, LANES), 1) % c
    incl = ri >= cj
    strict = ri > cj
    tri = jnp.where(lax.broadcasted_iota(jnp.int32, (c, c), 0) >= lax.broadcasted_iota(jnp.int32, (c, c), 1),
                    1.0, 0.0).astype(BF16)

    groups = []
    for ch in range(nc):
        rows = slice(ch * c, (ch + 1) * c)
        for b in range(bb):
            lw = lw_ref[b, rows, :]
            cum = _mm_exact_rhs_left(tri, lw)
            cend = cum[c - 1:c, :]
            w_inv = jnp.exp(-cum)
            w_end = jnp.exp(cend - cum)
            kvec = k_ref[b, rows, :]
            bvec = b_ref[b, rows, :]
            groups.append((kk_ref[b, rows, :] * jnp.exp(cum - lw), r_ref[b, rows, :] * jnp.exp(cum),
                           bvec * w_inv, kvec * w_inv, bvec * w_end, kvec * w_end, v_ref[b, rows, :],
                           jnp.exp(cend), bonus_ref[b, rows, :], g_ref[b, rows, :]))

    def paired(idx):
        return jnp.stack([grp[idx][:, p * LANES:(p + 1) * LANES] for grp in groups for p in range(pairs)])

    P = RWKV_PASSES
    kap, rho, bt, kt, bend, kend, v, wc = [paired(i) for i in range(8)]
    lhs2 = jnp.concatenate([kap, rho], axis=1)
    xb = _mm(lhs2, _pair_diag(bt), _BNT, P)
    xk = _mm(lhs2, _pair_diag(kt), _BNT, P)
    a_kb = jnp.where(strict, xb[:, :c], 0.0)
    a_rb = jnp.where(incl, xb[:, c:], 0.0)
    a_kk = jnp.where(strict, xk[:, :c], 0.0)
    a_rk = jnp.where(incl, xk[:, c:], 0.0)
    t_inv = _unit_lower_inverse(a_kb)
    v_d = _pair_diag(v)
    kap_p = _mm(t_inv, _pair_diag(kap), _BNN, P)
    v_p = _mm(t_inv, _pair_diag(_mm(a_kk, v_d, _BNN, P)), _BNN, P)
    kap_pd = _pair_diag(kap_p)
    er = lax.broadcasted_iota(jnp.int32, (LANES, LANES), 0)
    ec = lax.broadcasted_iota(jnp.int32, (LANES, LANES), 1)
    same_head = (er // C_HEAD_DIM) == (ec // C_HEAD_DIM)
    first = lax.broadcasted_iota(jnp.int32, (C_HEAD_DIM, LANES), 1) < C_HEAD_DIM
    pm = jnp.where(er == ec, wc, 0.0) - jnp.where(same_head, _mm(kap_p, bend, _BTN, P), 0.0)
    fq = _mm(v, kend, _BTN, P) - _mm(v_p, bend, _BTN, P)
    q = jnp.where(first, fq[:, :C_HEAD_DIM], fq[:, C_HEAD_DIM:])
    rp = rho - _mm(a_rb, kap_pd, _BNN, P)
    y0 = _mm(a_rk, v_d, _BNN, P) - _mm(a_rb, _pair_diag(v_p), _BNN, P)

    s = state_ref[...]
    ys = []
    for ch in range(nc):
        sl = slice(ch * n, (ch + 1) * n)
        ys.append(_mm(rp[sl], _pair_diag(s), _BNT, P) + y0[sl])
        s = _mm(s, pm[sl], _BNN, P) + q[sl]
    state_ref[...] = s
    y = jnp.concatenate(ys, axis=0) if nc > 1 else ys[0]

    g_all = y.shape[0]
    seg = jnp.where(same_head, 1.0, 0.0).astype(BF16)
    head_mean = lambda z: (_mm_exact_rhs(z.reshape(g_all * c, LANES), seg, 2) * (1.0 / C_HEAD_DIM)
                           ).reshape(g_all, c, LANES)
    cen = y - head_mean(y)
    out = cen * lax.rsqrt(head_mean(cen * cen) + GN_EPS)
    bonus = paired(8)
    gate = paired(9)
    idx = 0
    for ch in range(nc):
        for b in range(bb):
            for p in range(pairs):
                sl = slice(p * LANES, (p + 1) * LANES)
                yn = out[idx] * gng_ref[:, sl] + gnb_ref[:, sl]
                o_ref[b, ch * c:(ch + 1) * c, sl] = (yn + bonus[idx]) * gate[idx]
                idx += 1

    @pl.when(ci == pl.num_programs(1) - 1)
    def _():
        s_out_ref[...] = s.reshape(bb, pairs, C_HEAD_DIM, LANES)


def _rwkv_chunks(feats, s0, gn_g, gn_b):
    bsz, t, _ = feats[0].shape
    c = RWKV_CHUNK
    pairs = C_HEADS // 2
    bb = RWKV_GROUP if bsz % RWKV_GROUP == 0 else 1
    nc = max(1, min(RWKV_GROUP // bb, t // c))
    seq = pl.BlockSpec((bb, nc * c, C_WIDTH), lambda b, i: (b, i, 0))
    st = pl.BlockSpec((bb, pairs, C_HEAD_DIM, LANES), lambda b, i: (b, 0, 0, 0))
    vec = pl.BlockSpec((1, C_WIDTH), lambda b, i: (0, 0))
    pack = lambda s: jnp.swapaxes(s.reshape(bsz, pairs, 2, C_HEAD_DIM, C_HEAD_DIM), 2, 3).reshape(
        bsz, pairs, C_HEAD_DIM, LANES)
    unpack = lambda s: jnp.swapaxes(s.reshape(bsz, pairs, C_HEAD_DIM, 2, C_HEAD_DIM), 2, 3).reshape(
        bsz, C_HEADS, C_HEAD_DIM, C_HEAD_DIM)
    o_c, s_new = pl.pallas_call(
        functools.partial(_rwkv_chunk_kernel, c=c, nc=nc),
        grid=(bsz // bb, t // (nc * c)),
        in_specs=[seq] * 8 + [st, vec, vec],
        out_specs=[seq, st],
        out_shape=[jax.ShapeDtypeStruct((bsz, t, C_WIDTH), F32),
                   jax.ShapeDtypeStruct((bsz, pairs, C_HEAD_DIM, LANES), F32)],
        scratch_shapes=[pltpu.VMEM((bb * pairs, C_HEAD_DIM, LANES), F32)],
        compiler_params=_cparams("parallel", "arbitrary"),
        name="rwkv_chunks",
    )(*feats, pack(s0), gn_g, gn_b)
    return o_c, unpack(s_new)


def _merge_kernel(o0, l0, o1, l1, o2, l2, ob_ref, oc_ref, gate_ref, x_ref, wa_ref, wb_ref, wc_ref, wo_ref,
                  gain_ref, out_ref):
    ls = [l0[...], l1[...], l2[...]]
    mx = jnp.maximum(jnp.maximum(ls[0], ls[1]), ls[2])
    es = [jnp.exp(l - mx) for l in ls]
    den = es[0] + es[1] + es[2]
    o_a = (es[0] * o0[...] + es[1] * o1[...] + es[2] * o2[...]) / den
    merged = (gate_ref[:, 0:D_MODEL] * _mm(o_a, wa_ref[...])
              + gate_ref[:, D_MODEL:2 * D_MODEL] * _mm(ob_ref[...], wb_ref[...])
              + gate_ref[:, 2 * D_MODEL:] * _mm(oc_ref[...], wc_ref[...]))
    z = _mm(merged, wo_ref[...])
    out_ref[...] = x_ref[...] + _rms(z, gain_ref[...])


def _merge(att, o_b, o_c, gates, x, p):
    m = x.shape[0]
    tm = min(m, 512)
    row = lambda i: (i, 0)
    fixed = lambda i: (0, 0)
    rows = lambda w: pl.BlockSpec((tm, w), row)
    ws = (p['w_br_a'], p['w_br_b'], p['w_br_c'], p['w_out'])
    return pl.pallas_call(
        _merge_kernel,
        grid=(m // tm,),
        in_specs=[rows(LANES)] * 6 + [rows(B_WIDTH), rows(C_WIDTH), rows(IN_GATE), rows(D_MODEL)]
                 + [_layer_weight(a, p['layer']) for a in ws] + [pl.BlockSpec((1, D_MODEL), fixed)],
        out_specs=rows(D_MODEL),
        out_shape=jax.ShapeDtypeStruct((m, D_MODEL), F32),
        compiler_params=_cparams("parallel"),
        name="merge_out",
    )(*att, o_b, o_c, gates, x, *ws, p['norm_mix_post'])


def _gelu_tanh(x):
    return 0.5 * x * (1.0 + jnp.tanh(math.sqrt(2.0 / math.pi) * (x + 0.044715 * (x * x * x))))


FFN_PIECE = 32


def _ffn_kernel(x_ref, g1_ref, fu_ref, w_ref, b_ref, c_ref, fd_ref, g2_ref, out_ref, t_ref, ext_ref, f_ref,
                *, tm, nb, cr):
    @pl.when(pl.program_id(0) == 0)
    def _():
        ext_ref[0:cr, :] = c_ref[...]

    h = _rms(x_ref[...], g1_ref[...]).astype(BF16)
    n_chunks = D_FF // FF_CHUNK
    split = (n_chunks + 1) // 2 * FF_CHUNK
    z = None
    for jc in range(n_chunks):
        pair = (slice(jc * FF_CHUNK, (jc + 1) * FF_CHUNK),
                slice(D_FF + jc * FF_CHUNK, D_FF + (jc + 1) * FF_CHUNK))
        for cols in pair:
            ext_ref[cr:cr + tm, cols] = _dg(h, fu_ref[:, cols], _NN)
        for r0 in range(0, tm, FFN_PIECE):
            cus = []
            for cols in pair:
                cus.append(w_ref[0:1, cols] * ext_ref[r0 + cr - 2 * nb:r0 + cr - 2 * nb + FFN_PIECE, cols]
                           + w_ref[1:2, cols] * ext_ref[r0 + cr - nb:r0 + cr - nb + FFN_PIECE, cols]
                           + w_ref[2:3, cols] * ext_ref[r0 + cr:r0 + cr + FFN_PIECE, cols] + b_ref[:, cols])
            f_ref[r0:r0 + FFN_PIECE, jc * FF_CHUNK:(jc + 1) * FF_CHUNK] = (
                _gelu_tanh(cus[0]) * cus[1]).astype(BF16)
        if (jc + 1) * FF_CHUNK == split:
            z = _dg(f_ref[:, 0:split], fd_ref[0:split, :], _NN)
    z = z + _dg(f_ref[:, split:], fd_ref[split:, :], _NN)
    tail = ext_ref[tm:tm + cr, :]
    t_ref[...] = tail
    ext_ref[0:cr, :] = tail
    out_ref[...] = x_ref[...] + _rms(z, g2_ref[...])


def _ffn(x, ctx, p, nb):
    m = x.shape[0]
    tm = min(m, 512)
    cr = ctx.shape[0]
    row = lambda i: (i, 0)
    fixed = lambda i: (0, 0)
    once = lambda shape: pl.BlockSpec(shape, fixed, pipeline_mode=pl.Buffered(1))
    x2, tail = pl.pallas_call(
        functools.partial(_ffn_kernel, tm=tm, nb=nb, cr=cr),
        grid=(m // tm,),
        in_specs=[pl.BlockSpec((tm, D_MODEL), row), once((1, D_MODEL)),
                  _layer_weight(p['f_up'], p['layer'], pipeline_mode=pl.Buffered(1)),
                  once((F_CONV, 2 * D_FF)), once((1, 2 * D_FF)), once((cr, 2 * D_FF)),
                  _layer_weight(p['f_down'], p['layer'], pipeline_mode=pl.Buffered(1)), once((1, D_MODEL))],
        out_specs=[pl.BlockSpec((tm, D_MODEL), row), pl.BlockSpec((cr, 2 * D_FF), row)],
        out_shape=[jax.ShapeDtypeStruct((m, D_MODEL), F32),
                   jax.ShapeDtypeStruct(((m // tm) * cr, 2 * D_FF), F32)],
        scratch_shapes=[pltpu.VMEM((cr + tm, 2 * D_FF), F32), pltpu.VMEM((tm, D_FF), BF16)],
        compiler_params=_cparams("arbitrary"),
        name="ffn",
    )(x, p['norm_ffn_pre'], p['f_up'], p['f_dw_w'], p['f_dw_b'], ctx, p['f_down'], p['norm_ffn_post'])
    return x2, tail[-cr:]


def _to_time_major(a):
    a = jnp.swapaxes(a, 0, 1)
    return a.reshape((a.shape[0] * a.shape[1],) + a.shape[2:])


def _to_batch_major(a, ts):
    return jnp.swapaxes(a.reshape(ts, a.shape[0] // ts, a.shape[1]), 0, 1)


def _layer_prompt(x, tables, p):
    t = x.shape[0]
    qkv, glu, pc, gates = _in_proj(x, p['norm_mix_pre'], p['w_in'], p['layer'], *tables)
    att = []
    for g in range(N_A_GROUPS):
        att.extend(_attn_prompt(qkv, g))
    o_b = _conv_b(glu, None, p['b_dw_w'], p['b_dw_b'], p['b_ln_g'], p['b_ln_b'], 1)
    feats = _rwkv_pre(pc, None, p, 1)
    s0 = jnp.zeros((1, C_HEADS, C_HEAD_DIM, C_HEAD_DIM), F32)
    o_c, s_new = _rwkv_chunks([f[None] for f in feats], s0, p['c_gn_g'], p['c_gn_b'])
    x1 = _merge(att, o_b, o_c[0], gates, x, p)
    x2, tail = _ffn(x1, jnp.zeros((SUBLANES, 2 * D_FF), F32), p, 1)
    kv = [qkv[t - min(w, t):, 3 * A_WIDTH * g + A_WIDTH:3 * A_WIDTH * (g + 1)]
          .reshape(1, min(w, t), 2, A_HEADS, A_HEAD_DIM) for g, (w, _) in enumerate(A_GROUPS)]
    f_tail = tail[-(F_CONV - 1):][None]
    return x2, kv, glu[t - (B_CONV - 1):][None], pc[t - 1:], s_new, f_tail


def _layer_sample(x, tables, caches_t, layer, b_ctx, c_shift, c_state, f_ctx, p, ts):
    nb = x.shape[0] // ts
    qkv, glu, pc, gates = _in_proj(x, p['norm_mix_pre'], p['w_in'], p['layer'], *tables)
    qkv_bm = _to_batch_major(qkv, ts)
    qkv_pad = jnp.pad(qkv_bm, ((0, 0), (0, SUBLANES - ts), (0, 0)))
    o, lse = _attn_sample(qkv_pad, caches_t, layer, ts)
    att = []
    for g in range(N_A_GROUPS):
        att.append(_to_time_major(o[:, :ts, g * LANES:(g + 1) * LANES]))
        att.append(_to_time_major(lse[:, :ts, g * LANES:(g + 1) * LANES]))
    o_b = _conv_b(glu, _to_time_major(b_ctx), p['b_dw_w'], p['b_dw_b'], p['b_ln_g'], p['b_ln_b'], nb)
    feats = _rwkv_pre(pc, c_shift, p, nb)
    feats_bm = [jnp.pad(_to_batch_major(f, ts), ((0, 0), (0, RWKV_CHUNK - ts), (0, 0))) for f in feats]
    o_c, s_new = _rwkv_chunks(feats_bm, c_state, p['c_gn_g'], p['c_gn_b'])
    x1 = _merge(att, o_b, _to_time_major(o_c[:, :ts]), gates, x, p)
    x2, tail = _ffn(x1, _to_time_major(f_ctx), p, nb)
    kv = [_to_batch_major(qkv[:, 3 * A_WIDTH * g + A_WIDTH:3 * A_WIDTH * (g + 1)], ts)
          .reshape(nb, ts, 2, A_HEADS, A_HEAD_DIM) for g in range(N_A_GROUPS)]
    b_new = jnp.concatenate([b_ctx, _to_batch_major(glu, ts)], axis=1)[:, -(B_CONV - 1):]
    f_new = jnp.concatenate([f_ctx, _to_batch_major(tail, F_CONV - 1)], axis=1)
    return x2, kv, b_new, pc[(ts - 1) * nb:], s_new, f_new[:, -(F_CONV - 1):]


def kernel(x_prompt, x_sample, cache_a_kv0, cache_a_kv1, cache_a_kv2, state_b_conv, state_c_shift, state_c_wkv, state_f_conv, norm_mix_pre, norm_mix_post, norm_ffn_pre, norm_ffn_post, w_in, b_dw_w, b_dw_b, b_ln_g, b_ln_b, c_mu, c_w0, c_w2, c_a0, c_a2, c_g2, c_kk, c_ka, c_rk, c_gn_g, c_gn_b, w_br_a, w_br_b, w_br_c, w_out, f_up, f_dw_w, f_dw_b, f_down):
    depth = w_in.shape[0]
    bp, tp, _ = x_prompt.shape
    bs, ts, _ = x_sample.shape
    assert bp == 1 and ts <= SUBLANES
    caches_t = [jnp.transpose(c, (0, 1, 3, 4, 5, 2)) for c in (cache_a_kv0, cache_a_kv1, cache_a_kv2)]

    head_id = jnp.arange(C_WIDTH) // C_HEAD_DIM
    seg = (head_id[:, None] == head_id[None, :]).astype(BF16)
    zpad = jnp.zeros((LANES - C_DECAY_RANK, C_WIDTH), F32)

    tab_p = _rope_tables(tp, 0, 1)
    tab_s = _rope_tables(ts * bs, PAST_LEN, bs)

    stacked = {name: w.astype(BF16) for name, w in (
        ('w_in', w_in), ('w_br_a', w_br_a), ('w_br_b', w_br_b), ('w_br_c', w_br_c), ('w_out', w_out),
        ('f_up', f_up), ('f_down', f_down))}

    y_p = x_prompt[0]
    y_s = _to_time_major(x_sample)
    outs_p, outs_s = [], []
    for l in range(depth):
        vec = lambda a: a[l][None, :]
        p = {
            **stacked, 'layer': l,
            'norm_mix_pre': vec(norm_mix_pre), 'norm_mix_post': vec(norm_mix_post),
            'norm_ffn_pre': vec(norm_ffn_pre), 'norm_ffn_post': vec(norm_ffn_post),
            'b_dw_w': b_dw_w[l], 'b_dw_b': vec(b_dw_b), 'b_ln_g': vec(b_ln_g), 'b_ln_b': vec(b_ln_b),
            'c_mu': vec(c_mu), 'c_w0': vec(c_w0), 'c_a0': vec(c_a0),
            'c_w2p': jnp.concatenate([c_w2[l], zpad], axis=0),
            'c_a2p': jnp.concatenate([zpad, c_a2[l]], axis=0),
            'c_g2': c_g2[l], 'c_kk': vec(c_kk), 'c_ka': vec(c_ka),
            'c_rk': c_rk[l].reshape(1, C_WIDTH), 'c_gn_g': vec(c_gn_g), 'c_gn_b': vec(c_gn_b),
            'seg': seg,
            'f_dw_w': f_dw_w[l], 'f_dw_b': vec(f_dw_b),
        }
        y_p, *rest_p = _layer_prompt(y_p, tab_p, p)
        outs_p.append(rest_p)
        y_s, *rest_s = _layer_sample(y_s, tab_s, caches_t, l, state_b_conv[l], state_c_shift[l],
                                     state_c_wkv[l], state_f_conv[l], p, ts)
        outs_s.append(rest_s)

    stack = lambda outs, f: jnp.stack([f(o) for o in outs])
    res = [y_p[None], _to_batch_major(y_s, ts)]
    res += [stack(outs_p, lambda o, g=g: o[0][g]) for g in range(N_A_GROUPS)]
    res += [stack(outs_s, lambda o, g=g: o[0][g]) for g in range(N_A_GROUPS)]
    for idx in range(1, 5):
        res += [stack(outs_p, lambda o: o[idx]), stack(outs_s, lambda o: o[idx])]
    return tuple(res)
```

```python
import functools
import math

import jax
import jax.numpy as jnp
from jax import lax
from jax.experimental import pallas as pl
from jax.experimental.pallas import tpu as pltpu

F32 = jnp.float32
BF16 = jnp.bfloat16

D_MODEL = 1024
PAST_LEN = 16384
A_GROUPS = ((128, 1), (512, 4), (2048, 16))
N_A_GROUPS = len(A_GROUPS)
A_HEAD_DIM = 32
A_HEADS = 4
A_WIDTH = A_HEADS * A_HEAD_DIM
ROPE_THETA = 10000.0
ATT_BLOCK = 128
B_WIDTH = 256
B_CONV = 31
C_HEAD_DIM = 64
C_WIDTH = 384
C_HEADS = 6
C_DECAY_RANK = 64
C_ICLR_RANK = 64
C_GATE_RANK = 128
DECAY_SCALE = math.exp(-0.5)
D_FF = 2816
F_CONV = 3
IN_A = N_A_GROUPS * 3 * A_WIDTH
IN_B = 2 * B_WIDTH
IN_C = 3 * C_WIDTH + C_DECAY_RANK + C_ICLR_RANK + C_GATE_RANK
IN_GATE = 3 * D_MODEL
IN_WIDTH = IN_A + IN_B + IN_C + IN_GATE
RMS_EPS = 1e-6
LN_EPS = 1e-5
GN_EPS = C_HEAD_DIM * 1e-5
NEG_BIG = -1e30

LANES = 128
SUBLANES = 8
VMEM_LIMIT = 56 * 1024 * 1024

RWKV_CHUNK = 64
FF_CHUNK = 256


def _cparams(*sem):
    return pltpu.CompilerParams(dimension_semantics=sem, vmem_limit_bytes=VMEM_LIMIT)


def _split_bf16(x, n):
    pieces = []
    rem = x
    for _ in range(n):
        p = rem.astype(BF16)
        pieces.append(p)
        rem = rem - p.astype(F32)
    return pieces


_NN = (((1,), (0,)), ((), ()))
_NT = (((1,), (1,)), ((), ()))


def _dg(a, b, dims):
    return lax.dot_general(a, b, dims, preferred_element_type=F32)


def _mm(a, b, dims=_NN, passes=1):
    if passes == 1:
        return _dg(a.astype(BF16), b.astype(BF16), dims)
    a_hi, a_lo = _split_bf16(a, 2)
    b_hi, b_lo = _split_bf16(b, 2)
    return _dg(a_hi, b_hi, dims) + (_dg(a_hi, b_lo, dims) + _dg(a_lo, b_hi, dims))


def _mm_exact_rhs(a, b_bf16, pieces=3):
    out = None
    for p in _split_bf16(a, pieces):
        t = _dg(p, b_bf16, _NN)
        out = t if out is None else out + t
    return out


def _mm_exact_rhs_left(a_bf16, b, pieces=3):
    out = None
    for p in _split_bf16(b, pieces):
        t = _dg(a_bf16, p, _NN)
        out = t if out is None else out + t
    return out


def _rms(x, g):
    ms = jnp.mean(x * x, axis=-1, keepdims=True)
    return x * lax.rsqrt(ms + RMS_EPS) * g


def _sigmoid(x):
    return 1.0 / (1.0 + jnp.exp(-x))


def _rope_table_kernel(inv_ref, cos_ref, sin_ref, *, tm, pos0, nb):
    row = pl.program_id(0) * tm + lax.broadcasted_iota(jnp.int32, (tm, LANES), 0)
    pos = (pos0 + row // nb).astype(F32)
    ang = pos * inv_ref[...]
    lane = lax.broadcasted_iota(jnp.int32, (tm, LANES), 1)
    sign = jnp.where((lane % A_HEAD_DIM) < A_HEAD_DIM // 2, -1.0, 1.0)
    cos_ref[...] = jnp.cos(ang)
    sin_ref[...] = jnp.sin(ang) * sign


def _rope_tables(m, pos0, nb):
    half = A_HEAD_DIM // 2
    inv = ROPE_THETA ** (-(jnp.arange(half, dtype=F32) * 2.0 / A_HEAD_DIM))
    inv_lane = jnp.tile(inv, 2 * A_HEADS)[None, :]
    tm = min(m, 512)
    return pl.pallas_call(
        functools.partial(_rope_table_kernel, tm=tm, pos0=pos0, nb=nb),
        grid=(m // tm,),
        in_specs=[pl.BlockSpec((1, LANES), lambda i: (0, 0))],
        out_specs=[pl.BlockSpec((tm, LANES), lambda i: (i, 0))] * 2,
        out_shape=[jax.ShapeDtypeStruct((m, LANES), F32)] * 2,
        compiler_params=_cparams("parallel"),
        name="rope_tables",
    )(inv_lane)


def _in_proj_kernel(x_ref, g_ref, w_ref, cos_ref, sin_ref, qkv_ref, glu_ref, pc_ref, gate_ref):
    h = _rms(x_ref[...], g_ref[...]).astype(BF16)
    cos = cos_ref[...]
    sin = sin_ref[...]
    lane = lax.broadcasted_iota(jnp.int32, cos.shape, 1)
    first_half = (lane % A_HEAD_DIM) < A_HEAD_DIM // 2
    pa = _dg(h, w_ref[:, 0:IN_A], _NN)
    for blk in range(IN_A // LANES):
        y = pa[:, blk * LANES:(blk + 1) * LANES]
        if blk % 3 != 2:
            swapped = jnp.where(first_half, pltpu.roll(y, LANES - A_HEAD_DIM // 2, 1),
                                pltpu.roll(y, A_HEAD_DIM // 2, 1))
            y = y * cos + swapped * sin
        qkv_ref[:, blk * LANES:(blk + 1) * LANES] = y
    pb = _dg(h, w_ref[:, IN_A:IN_A + IN_B], _NN)
    glu_ref[...] = pb[:, :B_WIDTH] * _sigmoid(pb[:, B_WIDTH:])
    pc_ref[...] = _dg(h, w_ref[:, IN_A + IN_B:IN_A + IN_B + IN_C], _NN)
    gate_ref[...] = _sigmoid(_dg(h, w_ref[:, IN_A + IN_B + IN_C:], _NN)).astype(gate_ref.dtype)


def _layer_weight(w, layer, **kw):
    return pl.BlockSpec((None,) + w.shape[1:], lambda *_: (layer, 0, 0), **kw)


def _in_proj(x, gain, w_bf16, layer, cos, sin):
    m = x.shape[0]
    tm = min(m, 256)
    row = lambda i: (i, 0)
    fixed = lambda i: (0, 0)
    widths = (IN_A, B_WIDTH, IN_C, IN_GATE)
    return pl.pallas_call(
        _in_proj_kernel,
        grid=(m // tm,),
        in_specs=[pl.BlockSpec((tm, D_MODEL), row), pl.BlockSpec((1, D_MODEL), fixed),
                  _layer_weight(w_bf16, layer, pipeline_mode=pl.Buffered(1)),
                  pl.BlockSpec((tm, LANES), row), pl.BlockSpec((tm, LANES), row)],
        out_specs=[pl.BlockSpec((tm, w), row) for w in widths],
        out_shape=[jax.ShapeDtypeStruct((m, w), BF16 if w == IN_GATE else F32) for w in widths],
        compiler_params=_cparams("parallel"),
        name="in_proj",
    )(x, gain, w_bf16, cos, sin)


def _softmax_heads(q, k_bf16, v_bf16, valid):
    rows = q.shape[0]
    lane_head = lax.broadcasted_iota(jnp.int32, q.shape, 1) // A_HEAD_DIM
    scale = 1.0 / math.sqrt(A_HEAD_DIM)
    q_heads = jnp.concatenate([jnp.where(lane_head == h, q, 0.0) for h in range(A_HEADS)], axis=0)
    s_all = _dg(q_heads.astype(BF16), k_bf16, _NT) * scale
    ps, ls, lses = [], [], []
    for h in range(A_HEADS):
        s = jnp.where(valid, s_all[h * rows:(h + 1) * rows], NEG_BIG)
        mx = jnp.max(s, axis=-1, keepdims=True)
        p = jnp.exp(s - mx)
        l = jnp.sum(p, axis=-1, keepdims=True)
        ps.append(p.astype(BF16))
        ls.append(l)
        lses.append(mx + jnp.log(l))
    o_heads = _dg(jnp.concatenate(ps, axis=0), v_bf16, _NN)
    o_all = jnp.zeros(q.shape, F32)
    lse_all = jnp.zeros(q.shape, F32)
    for h in range(A_HEADS):
        hm = lane_head == h
        o_all = jnp.where(hm, o_heads[h * rows:(h + 1) * rows] / ls[h], o_all)
        lse_all = jnp.where(hm, lses[h], lse_all)
    return o_all, lse_all


ATT_UNROLL = 4


def _attn_prompt_kernel(q_ref, kh_ref, kc_ref, vh_ref, vc_ref, o_ref, lse_ref, kx_ref, vx_ref, *, d, nsub):
    i = pl.program_id(0)
    hb = d * ATT_BLOCK
    kx_ref[0:hb, :] = kh_ref[...]
    kx_ref[hb:, :] = kc_ref[...]
    vx_ref[0:hb, :] = vh_ref[...]
    vx_ref[hb:, :] = vc_ref[...]
    qi = lax.broadcasted_iota(jnp.int32, (ATT_BLOCK, 2 * ATT_BLOCK), 0)
    kj = lax.broadcasted_iota(jnp.int32, (ATT_BLOCK, 2 * ATT_BLOCK), 1)
    delta = qi + ATT_BLOCK - kj
    band = (delta >= 0) & (delta <= ATT_BLOCK)
    current = kj >= ATT_BLOCK

    def problem(it, carry):
        j = it // d
        start = j * hb + it % d
        valid = band & (current | (i > 0) | (j > 0))
        keys = pl.ds(start, 2 * ATT_BLOCK, stride=d)
        rows = pl.ds(start, ATT_BLOCK, stride=d)
        o, lse = _softmax_heads(q_ref[rows, :], kx_ref[keys, :].astype(BF16), vx_ref[keys, :].astype(BF16), valid)
        o_ref[rows, :] = o
        lse_ref[rows, :] = lse
        return carry

    lax.fori_loop(0, nsub * d, problem, 0, unroll=ATT_UNROLL)


def _attn_prompt(qkv, group):
    t = qkv.shape[0]
    _, d = A_GROUPS[group]
    hb = d * ATT_BLOCK
    nsub = max(1, 8 // d)
    while t % (nsub * hb):
        nsub //= 2
    col = 3 * group
    blk = (nsub * hb, LANES)
    cur = lambda off: (lambda i: (i, col + off))
    halo = lambda off: (lambda i: (jnp.maximum(i * nsub - 1, 0), col + off))
    return pl.pallas_call(
        functools.partial(_attn_prompt_kernel, d=d, nsub=nsub),
        grid=(t // (nsub * hb),),
        in_specs=[pl.BlockSpec(blk, cur(0)), pl.BlockSpec((hb, LANES), halo(1)), pl.BlockSpec(blk, cur(1)),
                  pl.BlockSpec((hb, LANES), halo(2)), pl.BlockSpec(blk, cur(2))],
        out_specs=[pl.BlockSpec(blk, lambda i: (i, 0))] * 2,
        out_shape=[jax.ShapeDtypeStruct((t, LANES), F32)] * 2,
        scratch_shapes=[pltpu.VMEM(((nsub + 1) * hb, LANES), F32)] * 2,
        compiler_params=_cparams("parallel"),
        name=f"attn_prompt_g{group}",
    )(qkv, qkv, qkv, qkv, qkv)


def _attn_sample_kernel(qkv_ref, c0_ref, c1_ref, c2_ref, o_ref, lse_ref, *, ts):
    qkv = qkv_ref[0]
    rows = qkv.shape[0]
    pad = jnp.zeros((LANES - rows, LANES), F32)
    for g, (window, d) in enumerate(A_GROUPS):
        base = 3 * A_WIDTH * g
        q = qkv[:, base:base + LANES]
        k_new = jnp.concatenate([qkv[:, base + LANES:base + 2 * LANES], pad], axis=0)
        v_new = jnp.concatenate([qkv[:, base + 2 * LANES:base + 3 * LANES], pad], axis=0)
        c_ref = (c0_ref, c1_ref, c2_ref)[g]
        k_t = c_ref[0].reshape(A_WIDTH, window).astype(BF16)
        v_t = c_ref[1].reshape(A_WIDTH, window).astype(BF16)
        qi = lax.broadcasted_iota(jnp.int32, (rows, window), 0)
        c = lax.broadcasted_iota(jnp.int32, (rows, window), 1)
        qn = lax.broadcasted_iota(jnp.int32, (rows, LANES), 0)
        cn = lax.broadcasted_iota(jnp.int32, (rows, LANES), 1)
        if d == 1:
            valid_c, valid_n = c >= qi, cn <= qn
        else:
            valid_c, valid_n = (c % d) == qi, cn == qn
        lane_head = lax.broadcasted_iota(jnp.int32, q.shape, 1) // A_HEAD_DIM
        scale = 1.0 / math.sqrt(A_HEAD_DIM)
        q_heads = jnp.concatenate([jnp.where(lane_head == h, q, 0.0) for h in range(A_HEADS)],
                                  axis=0).astype(BF16)
        sc_all = _dg(q_heads, k_t, _NN) * scale
        sn_all = _dg(q_heads, k_new.astype(BF16), _NT) * scale
        pcs, pns, ls, lses = [], [], [], []
        for h in range(A_HEADS):
            s_c = jnp.where(valid_c, sc_all[h * rows:(h + 1) * rows], NEG_BIG)
            s_n = jnp.where(valid_n, sn_all[h * rows:(h + 1) * rows], NEG_BIG)
            mx = jnp.maximum(jnp.max(s_c, axis=-1, keepdims=True), jnp.max(s_n, axis=-1, keepdims=True))
            pcs.append(jnp.exp(s_c - mx))
            pns.append(jnp.exp(s_n - mx))
            ls.append(jnp.sum(pcs[-1], axis=-1, keepdims=True) + jnp.sum(pns[-1], axis=-1, keepdims=True))
            lses.append(mx + jnp.log(ls[-1]))
        o_heads = (_dg(jnp.concatenate(pcs, axis=0).astype(BF16), v_t, _NT)
                   + _dg(jnp.concatenate(pns, axis=0).astype(BF16), v_new.astype(BF16), _NN))
        o_all = jnp.zeros(q.shape, F32)
        lse_all = jnp.zeros(q.shape, F32)
        for h in range(A_HEADS):
            hm = lane_head == h
            o_all = jnp.where(hm, o_heads[h * rows:(h + 1) * rows] / ls[h], o_all)
            lse_all = jnp.where(hm, lses[h], lse_all)
        o_ref[0, :, g * LANES:(g + 1) * LANES] = o_all
        lse_ref[0, :, g * LANES:(g + 1) * LANES] = lse_all


def _attn_sample(qkv_bm, caches_t, layer, ts):
    bsz, rows, _ = qkv_bm.shape
    specs = [pl.BlockSpec((1, rows, IN_A), lambda b: (b, 0, 0))]
    for (window, d), c in zip(A_GROUPS, caches_t):
        assert c.shape[-1] == window and window % d == 0 and (d == 1 or d >= ts)
        specs.append(pl.BlockSpec((None, None, 2, A_HEADS, A_HEAD_DIM, window),
                                  lambda b: (layer, b, 0, 0, 0, 0)))
    out = jax.ShapeDtypeStruct((bsz, rows, N_A_GROUPS * LANES), F32)
    return pl.pallas_call(
        functools.partial(_attn_sample_kernel, ts=ts),
        grid=(bsz,),
        in_specs=specs,
        out_specs=[pl.BlockSpec((1, rows, N_A_GROUPS * LANES), lambda b: (b, 0, 0))] * 2,
        out_shape=[out, out],
        compiler_params=_cparams("parallel"),
        name="attn_sample",
    )(qkv_bm, *caches_t)


def _conv_b_body(ext_ref, w_ref, b_ref, g_ref, beta_ref, o_ref, *, tm, nb, base, shift_ref=None):
    acc = jnp.zeros((tm, B_WIDTH), F32) + b_ref[...]
    if shift_ref is not None:
        n = shift_ref.shape[1]
        for s in range(1, SUBLANES):
            shift_ref[s] = ext_ref[s:s + n, :]
    for j in range(B_CONV):
        off = base + j * nb
        if shift_ref is None or off % SUBLANES == 0:
            tap = ext_ref[pl.ds(off, tm), :]
        else:
            tap = shift_ref[off % SUBLANES, pl.ds(off - off % SUBLANES, tm), :]
        acc = acc + w_ref[j:j + 1, :] * tap
    mu = jnp.mean(acc, axis=-1, keepdims=True)
    cen = acc - mu
    var = jnp.mean(cen * cen, axis=-1, keepdims=True)
    y = cen * lax.rsqrt(var + LN_EPS) * g_ref[...] + beta_ref[...]
    o_ref[...] = y * _sigmoid(y)


def _conv_b_prompt_kernel(halo_ref, cur_ref, w_ref, b_ref, g_ref, beta_ref, o_ref, ext_ref, shift_ref,
                          *, tm, halo):
    ext_ref[0:halo, :] = jnp.where(pl.program_id(0) > 0, halo_ref[...], 0.0)
    ext_ref[halo:halo + tm, :] = cur_ref[...]
    _conv_b_body(ext_ref, w_ref, b_ref, g_ref, beta_ref, o_ref, tm=tm, nb=1, base=halo - (B_CONV - 1),
                 shift_ref=shift_ref)


def _conv_b_sample_kernel(ctx_ref, cur_ref, w_ref, b_ref, g_ref, beta_ref, o_ref, ext_ref, *, tm, nb):
    n_ctx = (B_CONV - 1) * nb
    ext_ref[0:n_ctx, :] = ctx_ref[...]
    ext_ref[n_ctx:n_ctx + tm, :] = cur_ref[...]
    _conv_b_body(ext_ref, w_ref, b_ref, g_ref, beta_ref, o_ref, tm=tm, nb=nb, base=0)


def _conv_b(glu, ctx, w, b, g, beta, nb):
    m = glu.shape[0]
    fixed = lambda i: (0, 0)
    small = [pl.BlockSpec((B_CONV, B_WIDTH), fixed)] + [pl.BlockSpec((1, B_WIDTH), fixed)] * 3
    if ctx is None:
        tm = min(m, 512)
        halo = 32
        kern = functools.partial(_conv_b_prompt_kernel, tm=tm, halo=halo)
        first = pl.BlockSpec((halo, B_WIDTH), lambda i: (jnp.maximum(i * (tm // halo) - 1, 0), 0))
        lead, ext_rows = glu, halo + tm
        scratch = [pltpu.VMEM((SUBLANES, halo + tm - SUBLANES, B_WIDTH), F32)]
    else:
        tm = m
        kern = functools.partial(_conv_b_sample_kernel, tm=tm, nb=nb)
        first = pl.BlockSpec(ctx.shape, fixed)
        lead, ext_rows = ctx, ctx.shape[0] + tm
        scratch = []
    return pl.pallas_call(
        kern,
        grid=(m // tm,),
        in_specs=[first, pl.BlockSpec((tm, B_WIDTH), lambda i: (i, 0))] + small,
        out_specs=pl.BlockSpec((tm, B_WIDTH), lambda i: (i, 0)),
        out_shape=jax.ShapeDtypeStruct((m, B_WIDTH), F32),
        scratch_shapes=[pltpu.VMEM((ext_rows, B_WIDTH), F32)] + scratch,
        compiler_params=_cparams("parallel"),
        name="conv_b",
    )(lead, glu, w, b, g, beta)


def _rwkv_pre_kernel(prev_ref, pc_ref, mu_ref, w2_ref, a2_ref, g2_ref, w0_ref, a0_ref, kkw_ref, ka_ref,
                     rk_ref, seg_ref, r_o, lw_o, k_o, v_o, kk_o, b_o, g_o, bonus_o, ext_ref,
                     *, tm, nb, off, zero_first):
    prev_rows = prev_ref[...]
    if zero_first:
        prev_rows = jnp.where(pl.program_id(0) > 0, prev_rows, 0.0)
    ext_ref[0:off, :] = prev_rows
    ext_ref[off:off + tm, :] = pc_ref[...]
    pc = pc_ref[...]
    prev = ext_ref[pl.ds(off - nb, tm), :]
    xs = pc + mu_ref[...] * (prev - pc)
    r = xs[:, 0:C_WIDTH]
    k = xs[:, C_WIDTH:2 * C_WIDTH]
    v = xs[:, 2 * C_WIDTH:3 * C_WIDTH]
    wa = xs[:, 3 * C_WIDTH:3 * C_WIDTH + LANES]
    gl = xs[:, 3 * C_WIDTH + LANES:]
    seg = seg_ref[...]
    lw = -DECAY_SCALE * _sigmoid(w0_ref[...] + _mm(jnp.tanh(wa), w2_ref[...]))
    a = _sigmoid(a0_ref[...] + _mm(wa, a2_ref[...]))
    g = _mm(_sigmoid(gl), g2_ref[...])
    kk = k * kkw_ref[...]
    ss = _mm_exact_rhs(kk * kk, seg, 2)
    kk = kk * lax.rsqrt(jnp.maximum(ss, 1e-24))
    k2 = k * (1.0 + (a - 1.0) * ka_ref[...])
    r_o[...] = r
    lw_o[...] = lw
    k_o[...] = k2
    v_o[...] = v
    kk_o[...] = kk
    b_o[...] = kk * a
    g_o[...] = g
    bonus_o[...] = _mm_exact_rhs(r * k2 * rk_ref[...], seg, 2) * v


def _rwkv_pre(pc, ctx, p, nb):
    m = pc.shape[0]
    tm = min(m, 512)
    off = max(SUBLANES, nb)
    fixed = lambda i: (0, 0)
    row = lambda i: (i, 0)
    if ctx is None:
        lead = pc
        first = pl.BlockSpec((off, IN_C), lambda i: (jnp.maximum(i * (tm // off) - 1, 0), 0))
    else:
        assert m == tm
        lead = ctx
        first = pl.BlockSpec((off, IN_C), fixed)
    vec = pl.BlockSpec((1, C_WIDTH), fixed)
    out = jax.ShapeDtypeStruct((m, C_WIDTH), F32)
    return pl.pallas_call(
        functools.partial(_rwkv_pre_kernel, tm=tm, nb=nb, off=off, zero_first=ctx is None),
        grid=(m // tm,),
        in_specs=[first, pl.BlockSpec((tm, IN_C), row), pl.BlockSpec((1, IN_C), fixed),
                  pl.BlockSpec((LANES, C_WIDTH), fixed), pl.BlockSpec((LANES, C_WIDTH), fixed),
                  pl.BlockSpec((C_GATE_RANK, C_WIDTH), fixed), vec, vec, vec, vec, vec,
                  pl.BlockSpec((C_WIDTH, C_WIDTH), fixed)],
        out_specs=[pl.BlockSpec((tm, C_WIDTH), row)] * 8,
        out_shape=[out] * 8,
        scratch_shapes=[pltpu.VMEM((off + tm, IN_C), F32)],
        compiler_params=_cparams("parallel"),
        name="rwkv_pre",
    )(lead, pc, p['c_mu'], p['c_w2p'], p['c_a2p'], p['c_g2'], p['c_w0'], p['c_a0'], p['c_kk'], p['c_ka'],
      p['c_rk'], p['seg'])


RWKV_PASSES = 1
RWKV_GROUP = 8


_BNN = (((2,), (1,)), ((0,), (0,)))
_BNT = (((2,), (2,)), ((0,), (0,)))
_BTN = (((1,), (1,)), ((0,), (0,)))


def _pair_diag(x):
    first = lax.broadcasted_iota(jnp.int32, x.shape[1:], 1) < x.shape[-1] // 2
    return jnp.concatenate([jnp.where(first, x, 0.0), jnp.where(first, 0.0, x)], axis=1)


def _unit_lower_inverse(a_strict):
    g, c, _ = a_strict.shape
    ri = lax.broadcasted_iota(jnp.int32, (c, 2 * c), 0)
    ci = lax.broadcasted_iota(jnp.int32, (c, 2 * c), 1) % c
    inv = jnp.where(ri == ci, 1.0, 0.0) - jnp.where(((ri // 2) == (ci // 2)) & (ri > ci), a_strict, 0.0)
    s = 2
    while s < c:
        same = (ri // (2 * s)) == (ci // (2 * s))
        off = jnp.where(same & ((ri % (2 * s)) >= s) & ((ci % (2 * s)) < s), a_strict, 0.0)
        step = _mm(off, _pair_diag(inv), _BNN, RWKV_PASSES)
        inv = inv - _mm(inv, _pair_diag(step), _BNN, RWKV_PASSES)
        s *= 2
    return inv


def _rwkv_chunk_kernel(r_ref, lw_ref, k_ref, v_ref, kk_ref, b_ref, g_ref, bonus_ref, s0_ref, gng_ref, gnb_ref,
                       o_ref, s_out_ref, state_ref, *, c, nc):
    assert 2 * c == LANES and 2 * C_HEAD_DIM == LANES
    bb = r_ref.shape[0]
    pairs = C_HEADS // 2
    n = bb * pairs
    ci = pl.program_id(1)

    @pl.when(ci == 0)
    def _():
        for b in range(bb):
            for p in range(pairs):
                state_ref[b * pairs + p] = jnp.concatenate([s0_ref[b, 2 * p], s0_ref[b, 2 * p + 1]], axis=-1)

    ri = lax.broadcasted_iota(jnp.int32, (c, LANES), 0)
    cj = lax.broadcasted_iota(jnp.int32, (c, LANES), 1) % c
    incl = ri >= cj
    strict = ri > cj
    tri = jnp.where(lax.broadcasted_iota(jnp.int32, (c, c), 0) >= lax.broadcasted_iota(jnp.int32, (c, c), 1),
                    1.0, 0.0).astype(BF16)

    groups = []
    for ch in range(nc):
        rows = slice(ch * c, (ch + 1) * c)
        for b in range(bb):
            lw = lw_ref[b, rows, :]
            cum = _mm_exact_rhs_left(tri, lw)
            cend = cum[c - 1:c, :]
            w_inv = jnp.exp(-cum)
            w_end = jnp.exp(cend - cum)
            kvec = k_ref[b, rows, :]
            bvec = b_ref[b, rows, :]
            groups.append((kk_ref[b, rows, :] * jnp.exp(cum - lw), r_ref[b, rows, :] * jnp.exp(cum),
                           bvec * w_inv, kvec * w_inv, bvec * w_end, kvec * w_end, v_ref[b, rows, :],
                           jnp.exp(cend), bonus_ref[b, rows, :], g_ref[b, rows, :]))

    def paired(idx):
        return jnp.stack([grp[idx][:, p * LANES:(p + 1) * LANES] for grp in groups for p in range(pairs)])

    P = RWKV_PASSES
    kap, rho, bt, kt, bend, kend, v, wc = [paired(i) for i in range(8)]
    lhs2 = jnp.concatenate([kap, rho], axis=1)
    xb = _mm(lhs2, _pair_diag(bt), _BNT, P)
    xk = _mm(lhs2, _pair_diag(kt), _BNT, P)
    a_kb = jnp.where(strict, xb[:, :c], 0.0)
    a_rb = jnp.where(incl, xb[:, c:], 0.0)
    a_kk = jnp.where(strict, xk[:, :c], 0.0)
    a_rk = jnp.where(incl, xk[:, c:], 0.0)
    t_inv = _unit_lower_inverse(a_kb)
    v_d = _pair_diag(v)
    kap_p = _mm(t_inv, _pair_diag(kap), _BNN, P)
    v_p = _mm(t_inv, _pair_diag(_mm(a_kk, v_d, _BNN, P)), _BNN, P)
    kap_pd = _pair_diag(kap_p)
    er = lax.broadcasted_iota(jnp.int32, (LANES, LANES), 0)
    ec = lax.broadcasted_iota(jnp.int32, (LANES, LANES), 1)
    same_head = (er // C_HEAD_DIM) == (ec // C_HEAD_DIM)
    first = lax.broadcasted_iota(jnp.int32, (C_HEAD_DIM, LANES), 1) < C_HEAD_DIM
    pm = jnp.where(er == ec, wc, 0.0) - jnp.where(same_head, _mm(kap_p, bend, _BTN, P), 0.0)
    fq = _mm(v, kend, _BTN, P) - _mm(v_p, bend, _BTN, P)
    q = jnp.where(first, fq[:, :C_HEAD_DIM], fq[:, C_HEAD_DIM:])
    rp = rho - _mm(a_rb, kap_pd, _BNN, P)
    y0 = _mm(a_rk, v_d, _BNN, P) - _mm(a_rb, _pair_diag(v_p), _BNN, P)

    s = state_ref[...]
    ys = []
    for ch in range(nc):
        sl = slice(ch * n, (ch + 1) * n)
        ys.append(_mm(rp[sl], _pair_diag(s), _BNT, P) + y0[sl])
        s = _mm(s, pm[sl], _BNN, P) + q[sl]
    state_ref[...] = s
    y = jnp.concatenate(ys, axis=0) if nc > 1 else ys[0]

    g_all = y.shape[0]
    seg = jnp.where(same_head, 1.0, 0.0).astype(BF16)
    head_mean = lambda z: (_mm_exact_rhs(z.reshape(g_all * c, LANES), seg, 2) * (1.0 / C_HEAD_DIM)
                           ).reshape(g_all, c, LANES)
    cen = y - head_mean(y)
    out = cen * lax.rsqrt(head_mean(cen * cen) + GN_EPS)
    bonus = paired(8)
    gate = paired(9)
    idx = 0
    for ch in range(nc):
        for b in range(bb):
            for p in range(pairs):
                sl = slice(p * LANES, (p + 1) * LANES)
                yn = out[idx] * gng_ref[:, sl] + gnb_ref[:, sl]
                o_ref[b, ch * c:(ch + 1) * c, sl] = (yn + bonus[idx]) * gate[idx]
                idx += 1

    @pl.when(ci == pl.num_programs(1) - 1)
    def _():
        for b in range(bb):
            for p in range(pairs):
                s_out_ref[b, 2 * p] = s[b * pairs + p][:, :C_HEAD_DIM]
                s_out_ref[b, 2 * p + 1] = s[b * pairs + p][:, C_HEAD_DIM:]


def _rwkv_chunks(feats, s0, gn_g, gn_b):
    bsz, t, _ = feats[0].shape
    c = RWKV_CHUNK
    pairs = C_HEADS // 2
    bb = RWKV_GROUP if bsz % RWKV_GROUP == 0 else 1
    nc = max(1, min(RWKV_GROUP // bb, t // c))
    seq = pl.BlockSpec((bb, nc * c, C_WIDTH), lambda b, i: (b, i, 0))
    st = pl.BlockSpec((bb, C_HEADS, C_HEAD_DIM, C_HEAD_DIM), lambda b, i: (b, 0, 0, 0))
    vec = pl.BlockSpec((1, C_WIDTH), lambda b, i: (0, 0))
    return pl.pallas_call(
        functools.partial(_rwkv_chunk_kernel, c=c, nc=nc),
        grid=(bsz // bb, t // (nc * c)),
        in_specs=[seq] * 8 + [st, vec, vec],
        out_specs=[seq, st],
        out_shape=[jax.ShapeDtypeStruct((bsz, t, C_WIDTH), F32),
                   jax.ShapeDtypeStruct((bsz, C_HEADS, C_HEAD_DIM, C_HEAD_DIM), F32)],
        scratch_shapes=[pltpu.VMEM((bb * pairs, C_HEAD_DIM, LANES), F32)],
        compiler_params=_cparams("parallel", "arbitrary"),
        name="rwkv_chunks",
    )(*feats, s0, gn_g, gn_b)


def _merge_kernel(o0, l0, o1, l1, o2, l2, ob_ref, oc_ref, gate_ref, x_ref, wa_ref, wb_ref, wc_ref, wo_ref,
                  gain_ref, out_ref):
    ls = [l0[...], l1[...], l2[...]]
    mx = jnp.maximum(jnp.maximum(ls[0], ls[1]), ls[2])
    es = [jnp.exp(l - mx) for l in ls]
    den = es[0] + es[1] + es[2]
    o_a = (es[0] * o0[...] + es[1] * o1[...] + es[2] * o2[...]) / den
    merged = (gate_ref[:, 0:D_MODEL] * _mm(o_a, wa_ref[...])
              + gate_ref[:, D_MODEL:2 * D_MODEL] * _mm(ob_ref[...], wb_ref[...])
              + gate_ref[:, 2 * D_MODEL:] * _mm(oc_ref[...], wc_ref[...]))
    z = _mm(merged, wo_ref[...])
    out_ref[...] = x_ref[...] + _rms(z, gain_ref[...])


def _merge(att, o_b, o_c, gates, x, p):
    m = x.shape[0]
    tm = min(m, 512)
    row = lambda i: (i, 0)
    fixed = lambda i: (0, 0)
    rows = lambda w: pl.BlockSpec((tm, w), row)
    ws = (p['w_br_a'], p['w_br_b'], p['w_br_c'], p['w_out'])
    return pl.pallas_call(
        _merge_kernel,
        grid=(m // tm,),
        in_specs=[rows(LANES)] * 6 + [rows(B_WIDTH), rows(C_WIDTH), rows(IN_GATE), rows(D_MODEL)]
                 + [_layer_weight(a, p['layer']) for a in ws] + [pl.BlockSpec((1, D_MODEL), fixed)],
        out_specs=rows(D_MODEL),
        out_shape=jax.ShapeDtypeStruct((m, D_MODEL), F32),
        compiler_params=_cparams("parallel"),
        name="merge_out",
    )(*att, o_b, o_c, gates, x, *ws, p['norm_mix_post'])


def _gelu_tanh(x):
    return 0.5 * x * (1.0 + jnp.tanh(math.sqrt(2.0 / math.pi) * (x + 0.044715 * (x * x * x))))


FFN_PIECE = 32


def _ffn_kernel(x_ref, g1_ref, fu_ref, w_ref, b_ref, c_ref, fd_ref, g2_ref, out_ref, t_ref, ext_ref, f_ref,
                *, tm, nb, cr):
    @pl.when(pl.program_id(0) == 0)
    def _():
        ext_ref[0:cr, :] = c_ref[...]

    h = _rms(x_ref[...], g1_ref[...]).astype(BF16)
    n_chunks = D_FF // FF_CHUNK
    split = (n_chunks + 1) // 2 * FF_CHUNK
    z = None
    for jc in range(n_chunks):
        pair = (slice(jc * FF_CHUNK, (jc + 1) * FF_CHUNK),
                slice(D_FF + jc * FF_CHUNK, D_FF + (jc + 1) * FF_CHUNK))
        for cols in pair:
            ext_ref[cr:cr + tm, cols] = _dg(h, fu_ref[:, cols], _NN)
        for r0 in range(0, tm, FFN_PIECE):
            cus = []
            for cols in pair:
                cus.append(w_ref[0:1, cols] * ext_ref[r0 + cr - 2 * nb:r0 + cr - 2 * nb + FFN_PIECE, cols]
                           + w_ref[1:2, cols] * ext_ref[r0 + cr - nb:r0 + cr - nb + FFN_PIECE, cols]
                           + w_ref[2:3, cols] * ext_ref[r0 + cr:r0 + cr + FFN_PIECE, cols] + b_ref[:, cols])
            f_ref[r0:r0 + FFN_PIECE, jc * FF_CHUNK:(jc + 1) * FF_CHUNK] = (
                _gelu_tanh(cus[0]) * cus[1]).astype(BF16)
        if (jc + 1) * FF_CHUNK == split:
            z = _dg(f_ref[:, 0:split], fd_ref[0:split, :], _NN)
    z = z + _dg(f_ref[:, split:], fd_ref[split:, :], _NN)
    tail = ext_ref[tm:tm + cr, :]
    t_ref[...] = tail
    ext_ref[0:cr, :] = tail
    out_ref[...] = x_ref[...] + _rms(z, g2_ref[...])


def _ffn(x, ctx, p, nb):
    m = x.shape[0]
    tm = min(m, 512)
    cr = ctx.shape[0]
    row = lambda i: (i, 0)
    fixed = lambda i: (0, 0)
    once = lambda shape: pl.BlockSpec(shape, fixed, pipeline_mode=pl.Buffered(1))
    x2, tail = pl.pallas_call(
        functools.partial(_ffn_kernel, tm=tm, nb=nb, cr=cr),
        grid=(m // tm,),
        in_specs=[pl.BlockSpec((tm, D_MODEL), row), once((1, D_MODEL)),
                  _layer_weight(p['f_up'], p['layer'], pipeline_mode=pl.Buffered(1)),
                  once((F_CONV, 2 * D_FF)), once((1, 2 * D_FF)), once((cr, 2 * D_FF)),
                  _layer_weight(p['f_down'], p['layer'], pipeline_mode=pl.Buffered(1)), once((1, D_MODEL))],
        out_specs=[pl.BlockSpec((tm, D_MODEL), row), pl.BlockSpec((cr, 2 * D_FF), row)],
        out_shape=[jax.ShapeDtypeStruct((m, D_MODEL), F32),
                   jax.ShapeDtypeStruct(((m // tm) * cr, 2 * D_FF), F32)],
        scratch_shapes=[pltpu.VMEM((cr + tm, 2 * D_FF), F32), pltpu.VMEM((tm, D_FF), BF16)],
        compiler_params=_cparams("arbitrary"),
        name="ffn",
    )(x, p['norm_ffn_pre'], p['f_up'], p['f_dw_w'], p['f_dw_b'], ctx, p['f_down'], p['norm_ffn_post'])
    return x2, tail[-cr:]


def _to_time_major(a):
    a = jnp.swapaxes(a, 0, 1)
    return a.reshape((a.shape[0] * a.shape[1],) + a.shape[2:])


def _to_batch_major(a, ts):
    return jnp.swapaxes(a.reshape(ts, a.shape[0] // ts, a.shape[1]), 0, 1)


def _layer_prompt(x, tables, p):
    t = x.shape[0]
    qkv, glu, pc, gates = _in_proj(x, p['norm_mix_pre'], p['w_in'], p['layer'], *tables)
    att = []
    for g in range(N_A_GROUPS):
        att.extend(_attn_prompt(qkv, g))
    o_b = _conv_b(glu, None, p['b_dw_w'], p['b_dw_b'], p['b_ln_g'], p['b_ln_b'], 1)
    feats = _rwkv_pre(pc, None, p, 1)
    s0 = jnp.zeros((1, C_HEADS, C_HEAD_DIM, C_HEAD_DIM), F32)
    o_c, s_new = _rwkv_chunks([f[None] for f in feats], s0, p['c_gn_g'], p['c_gn_b'])
    x1 = _merge(att, o_b, o_c[0], gates, x, p)
    x2, tail = _ffn(x1, jnp.zeros((SUBLANES, 2 * D_FF), F32), p, 1)
    kv = [qkv[t - min(w, t):, 3 * A_WIDTH * g + A_WIDTH:3 * A_WIDTH * (g + 1)]
          .reshape(1, min(w, t), 2, A_HEADS, A_HEAD_DIM) for g, (w, _) in enumerate(A_GROUPS)]
    f_tail = tail[-(F_CONV - 1):][None]
    return x2, kv, glu[t - (B_CONV - 1):][None], pc[t - 1:], s_new, f_tail


def _layer_sample(x, tables, caches_t, layer, b_ctx, c_shift, c_state, f_ctx, p, ts):
    nb = x.shape[0] // ts
    qkv, glu, pc, gates = _in_proj(x, p['norm_mix_pre'], p['w_in'], p['layer'], *tables)
    qkv_bm = _to_batch_major(qkv, ts)
    qkv_pad = jnp.pad(qkv_bm, ((0, 0), (0, SUBLANES - ts), (0, 0)))
    o, lse = _attn_sample(qkv_pad, caches_t, layer, ts)
    att = []
    for g in range(N_A_GROUPS):
        att.append(_to_time_major(o[:, :ts, g * LANES:(g + 1) * LANES]))
        att.append(_to_time_major(lse[:, :ts, g * LANES:(g + 1) * LANES]))
    o_b = _conv_b(glu, _to_time_major(b_ctx), p['b_dw_w'], p['b_dw_b'], p['b_ln_g'], p['b_ln_b'], nb)
    feats = _rwkv_pre(pc, c_shift, p, nb)
    feats_bm = [jnp.pad(_to_batch_major(f, ts), ((0, 0), (0, RWKV_CHUNK - ts), (0, 0))) for f in feats]
    o_c, s_new = _rwkv_chunks(feats_bm, c_state, p['c_gn_g'], p['c_gn_b'])
    x1 = _merge(att, o_b, _to_time_major(o_c[:, :ts]), gates, x, p)
    x2, tail = _ffn(x1, _to_time_major(f_ctx), p, nb)
    kv = [_to_batch_major(qkv[:, 3 * A_WIDTH * g + A_WIDTH:3 * A_WIDTH * (g + 1)], ts)
          .reshape(nb, ts, 2, A_HEADS, A_HEAD_DIM) for g in range(N_A_GROUPS)]
    b_new = jnp.concatenate([b_ctx, _to_batch_major(glu, ts)], axis=1)[:, -(B_CONV - 1):]
    f_new = jnp.concatenate([f_ctx, _to_batch_major(tail, F_CONV - 1)], axis=1)
    return x2, kv, b_new, pc[(ts - 1) * nb:], s_new, f_new[:, -(F_CONV - 1):]


def kernel(x_prompt, x_sample, cache_a_kv0, cache_a_kv1, cache_a_kv2, state_b_conv, state_c_shift, state_c_wkv, state_f_conv, norm_mix_pre, norm_mix_post, norm_ffn_pre, norm_ffn_post, w_in, b_dw_w, b_dw_b, b_ln_g, b_ln_b, c_mu, c_w0, c_w2, c_a0, c_a2, c_g2, c_kk, c_ka, c_rk, c_gn_g, c_gn_b, w_br_a, w_br_b, w_br_c, w_out, f_up, f_dw_w, f_dw_b, f_down):
    depth = w_in.shape[0]
    bp, tp, _ = x_prompt.shape
    bs, ts, _ = x_sample.shape
    assert bp == 1 and ts <= SUBLANES
    caches_t = [jnp.transpose(c, (0, 1, 3, 4, 5, 2)) for c in (cache_a_kv0, cache_a_kv1, cache_a_kv2)]

    head_id = jnp.arange(C_WIDTH) // C_HEAD_DIM
    seg = (head_id[:, None] == head_id[None, :]).astype(BF16)
    zpad = jnp.zeros((LANES - C_DECAY_RANK, C_WIDTH), F32)

    tab_p = _rope_tables(tp, 0, 1)
    tab_s = _rope_tables(ts * bs, PAST_LEN, bs)

    stacked = {name: w.astype(BF16) for name, w in (
        ('w_in', w_in), ('w_br_a', w_br_a), ('w_br_b', w_br_b), ('w_br_c', w_br_c), ('w_out', w_out),
        ('f_up', f_up), ('f_down', f_down))}

    y_p = x_prompt[0]
    y_s = _to_time_major(x_sample)
    outs_p, outs_s = [], []
    for l in range(depth):
        vec = lambda a: a[l][None, :]
        p = {
            **stacked, 'layer': l,
            'norm_mix_pre': vec(norm_mix_pre), 'norm_mix_post': vec(norm_mix_post),
            'norm_ffn_pre': vec(norm_ffn_pre), 'norm_ffn_post': vec(norm_ffn_post),
            'b_dw_w': b_dw_w[l], 'b_dw_b': vec(b_dw_b), 'b_ln_g': vec(b_ln_g), 'b_ln_b': vec(b_ln_b),
            'c_mu': vec(c_mu), 'c_w0': vec(c_w0), 'c_a0': vec(c_a0),
            'c_w2p': jnp.concatenate([c_w2[l], zpad], axis=0),
            'c_a2p': jnp.concatenate([zpad, c_a2[l]], axis=0),
            'c_g2': c_g2[l], 'c_kk': vec(c_kk), 'c_ka': vec(c_ka),
            'c_rk': c_rk[l].reshape(1, C_WIDTH), 'c_gn_g': vec(c_gn_g), 'c_gn_b': vec(c_gn_b),
            'seg': seg,
            'f_dw_w': f_dw_w[l], 'f_dw_b': vec(f_dw_b),
        }
        y_p, *rest_p = _layer_prompt(y_p, tab_p, p)
        outs_p.append(rest_p)
        y_s, *rest_s = _layer_sample(y_s, tab_s, caches_t, l, state_b_conv[l], state_c_shift[l],
                                     state_c_wkv[l], state_f_conv[l], p, ts)
        outs_s.append(rest_s)

    stack = lambda outs, f: jnp.stack([f(o) for o in outs])
    res = [y_p[None], _to_batch_major(y_s, ts)]
    res += [stack(outs_p, lambda o, g=g: o[0][g]) for g in range(N_A_GROUPS)]
    res += [stack(outs_s, lambda o, g=g: o[0][g]) for g in range(N_A_GROUPS)]
    for idx in range(1, 5):
        res += [stack(outs_p, lambda o: o[idx]), stack(outs_s, lambda o: o[idx])]
    return tuple(res)
```

```python
import functools
import math

import jax
import jax.numpy as jnp
from jax import lax
from jax.experimental import pallas as pl
from jax.experimental.pallas import tpu as pltpu

F32 = jnp.float32
BF16 = jnp.bfloat16

D_MODEL = 1024
PAST_LEN = 16384
A_GROUPS = ((128, 1), (512, 4), (2048, 16))
N_A_GROUPS = len(A_GROUPS)
A_HEAD_DIM = 32
A_HEADS = 4
A_WIDTH = A_HEADS * A_HEAD_DIM
ROPE_THETA = 10000.0
ATT_BLOCK = 128
B_WIDTH = 256
B_CONV = 31
C_HEAD_DIM = 64
C_WIDTH = 384
C_HEADS = 6
C_DECAY_RANK = 64
C_ICLR_RANK = 64
C_GATE_RANK = 128
DECAY_SCALE = math.exp(-0.5)
D_FF = 2816
F_CONV = 3
IN_A = N_A_GROUPS * 3 * A_WIDTH
IN_B = 2 * B_WIDTH
IN_C = 3 * C_WIDTH + C_DECAY_RANK + C_ICLR_RANK + C_GATE_RANK
IN_GATE = 3 * D_MODEL
IN_WIDTH = IN_A + IN_B + IN_C + IN_GATE
RMS_EPS = 1e-6
LN_EPS = 1e-5
GN_EPS = C_HEAD_DIM * 1e-5
NEG_BIG = -1e30

LANES = 128
SUBLANES = 8
VMEM_LIMIT = 56 * 1024 * 1024

RWKV_CHUNK = 64
FF_CHUNK = 256


def _cparams(*sem):
    return pltpu.CompilerParams(dimension_semantics=sem, vmem_limit_bytes=VMEM_LIMIT)


def _split_bf16(x, n):
    pieces = []
    rem = x
    for _ in range(n):
        p = rem.astype(BF16)
        pieces.append(p)
        rem = rem - p.astype(F32)
    return pieces


_NN = (((1,), (0,)), ((), ()))
_NT = (((1,), (1,)), ((), ()))


def _dg(a, b, dims):
    return lax.dot_general(a, b, dims, preferred_element_type=F32)


def _mm(a, b, dims=_NN, passes=1):
    if passes == 1:
        return _dg(a.astype(BF16), b.astype(BF16), dims)
    a_hi, a_lo = _split_bf16(a, 2)
    b_hi, b_lo = _split_bf16(b, 2)
    return _dg(a_hi, b_hi, dims) + (_dg(a_hi, b_lo, dims) + _dg(a_lo, b_hi, dims))


def _mm_exact_rhs(a, b_bf16, pieces=3):
    out = None
    for p in _split_bf16(a, pieces):
        t = _dg(p, b_bf16, _NN)
        out = t if out is None else out + t
    return out


def _mm_exact_rhs_left(a_bf16, b, pieces=3):
    out = None
    for p in _split_bf16(b, pieces):
        t = _dg(a_bf16, p, _NN)
        out = t if out is None else out + t
    return out


def _rms(x, g):
    ms = jnp.mean(x * x, axis=-1, keepdims=True)
    return x * lax.rsqrt(ms + RMS_EPS) * g


def _sigmoid(x):
    return 1.0 / (1.0 + jnp.exp(-x))


def _rope_table_kernel(inv_ref, cos_ref, sin_ref, *, tm, pos0, nb):
    row = pl.program_id(0) * tm + lax.broadcasted_iota(jnp.int32, (tm, LANES), 0)
    pos = (pos0 + row // nb).astype(F32)
    ang = pos * inv_ref[...]
    lane = lax.broadcasted_iota(jnp.int32, (tm, LANES), 1)
    sign = jnp.where((lane % A_HEAD_DIM) < A_HEAD_DIM // 2, -1.0, 1.0)
    cos_ref[...] = jnp.cos(ang)
    sin_ref[...] = jnp.sin(ang) * sign


def _rope_tables(m, pos0, nb):
    half = A_HEAD_DIM // 2
    inv = ROPE_THETA ** (-(jnp.arange(half, dtype=F32) * 2.0 / A_HEAD_DIM))
    inv_lane = jnp.tile(inv, 2 * A_HEADS)[None, :]
    tm = min(m, 512)
    return pl.pallas_call(
        functools.partial(_rope_table_kernel, tm=tm, pos0=pos0, nb=nb),
        grid=(m // tm,),
        in_specs=[pl.BlockSpec((1, LANES), lambda i: (0, 0))],
        out_specs=[pl.BlockSpec((tm, LANES), lambda i: (i, 0))] * 2,
        out_shape=[jax.ShapeDtypeStruct((m, LANES), F32)] * 2,
        compiler_params=_cparams("parallel"),
        name="rope_tables",
    )(inv_lane)


N_C_PARAMS = 10


def _in_proj_kernel(x_ref, g_ref, w_ref, cos_ref, sin_ref, bctx_ref, bw_ref, bb_ref, bg_ref, bbeta_ref, cctx_ref,
                    *rest, tm, nb, bctx, coff, carry):
    c_params = rest[:N_C_PARAMS]
    qkv_ref, glu_ref, pc_ref, gate_ref, ob_ref = rest[N_C_PARAMS:N_C_PARAMS + 5]
    c_outs = rest[N_C_PARAMS + 5:N_C_PARAMS + 13]
    extb_ref, extc_ref = rest[N_C_PARAMS + 13:N_C_PARAMS + 15]
    shift_ref = rest[N_C_PARAMS + 15] if len(rest) > N_C_PARAMS + 15 else None

    @pl.when(pl.program_id(0) == 0)
    def _():
        extb_ref[0:bctx, :] = bctx_ref[...]
        extc_ref[0:coff, :] = cctx_ref[...]

    h = _rms(x_ref[...], g_ref[...]).astype(BF16)
    cos = cos_ref[...]
    sin = sin_ref[...]
    lane = lax.broadcasted_iota(jnp.int32, cos.shape, 1)
    first_half = (lane % A_HEAD_DIM) < A_HEAD_DIM // 2
    pa = _dg(h, w_ref[:, 0:IN_A], _NN)
    for blk in range(IN_A // LANES):
        y = pa[:, blk * LANES:(blk + 1) * LANES]
        if blk % 3 != 2:
            swapped = jnp.where(first_half, pltpu.roll(y, LANES - A_HEAD_DIM // 2, 1),
                                pltpu.roll(y, A_HEAD_DIM // 2, 1))
            y = y * cos + swapped * sin
        qkv_ref[:, blk * LANES:(blk + 1) * LANES] = y
    pb = _dg(h, w_ref[:, IN_A:IN_A + IN_B], _NN)
    glu = pb[:, :B_WIDTH] * _sigmoid(pb[:, B_WIDTH:])
    glu_ref[...] = glu
    extb_ref[bctx:bctx + tm, :] = glu
    _conv_b_body(extb_ref, bw_ref, bb_ref, bg_ref, bbeta_ref, ob_ref, tm=tm, nb=nb,
                 base=bctx - (B_CONV - 1) * nb, shift_ref=shift_ref)
    pc = _dg(h, w_ref[:, IN_A + IN_B:IN_A + IN_B + IN_C], _NN)
    pc_ref[...] = pc
    extc_ref[coff:coff + tm, :] = pc
    _rwkv_feature_rows(pc, extc_ref[coff - nb:coff - nb + tm, :], *c_params, *c_outs)
    gate_ref[...] = _sigmoid(_dg(h, w_ref[:, IN_A + IN_B + IN_C:], _NN)).astype(gate_ref.dtype)
    if carry:
        extb_ref[0:bctx, :] = extb_ref[tm:tm + bctx, :]
        extc_ref[0:coff, :] = extc_ref[tm:tm + coff, :]


def _layer_weight(w, layer, **kw):
    return pl.BlockSpec((None,) + w.shape[1:], lambda *_: (layer, 0, 0), **kw)


def _in_proj(x, tables, b_ctx, c_ctx, p, nb):
    m = x.shape[0]
    tm = min(m, 256)
    bctx, coff = b_ctx.shape[0], c_ctx.shape[0]
    row = lambda i: (i, 0)
    fixed = lambda i: (0, 0)
    full = lambda a: pl.BlockSpec(a.shape, fixed)
    widths = (IN_A, B_WIDTH, IN_C, IN_GATE, B_WIDTH) + (C_WIDTH,) * 8
    b_params = (b_ctx, p['b_dw_w'], p['b_dw_b'], p['b_ln_g'], p['b_ln_b'])
    c_params = (c_ctx, p['c_mu'], p['c_w2p'], p['c_a2p'], p['c_g2'], p['c_w0'], p['c_a0'], p['c_kk'], p['c_ka'],
                p['c_rk'], p['seg'])
    scratch = [pltpu.VMEM((bctx + tm, B_WIDTH), F32), pltpu.VMEM((coff + tm, IN_C), F32)]
    if nb == 1:
        scratch.append(pltpu.VMEM((SUBLANES, bctx + tm - SUBLANES, B_WIDTH), F32))
    outs = pl.pallas_call(
        functools.partial(_in_proj_kernel, tm=tm, nb=nb, bctx=bctx, coff=coff, carry=m > tm),
        grid=(m // tm,),
        in_specs=[pl.BlockSpec((tm, D_MODEL), row), pl.BlockSpec((1, D_MODEL), fixed),
                  _layer_weight(p['w_in'], p['layer'], pipeline_mode=pl.Buffered(1)),
                  pl.BlockSpec((tm, LANES), row), pl.BlockSpec((tm, LANES), row)]
                 + [full(a) for a in b_params + c_params],
        out_specs=[pl.BlockSpec((tm, w), row) for w in widths],
        out_shape=[jax.ShapeDtypeStruct((m, w), BF16 if w == IN_GATE else F32) for w in widths],
        scratch_shapes=scratch,
        compiler_params=_cparams("arbitrary"),
        name="in_proj",
    )(x, p['norm_mix_pre'], p['w_in'], *tables, *b_params, *c_params)
    return outs[0], outs[1], outs[2], outs[3], outs[4], outs[5:]


def _softmax_heads(q, k_bf16, v_bf16, valid):
    rows = q.shape[0]
    lane_head = lax.broadcasted_iota(jnp.int32, q.shape, 1) // A_HEAD_DIM
    scale = 1.0 / math.sqrt(A_HEAD_DIM)
    q_heads = jnp.concatenate([jnp.where(lane_head == h, q, 0.0) for h in range(A_HEADS)], axis=0)
    s_all = _dg(q_heads.astype(BF16), k_bf16, _NT) * scale
    ps, ls, lses = [], [], []
    for h in range(A_HEADS):
        s = jnp.where(valid, s_all[h * rows:(h + 1) * rows], NEG_BIG)
        mx = jnp.max(s, axis=-1, keepdims=True)
        p = jnp.exp(s - mx)
        l = jnp.sum(p, axis=-1, keepdims=True)
        ps.append(p.astype(BF16))
        ls.append(l)
        lses.append(mx + jnp.log(l))
    o_heads = _dg(jnp.concatenate(ps, axis=0), v_bf16, _NN)
    o_all = jnp.zeros(q.shape, F32)
    lse_all = jnp.zeros(q.shape, F32)
    for h in range(A_HEADS):
        hm = lane_head == h
        o_all = jnp.where(hm, o_heads[h * rows:(h + 1) * rows] / ls[h], o_all)
        lse_all = jnp.where(hm, lses[h], lse_all)
    return o_all, lse_all


ATT_UNROLL = 4


def _attn_prompt_kernel(q_ref, kh_ref, kc_ref, vh_ref, vc_ref, o_ref, lse_ref, kx_ref, vx_ref, *, d, nsub):
    i = pl.program_id(0)
    hb = d * ATT_BLOCK
    kx_ref[0:hb, :] = kh_ref[...]
    kx_ref[hb:, :] = kc_ref[...]
    vx_ref[0:hb, :] = vh_ref[...]
    vx_ref[hb:, :] = vc_ref[...]
    qi = lax.broadcasted_iota(jnp.int32, (ATT_BLOCK, 2 * ATT_BLOCK), 0)
    kj = lax.broadcasted_iota(jnp.int32, (ATT_BLOCK, 2 * ATT_BLOCK), 1)
    delta = qi + ATT_BLOCK - kj
    band = (delta >= 0) & (delta <= ATT_BLOCK)
    current = kj >= ATT_BLOCK

    def problem(it, carry):
        j = it // d
        start = j * hb + it % d
        valid = band & (current | (i > 0) | (j > 0))
        keys = pl.ds(start, 2 * ATT_BLOCK, stride=d)
        rows = pl.ds(start, ATT_BLOCK, stride=d)
        o, lse = _softmax_heads(q_ref[rows, :], kx_ref[keys, :].astype(BF16), vx_ref[keys, :].astype(BF16), valid)
        o_ref[rows, :] = o
        lse_ref[rows, :] = lse
        return carry

    lax.fori_loop(0, nsub * d, problem, 0, unroll=ATT_UNROLL)


def _attn_prompt(qkv, group):
    t = qkv.shape[0]
    _, d = A_GROUPS[group]
    hb = d * ATT_BLOCK
    nsub = max(1, 8 // d)
    while t % (nsub * hb):
        nsub //= 2
    col = 3 * group
    blk = (nsub * hb, LANES)
    cur = lambda off: (lambda i: (i, col + off))
    halo = lambda off: (lambda i: (jnp.maximum(i * nsub - 1, 0), col + off))
    return pl.pallas_call(
        functools.partial(_attn_prompt_kernel, d=d, nsub=nsub),
        grid=(t // (nsub * hb),),
        in_specs=[pl.BlockSpec(blk, cur(0)), pl.BlockSpec((hb, LANES), halo(1)), pl.BlockSpec(blk, cur(1)),
                  pl.BlockSpec((hb, LANES), halo(2)), pl.BlockSpec(blk, cur(2))],
        out_specs=[pl.BlockSpec(blk, lambda i: (i, 0))] * 2,
        out_shape=[jax.ShapeDtypeStruct((t, LANES), F32)] * 2,
        scratch_shapes=[pltpu.VMEM(((nsub + 1) * hb, LANES), F32)] * 2,
        compiler_params=_cparams("parallel"),
        name=f"attn_prompt_g{group}",
    )(qkv, qkv, qkv, qkv, qkv)


def _attn_sample_kernel(qkv_ref, c0_ref, c1_ref, c2_ref, o_ref, lse_ref, *, ts):
    qkv = qkv_ref[0]
    rows = qkv.shape[0]
    pad = jnp.zeros((LANES - rows, LANES), F32)
    for g, (window, d) in enumerate(A_GROUPS):
        base = 3 * A_WIDTH * g
        q = qkv[:, base:base + LANES]
        k_new = jnp.concatenate([qkv[:, base + LANES:base + 2 * LANES], pad], axis=0)
        v_new = jnp.concatenate([qkv[:, base + 2 * LANES:base + 3 * LANES], pad], axis=0)
        c_ref = (c0_ref, c1_ref, c2_ref)[g]
        k_t = c_ref[0].reshape(A_WIDTH, window).astype(BF16)
        v_t = c_ref[1].reshape(A_WIDTH, window).astype(BF16)
        qi = lax.broadcasted_iota(jnp.int32, (rows, window), 0)
        c = lax.broadcasted_iota(jnp.int32, (rows, window), 1)
        qn = lax.broadcasted_iota(jnp.int32, (rows, LANES), 0)
        cn = lax.broadcasted_iota(jnp.int32, (rows, LANES), 1)
        if d == 1:
            valid_c, valid_n = c >= qi, cn <= qn
        else:
            valid_c, valid_n = (c % d) == qi, cn == qn
        lane_head = lax.broadcasted_iota(jnp.int32, q.shape, 1) // A_HEAD_DIM
        scale = 1.0 / math.sqrt(A_HEAD_DIM)
        q_heads = jnp.concatenate([jnp.where(lane_head == h, q, 0.0) for h in range(A_HEADS)],
                                  axis=0).astype(BF16)
        sc_all = _dg(q_heads, k_t, _NN) * scale
        sn_all = _dg(q_heads, k_new.astype(BF16), _NT) * scale
        pcs, pns, ls, lses = [], [], [], []
        for h in range(A_HEADS):
            s_c = jnp.where(valid_c, sc_all[h * rows:(h + 1) * rows], NEG_BIG)
            s_n = jnp.where(valid_n, sn_all[h * rows:(h + 1) * rows], NEG_BIG)
            mx = jnp.maximum(jnp.max(s_c, axis=-1, keepdims=True), jnp.max(s_n, axis=-1, keepdims=True))
            pcs.append(jnp.exp(s_c - mx))
            pns.append(jnp.exp(s_n - mx))
            ls.append(jnp.sum(pcs[-1], axis=-1, keepdims=True) + jnp.sum(pns[-1], axis=-1, keepdims=True))
            lses.append(mx + jnp.log(ls[-1]))
        o_heads = (_dg(jnp.concatenate(pcs, axis=0).astype(BF16), v_t, _NT)
                   + _dg(jnp.concatenate(pns, axis=0).astype(BF16), v_new.astype(BF16), _NN))
        o_all = jnp.zeros(q.shape, F32)
        lse_all = jnp.zeros(q.shape, F32)
        for h in range(A_HEADS):
            hm = lane_head == h
            o_all = jnp.where(hm, o_heads[h * rows:(h + 1) * rows] / ls[h], o_all)
            lse_all = jnp.where(hm, lses[h], lse_all)
        o_ref[0, :, g * LANES:(g + 1) * LANES] = o_all
        lse_ref[0, :, g * LANES:(g + 1) * LANES] = lse_all


def _attn_sample(qkv_bm, caches_t, layer, ts):
    bsz, rows, _ = qkv_bm.shape
    specs = [pl.BlockSpec((1, rows, IN_A), lambda b: (b, 0, 0))]
    for (window, d), c in zip(A_GROUPS, caches_t):
        assert c.shape[-1] == window and window % d == 0 and (d == 1 or d >= ts)
        specs.append(pl.BlockSpec((None, None, 2, A_HEADS, A_HEAD_DIM, window),
                                  lambda b: (layer, b, 0, 0, 0, 0)))
    out = jax.ShapeDtypeStruct((bsz, rows, N_A_GROUPS * LANES), F32)
    return pl.pallas_call(
        functools.partial(_attn_sample_kernel, ts=ts),
        grid=(bsz,),
        in_specs=specs,
        out_specs=[pl.BlockSpec((1, rows, N_A_GROUPS * LANES), lambda b: (b, 0, 0))] * 2,
        out_shape=[out, out],
        compiler_params=_cparams("parallel"),
        name="attn_sample",
    )(qkv_bm, *caches_t)


def _conv_b_body(ext_ref, w_ref, b_ref, g_ref, beta_ref, o_ref, *, tm, nb, base, shift_ref=None):
    acc = jnp.zeros((tm, B_WIDTH), F32) + b_ref[...]
    if shift_ref is not None:
        n = shift_ref.shape[1]
        for s in range(1, SUBLANES):
            shift_ref[s] = ext_ref[s:s + n, :]
    for j in range(B_CONV):
        off = base + j * nb
        if shift_ref is None or off % SUBLANES == 0:
            tap = ext_ref[pl.ds(off, tm), :]
        else:
            tap = shift_ref[off % SUBLANES, pl.ds(off - off % SUBLANES, tm), :]
        acc = acc + w_ref[j:j + 1, :] * tap
    mu = jnp.mean(acc, axis=-1, keepdims=True)
    cen = acc - mu
    var = jnp.mean(cen * cen, axis=-1, keepdims=True)
    y = cen * lax.rsqrt(var + LN_EPS) * g_ref[...] + beta_ref[...]
    o_ref[...] = y * _sigmoid(y)


def _rwkv_feature_rows(pc, prev, mu_ref, w2_ref, a2_ref, g2_ref, w0_ref, a0_ref, kkw_ref, ka_ref, rk_ref, seg_ref,
                       r_o, lw_o, k_o, v_o, kk_o, b_o, g_o, bonus_o):
    xs = pc + mu_ref[...] * (prev - pc)
    r = xs[:, 0:C_WIDTH]
    k = xs[:, C_WIDTH:2 * C_WIDTH]
    v = xs[:, 2 * C_WIDTH:3 * C_WIDTH]
    wa = xs[:, 3 * C_WIDTH:3 * C_WIDTH + LANES]
    gl = xs[:, 3 * C_WIDTH + LANES:]
    seg = seg_ref[...]
    lw = -DECAY_SCALE * _sigmoid(w0_ref[...] + _mm(jnp.tanh(wa), w2_ref[...]))
    a = _sigmoid(a0_ref[...] + _mm(wa, a2_ref[...]))
    g = _mm(_sigmoid(gl), g2_ref[...])
    kk = k * kkw_ref[...]
    ss = _mm_exact_rhs(kk * kk, seg, 2)
    kk = kk * lax.rsqrt(jnp.maximum(ss, 1e-24))
    k2 = k * (1.0 + (a - 1.0) * ka_ref[...])
    r_o[...] = r
    lw_o[...] = lw
    k_o[...] = k2
    v_o[...] = v
    kk_o[...] = kk
    b_o[...] = kk * a
    g_o[...] = g
    bonus_o[...] = _mm_exact_rhs(r * k2 * rk_ref[...], seg, 2) * v


RWKV_PASSES = 1
RWKV_GROUP = 8


_BNN = (((2,), (1,)), ((0,), (0,)))
_BNT = (((2,), (2,)), ((0,), (0,)))
_BTN = (((1,), (1,)), ((0,), (0,)))


def _pair_diag(x):
    first = lax.broadcasted_iota(jnp.int32, x.shape[1:], 1) < x.shape[-1] // 2
    return jnp.concatenate([jnp.where(first, x, 0.0), jnp.where(first, 0.0, x)], axis=1)


def _unit_lower_inverse(a_strict):
    g, c, _ = a_strict.shape
    ri = lax.broadcasted_iota(jnp.int32, (c, 2 * c), 0)
    ci = lax.broadcasted_iota(jnp.int32, (c, 2 * c), 1) % c
    inv = jnp.where(ri == ci, 1.0, 0.0) - jnp.where(((ri // 2) == (ci // 2)) & (ri > ci), a_strict, 0.0)
    s = 2
    while s < c:
        same = (ri // (2 * s)) == (ci // (2 * s))
        off = jnp.where(same & ((ri % (2 * s)) >= s) & ((ci % (2 * s)) < s), a_strict, 0.0)
        step = _mm(off, _pair_diag(inv), _BNN, RWKV_PASSES)
        inv = inv - _mm(inv, _pair_diag(step), _BNN, RWKV_PASSES)
        s *= 2
    return inv


def _rwkv_chunk_kernel(r_ref, lw_ref, k_ref, v_ref, kk_ref, b_ref, g_ref, bonus_ref, s0_ref, gng_ref, gnb_ref,
                       o_ref, s_out_ref, state_ref, *, c, nc):
    assert 2 * c == LANES and 2 * C_HEAD_DIM == LANES
    bb = r_ref.shape[0]
    pairs = C_HEADS // 2
    n = bb * pairs
    ci = pl.program_id(1)

    @pl.when(ci == 0)
    def _():
        for b in range(bb):
            for p in range(pairs):
                state_ref[b * pairs + p] = jnp.concatenate([s0_ref[b, 2 * p], s0_ref[b, 2 * p + 1]], axis=-1)

    ri = lax.broadcasted_iota(jnp.int32, (c, LANES), 0)
    cj = lax.broadcasted_iota(jnp.int32, (c, LANES), 1) % c
    incl = ri >= cj
    strict = ri > cj
    tri = jnp.where(lax.broadcasted_iota(jnp.int32, (c, c), 0) >= lax.broadcasted_iota(jnp.int32, (c, c), 1),
                    1.0, 0.0).astype(BF16)

    groups = []
    for ch in range(nc):
        rows = slice(ch * c, (ch + 1) * c)
        for b in range(bb):
            lw = lw_ref[b, rows, :]
            cum = _mm_exact_rhs_left(tri, lw)
            cend = cum[c - 1:c, :]
            w_inv = jnp.exp(-cum)
            w_end = jnp.exp(cend - cum)
            kvec = k_ref[b, rows, :]
            bvec = b_ref[b, rows, :]
            groups.append((kk_ref[b, rows, :] * jnp.exp(cum - lw), r_ref[b, rows, :] * jnp.exp(cum),
                           bvec * w_inv, kvec * w_inv, bvec * w_end, kvec * w_end, v_ref[b, rows, :],
                           jnp.exp(cend), bonus_ref[b, rows, :], g_ref[b, rows, :]))

    def paired(idx):
        return jnp.stack([grp[idx][:, p * LANES:(p + 1) * LANES] for grp in groups for p in range(pairs)])

    P = RWKV_PASSES
    kap, rho, bt, kt, bend, kend, v, wc = [paired(i) for i in range(8)]
    lhs2 = jnp.concatenate([kap, rho], axis=1)
    xb = _mm(lhs2, _pair_diag(bt), _BNT, P)
    xk = _mm(lhs2, _pair_diag(kt), _BNT, P)
    a_kb = jnp.where(strict, xb[:, :c], 0.0)
    a_rb = jnp.where(incl, xb[:, c:], 0.0)
    a_kk = jnp.where(strict, xk[:, :c], 0.0)
    a_rk = jnp.where(incl, xk[:, c:], 0.0)
    t_inv = _unit_lower_inverse(a_kb)
    v_d = _pair_diag(v)
    kap_p = _mm(t_inv, _pair_diag(kap), _BNN, P)
    v_p = _mm(t_inv, _pair_diag(_mm(a_kk, v_d, _BNN, P)), _BNN, P)
    kap_pd = _pair_diag(kap_p)
    er = lax.broadcasted_iota(jnp.int32, (LANES, LANES), 0)
    ec = lax.broadcasted_iota(jnp.int32, (LANES, LANES), 1)
    same_head = (er // C_HEAD_DIM) == (ec // C_HEAD_DIM)
    first = lax.broadcasted_iota(jnp.int32, (C_HEAD_DIM, LANES), 1) < C_HEAD_DIM
    pm = jnp.where(er == ec, wc, 0.0) - jnp.where(same_head, _mm(kap_p, bend, _BTN, P), 0.0)
    fq = _mm(v, kend, _BTN, P) - _mm(v_p, bend, _BTN, P)
    q = jnp.where(first, fq[:, :C_HEAD_DIM], fq[:, C_HEAD_DIM:])
    rp = rho - _mm(a_rb, kap_pd, _BNN, P)
    y0 = _mm(a_rk, v_d, _BNN, P) - _mm(a_rb, _pair_diag(v_p), _BNN, P)

    s = state_ref[...]
    ys = []
    for ch in range(nc):
        sl = slice(ch * n, (ch + 1) * n)
        ys.append(_mm(rp[sl], _pair_diag(s), _BNT, P) + y0[sl])
        s = _mm(s, pm[sl], _BNN, P) + q[sl]
    state_ref[...] = s
    y = jnp.concatenate(ys, axis=0) if nc > 1 else ys[0]

    g_all = y.shape[0]
    seg = jnp.where(same_head, 1.0, 0.0).astype(BF16)
    head_mean = lambda z: (_mm_exact_rhs(z.reshape(g_all * c, LANES), seg, 2) * (1.0 / C_HEAD_DIM)
                           ).reshape(g_all, c, LANES)
    cen = y - head_mean(y)
    out = cen * lax.rsqrt(head_mean(cen * cen) + GN_EPS)
    bonus = paired(8)
    gate = paired(9)
    idx = 0
    for ch in range(nc):
        for b in range(bb):
            for p in range(pairs):
                sl = slice(p * LANES, (p + 1) * LANES)
                yn = out[idx] * gng_ref[:, sl] + gnb_ref[:, sl]
                o_ref[b, ch * c:(ch + 1) * c, sl] = (yn + bonus[idx]) * gate[idx]
                idx += 1

    @pl.when(ci == pl.num_programs(1) - 1)
    def _():
        for b in range(bb):
            for p in range(pairs):
                s_out_ref[b, 2 * p] = s[b * pairs + p][:, :C_HEAD_DIM]
                s_out_ref[b, 2 * p + 1] = s[b * pairs + p][:, C_HEAD_DIM:]


def _rwkv_chunks(feats, s0, gn_g, gn_b):
    bsz, t, _ = feats[0].shape
    c = RWKV_CHUNK
    pairs = C_HEADS // 2
    bb = RWKV_GROUP if bsz % RWKV_GROUP == 0 else 1
    nc = max(1, min(RWKV_GROUP // bb, t // c))
    seq = pl.BlockSpec((bb, nc * c, C_WIDTH), lambda b, i: (b, i, 0))
    st = pl.BlockSpec((bb, C_HEADS, C_HEAD_DIM, C_HEAD_DIM), lambda b, i: (b, 0, 0, 0))
    vec = pl.BlockSpec((1, C_WIDTH), lambda b, i: (0, 0))
    return pl.pallas_call(
        functools.partial(_rwkv_chunk_kernel, c=c, nc=nc),
        grid=(bsz // bb, t // (nc * c)),
        in_specs=[seq] * 8 + [st, vec, vec],
        out_specs=[seq, st],
        out_shape=[jax.ShapeDtypeStruct((bsz, t, C_WIDTH), F32),
                   jax.ShapeDtypeStruct((bsz, C_HEADS, C_HEAD_DIM, C_HEAD_DIM), F32)],
        scratch_shapes=[pltpu.VMEM((bb * pairs, C_HEAD_DIM, LANES), F32)],
        compiler_params=_cparams("parallel", "arbitrary"),
        name="rwkv_chunks",
    )(*feats, s0, gn_g, gn_b)


N_MERGE_REFS = 15


def _merge_rows(o0, l0, o1, l1, o2, l2, ob_ref, oc_ref, gate_ref, x_ref, wa_ref, wb_ref, wc_ref, wo_ref,
                gain_ref):
    ls = [l0[...], l1[...], l2[...]]
    mx = jnp.maximum(jnp.maximum(ls[0], ls[1]), ls[2])
    es = [jnp.exp(l - mx) for l in ls]
    den = es[0] + es[1] + es[2]
    o_a = (es[0] * o0[...] + es[1] * o1[...] + es[2] * o2[...]) / den
    merged = (gate_ref[:, 0:D_MODEL] * _mm(o_a, wa_ref[...])
              + gate_ref[:, D_MODEL:2 * D_MODEL] * _mm(ob_ref[...], wb_ref[...])
              + gate_ref[:, 2 * D_MODEL:] * _mm(oc_ref[...], wc_ref[...]))
    z = _mm(merged, wo_ref[...])
    return x_ref[...] + _rms(z, gain_ref[...])


def _gelu_tanh(x):
    return 0.5 * x * (1.0 + jnp.tanh(math.sqrt(2.0 / math.pi) * (x + 0.044715 * (x * x * x))))


FFN_PIECE = 128


def _ffn_kernel(*refs, tm, nb, cr):
    g1_ref, fu_ref, w_ref, b_ref, c_ref, fd_ref, g2_ref, out_ref, t_ref, ext_ref, f_ref = refs[N_MERGE_REFS:]

    @pl.when(pl.program_id(0) == 0)
    def _():
        ext_ref[0:cr, :] = c_ref[...]

    x1 = _merge_rows(*refs[:N_MERGE_REFS])
    h = _rms(x1, g1_ref[...]).astype(BF16)
    n_chunks = D_FF // FF_CHUNK
    split = (n_chunks + 1) // 2 * FF_CHUNK
    z = None
    for jc in range(n_chunks):
        pair = (slice(jc * FF_CHUNK, (jc + 1) * FF_CHUNK),
                slice(D_FF + jc * FF_CHUNK, D_FF + (jc + 1) * FF_CHUNK))
        for cols in pair:
            ext_ref[cr:cr + tm, cols] = _dg(h, fu_ref[:, cols], _NN)
        for r0 in range(0, tm, FFN_PIECE):
            cus = []
            for cols in pair:
                cus.append(w_ref[0:1, cols] * ext_ref[r0 + cr - 2 * nb:r0 + cr - 2 * nb + FFN_PIECE, cols]
                           + w_ref[1:2, cols] * ext_ref[r0 + cr - nb:r0 + cr - nb + FFN_PIECE, cols]
                           + w_ref[2:3, cols] * ext_ref[r0 + cr:r0 + cr + FFN_PIECE, cols] + b_ref[:, cols])
            f_ref[r0:r0 + FFN_PIECE, jc * FF_CHUNK:(jc + 1) * FF_CHUNK] = (
                _gelu_tanh(cus[0]) * cus[1]).astype(BF16)
        if (jc + 1) * FF_CHUNK == split:
            z = _dg(f_ref[:, 0:split], fd_ref[0:split, :], _NN)
    z = z + _dg(f_ref[:, split:], fd_ref[split:, :], _NN)
    tail = ext_ref[tm:tm + cr, :]
    t_ref[...] = tail
    ext_ref[0:cr, :] = tail
    out_ref[...] = x1 + _rms(z, g2_ref[...])


def _merge_ffn(att, o_b, o_c, gates, x, ctx, p, nb):
    m = x.shape[0]
    tm = min(m, 256)
    cr = ctx.shape[0]
    row = lambda i: (i, 0)
    fixed = lambda i: (0, 0)
    rows = lambda w: pl.BlockSpec((tm, w), row)
    once = lambda shape: pl.BlockSpec(shape, fixed, pipeline_mode=pl.Buffered(1))
    resident = lambda name: _layer_weight(p[name], p['layer'], pipeline_mode=pl.Buffered(1))
    merge_ws = ('w_br_a', 'w_br_b', 'w_br_c', 'w_out')
    x2, tail = pl.pallas_call(
        functools.partial(_ffn_kernel, tm=tm, nb=nb, cr=cr),
        grid=(m // tm,),
        in_specs=[rows(LANES)] * 6 + [rows(B_WIDTH), rows(C_WIDTH), rows(IN_GATE), rows(D_MODEL)]
                 + [resident(name) for name in merge_ws] + [once((1, D_MODEL))]
                 + [once((1, D_MODEL)),
                  _layer_weight(p['f_up'], p['layer'], pipeline_mode=pl.Buffered(1)),
                  once((F_CONV, 2 * D_FF)), once((1, 2 * D_FF)), once((cr, 2 * D_FF)),
                  _layer_weight(p['f_down'], p['layer'], pipeline_mode=pl.Buffered(1)), once((1, D_MODEL))],
        out_specs=[pl.BlockSpec((tm, D_MODEL), row), pl.BlockSpec((cr, 2 * D_FF), row)],
        out_shape=[jax.ShapeDtypeStruct((m, D_MODEL), F32),
                   jax.ShapeDtypeStruct(((m // tm) * cr, 2 * D_FF), F32)],
        scratch_shapes=[pltpu.VMEM((cr + tm, 2 * D_FF), F32), pltpu.VMEM((tm, D_FF), BF16)],
        compiler_params=_cparams("arbitrary"),
        name="merge_ffn",
    )(*att, o_b, o_c, gates, x, *[p[name] for name in merge_ws], p['norm_mix_post'],
      p['norm_ffn_pre'], p['f_up'], p['f_dw_w'], p['f_dw_b'], ctx, p['f_down'], p['norm_ffn_post'])
    return x2, tail[-cr:]


def _to_time_major(a):
    a = jnp.swapaxes(a, 0, 1)
    return a.reshape((a.shape[0] * a.shape[1],) + a.shape[2:])


def _to_batch_major(a, ts):
    return jnp.swapaxes(a.reshape(ts, a.shape[0] // ts, a.shape[1]), 0, 1)


def _layer_prompt(x, tables, p):
    t = x.shape[0]
    qkv, glu, pc, gates, o_b, feats = _in_proj(x, tables, jnp.zeros((4 * SUBLANES, B_WIDTH), F32),
                                               jnp.zeros((SUBLANES, IN_C), F32), p, 1)
    att = []
    for g in range(N_A_GROUPS):
        att.extend(_attn_prompt(qkv, g))
    s0 = jnp.zeros((1, C_HEADS, C_HEAD_DIM, C_HEAD_DIM), F32)
    o_c, s_new = _rwkv_chunks([f[None] for f in feats], s0, p['c_gn_g'], p['c_gn_b'])
    x2, tail = _merge_ffn(att, o_b, o_c[0], gates, x, jnp.zeros((SUBLANES, 2 * D_FF), F32), p, 1)
    kv = [qkv[t - min(w, t):, 3 * A_WIDTH * g + A_WIDTH:3 * A_WIDTH * (g + 1)]
          .reshape(1, min(w, t), 2, A_HEADS, A_HEAD_DIM) for g, (w, _) in enumerate(A_GROUPS)]
    f_tail = tail[-(F_CONV - 1):][None]
    return x2, kv, glu[t - (B_CONV - 1):][None], pc[t - 1:], s_new, f_tail


def _layer_sample(x, tables, caches_t, layer, b_ctx, c_shift, c_state, f_ctx, p, ts):
    nb = x.shape[0] // ts
    qkv, glu, pc, gates, o_b, feats = _in_proj(x, tables, _to_time_major(b_ctx), c_shift, p, nb)
    qkv_bm = _to_batch_major(qkv, ts)
    qkv_pad = jnp.pad(qkv_bm, ((0, 0), (0, SUBLANES - ts), (0, 0)))
    o, lse = _attn_sample(qkv_pad, caches_t, layer, ts)
    att = []
    for g in range(N_A_GROUPS):
        att.append(_to_time_major(o[:, :ts, g * LANES:(g + 1) * LANES]))
        att.append(_to_time_major(lse[:, :ts, g * LANES:(g + 1) * LANES]))
    feats_bm = [jnp.pad(_to_batch_major(f, ts), ((0, 0), (0, RWKV_CHUNK - ts), (0, 0))) for f in feats]
    o_c, s_new = _rwkv_chunks(feats_bm, c_state, p['c_gn_g'], p['c_gn_b'])
    x2, tail = _merge_ffn(att, o_b, _to_time_major(o_c[:, :ts]), gates, x, _to_time_major(f_ctx), p, nb)
    kv = [_to_batch_major(qkv[:, 3 * A_WIDTH * g + A_WIDTH:3 * A_WIDTH * (g + 1)], ts)
          .reshape(nb, ts, 2, A_HEADS, A_HEAD_DIM) for g in range(N_A_GROUPS)]
    b_new = jnp.concatenate([b_ctx, _to_batch_major(glu, ts)], axis=1)[:, -(B_CONV - 1):]
    f_new = jnp.concatenate([f_ctx, _to_batch_major(tail, F_CONV - 1)], axis=1)
    return x2, kv, b_new, pc[(ts - 1) * nb:], s_new, f_new[:, -(F_CONV - 1):]


def kernel(x_prompt, x_sample, cache_a_kv0, cache_a_kv1, cache_a_kv2, state_b_conv, state_c_shift, state_c_wkv, state_f_conv, norm_mix_pre, norm_mix_post, norm_ffn_pre, norm_ffn_post, w_in, b_dw_w, b_dw_b, b_ln_g, b_ln_b, c_mu, c_w0, c_w2, c_a0, c_a2, c_g2, c_kk, c_ka, c_rk, c_gn_g, c_gn_b, w_br_a, w_br_b, w_br_c, w_out, f_up, f_dw_w, f_dw_b, f_down):
    depth = w_in.shape[0]
    bp, tp, _ = x_prompt.shape
    bs, ts, _ = x_sample.shape
    assert bp == 1 and ts <= SUBLANES
    caches_t = [jnp.transpose(c, (0, 1, 3, 4, 5, 2)) for c in (cache_a_kv0, cache_a_kv1, cache_a_kv2)]

    head_id = jnp.arange(C_WIDTH) // C_HEAD_DIM
    seg = (head_id[:, None] == head_id[None, :]).astype(BF16)
    zpad = jnp.zeros((LANES - C_DECAY_RANK, C_WIDTH), F32)

    tab_p = _rope_tables(tp, 0, 1)
    tab_s = _rope_tables(ts * bs, PAST_LEN, bs)

    stacked = {name: w.astype(BF16) for name, w in (
        ('w_in', w_in), ('w_br_a', w_br_a), ('w_br_b', w_br_b), ('w_br_c', w_br_c), ('w_out', w_out),
        ('f_up', f_up), ('f_down', f_down))}

    y_p = x_prompt[0]
    y_s = _to_time_major(x_sample)
    outs_p, outs_s = [], []
    for l in range(depth):
        vec = lambda a: a[l][None, :]
        p = {
            **stacked, 'layer': l,
            'norm_mix_pre': vec(norm_mix_pre), 'norm_mix_post': vec(norm_mix_post),
            'norm_ffn_pre': vec(norm_ffn_pre), 'norm_ffn_post': vec(norm_ffn_post),
            'b_dw_w': b_dw_w[l], 'b_dw_b': vec(b_dw_b), 'b_ln_g': vec(b_ln_g), 'b_ln_b': vec(b_ln_b),
            'c_mu': vec(c_mu), 'c_w0': vec(c_w0), 'c_a0': vec(c_a0),
            'c_w2p': jnp.concatenate([c_w2[l], zpad], axis=0),
            'c_a2p': jnp.concatenate([zpad, c_a2[l]], axis=0),
            'c_g2': c_g2[l], 'c_kk': vec(c_kk), 'c_ka': vec(c_ka),
            'c_rk': c_rk[l].reshape(1, C_WIDTH), 'c_gn_g': vec(c_gn_g), 'c_gn_b': vec(c_gn_b),
            'seg': seg,
            'f_dw_w': f_dw_w[l], 'f_dw_b': vec(f_dw_b),
        }
        y_p, *rest_p = _layer_prompt(y_p, tab_p, p)
        outs_p.append(rest_p)
        y_s, *rest_s = _layer_sample(y_s, tab_s, caches_t, l, state_b_conv[l], state_c_shift[l],
                                     state_c_wkv[l], state_f_conv[l], p, ts)
        outs_s.append(rest_s)

    stack = lambda outs, f: jnp.stack([f(o) for o in outs])
    res = [y_p[None], _to_batch_major(y_s, ts)]
    res += [stack(outs_p, lambda o, g=g: o[0][g]) for g in range(N_A_GROUPS)]
    res += [stack(outs_s, lambda o, g=g: o[0][g]) for g in range(N_A_GROUPS)]
    for idx in range(1, 5):
        res += [stack(outs_p, lambda o: o[idx]), stack(outs_s, lambda o: o[idx])]
    return tuple(res)
```

```python
import functools
import math

import jax
import jax.numpy as jnp
from jax import lax
from jax.experimental import pallas as pl
from jax.experimental.pallas import tpu as pltpu

F32 = jnp.float32
BF16 = jnp.bfloat16

D_MODEL = 1024
PAST_LEN = 16384
A_GROUPS = ((128, 1), (512, 4), (2048, 16))
N_A_GROUPS = len(A_GROUPS)
A_HEAD_DIM = 32
A_HEADS = 4
A_WIDTH = A_HEADS * A_HEAD_DIM
ROPE_THETA = 10000.0
ATT_BLOCK = 128
B_WIDTH = 256
B_CONV = 31
C_HEAD_DIM = 64
C_WIDTH = 384
C_HEADS = 6
C_DECAY_RANK = 64
C_ICLR_RANK = 64
C_GATE_RANK = 128
DECAY_SCALE = math.exp(-0.5)
D_FF = 2816
F_CONV = 3
IN_A = N_A_GROUPS * 3 * A_WIDTH
IN_B = 2 * B_WIDTH
IN_C = 3 * C_WIDTH + C_DECAY_RANK + C_ICLR_RANK + C_GATE_RANK
IN_GATE = 3 * D_MODEL
IN_WIDTH = IN_A + IN_B + IN_C + IN_GATE
RMS_EPS = 1e-6
LN_EPS = 1e-5
GN_EPS = C_HEAD_DIM * 1e-5
NEG_BIG = -1e30

LANES = 128
SUBLANES = 8
VMEM_LIMIT = 56 * 1024 * 1024

RWKV_CHUNK = 64
FF_CHUNK = 256


def _cparams(*sem):
    return pltpu.CompilerParams(dimension_semantics=sem, vmem_limit_bytes=VMEM_LIMIT)


def _split_bf16(x, n):
    pieces = []
    rem = x
    for _ in range(n):
        p = rem.astype(BF16)
        pieces.append(p)
        rem = rem - p.astype(F32)
    return pieces


_NN = (((1,), (0,)), ((), ()))
_NT = (((1,), (1,)), ((), ()))


def _dg(a, b, dims):
    return lax.dot_general(a, b, dims, preferred_element_type=F32)


def _mm(a, b, dims=_NN, passes=1):
    if passes == 1:
        return _dg(a.astype(BF16), b.astype(BF16), dims)
    a_hi, a_lo = _split_bf16(a, 2)
    b_hi, b_lo = _split_bf16(b, 2)
    return _dg(a_hi, b_hi, dims) + (_dg(a_hi, b_lo, dims) + _dg(a_lo, b_hi, dims))


def _mm_exact_rhs(a, b_bf16, pieces=3):
    out = None
    for p in _split_bf16(a, pieces):
        t = _dg(p, b_bf16, _NN)
        out = t if out is None else out + t
    return out


def _mm_exact_rhs_left(a_bf16, b, pieces=3):
    out = None
    for p in _split_bf16(b, pieces):
        t = _dg(a_bf16, p, _NN)
        out = t if out is None else out + t
    return out


def _rms(x, g):
    ms = jnp.mean(x * x, axis=-1, keepdims=True)
    return x * lax.rsqrt(ms + RMS_EPS) * g


def _sigmoid(x):
    return 1.0 / (1.0 + jnp.exp(-x))


def _rope_table_kernel(inv_ref, cos_ref, sin_ref, cl_ref, sl_ref, *, tm, pos0, nb):
    inv = inv_ref[...]

    @pl.when(pl.program_id(0) == 0)
    def _():
        within = (lax.broadcasted_iota(jnp.int32, (tm, LANES), 0) // nb).astype(F32) * inv
        cl_ref[...] = jnp.cos(within)
        sl_ref[...] = jnp.sin(within)

    base = (pos0 + pl.program_id(0) * (tm // nb)).astype(F32) * jnp.broadcast_to(inv, (SUBLANES, LANES))
    ch = jnp.cos(base)[0:1, :]
    sh = jnp.sin(base)[0:1, :]
    lane = lax.broadcasted_iota(jnp.int32, (tm, LANES), 1)
    sign = jnp.where((lane % A_HEAD_DIM) < A_HEAD_DIM // 2, -1.0, 1.0)
    cos_ref[...] = ch * cl_ref[...] - sh * sl_ref[...]
    sin_ref[...] = (sh * cl_ref[...] + ch * sl_ref[...]) * sign


def _rope_tables(m, pos0, nb):
    half = A_HEAD_DIM // 2
    inv = ROPE_THETA ** (-(jnp.arange(half, dtype=F32) * 2.0 / A_HEAD_DIM))
    inv_lane = jnp.tile(inv, 2 * A_HEADS)[None, :]
    tm = min(m, 512)
    assert tm % nb == 0
    return pl.pallas_call(
        functools.partial(_rope_table_kernel, tm=tm, pos0=pos0, nb=nb),
        grid=(m // tm,),
        in_specs=[pl.BlockSpec((1, LANES), lambda i: (0, 0))],
        out_specs=[pl.BlockSpec((tm, LANES), lambda i: (i, 0))] * 2,
        out_shape=[jax.ShapeDtypeStruct((m, LANES), F32)] * 2,
        scratch_shapes=[pltpu.VMEM((tm, LANES), F32)] * 2,
        compiler_params=_cparams("arbitrary"),
        name="rope_tables",
    )(inv_lane)


N_C_PARAMS = 10


def _in_proj_kernel(x_ref, g_ref, w_ref, cos_ref, sin_ref, bctx_ref, bw_ref, bb_ref, bg_ref, bbeta_ref, cctx_ref,
                    *rest, tm, nb, bctx, coff, carry):
    c_params = rest[:N_C_PARAMS]
    qkv_ref, glu_ref, pc_ref, gate_ref, ob_ref = rest[N_C_PARAMS:N_C_PARAMS + 5]
    c_outs = rest[N_C_PARAMS + 5:N_C_PARAMS + 13]
    extb_ref, extc_ref = rest[N_C_PARAMS + 13:N_C_PARAMS + 15]
    shift_ref = rest[N_C_PARAMS + 15] if len(rest) > N_C_PARAMS + 15 else None

    @pl.when(pl.program_id(0) == 0)
    def _():
        extb_ref[0:bctx, :] = bctx_ref[...]
        extc_ref[0:coff, :] = cctx_ref[...]

    h = _rms(x_ref[...], g_ref[...]).astype(BF16)
    cos = cos_ref[...]
    sin = sin_ref[...]
    lane = lax.broadcasted_iota(jnp.int32, cos.shape, 1)
    first_half = (lane % A_HEAD_DIM) < A_HEAD_DIM // 2
    pa = _dg(h, w_ref[:, 0:IN_A], _NN)
    for blk in range(IN_A // LANES):
        y = pa[:, blk * LANES:(blk + 1) * LANES]
        if blk % 3 != 2:
            swapped = jnp.where(first_half, pltpu.roll(y, LANES - A_HEAD_DIM // 2, 1),
                                pltpu.roll(y, A_HEAD_DIM // 2, 1))
            y = y * cos + swapped * sin
        qkv_ref[:, blk * LANES:(blk + 1) * LANES] = y
    pb = _dg(h, w_ref[:, IN_A:IN_A + IN_B], _NN)
    glu = pb[:, :B_WIDTH] * _sigmoid(pb[:, B_WIDTH:])
    glu_ref[...] = glu
    extb_ref[bctx:bctx + tm, :] = glu
    _conv_b_body(extb_ref, bw_ref, bb_ref, bg_ref, bbeta_ref, ob_ref, tm=tm, nb=nb,
                 base=bctx - (B_CONV - 1) * nb, shift_ref=shift_ref)
    pc = _dg(h, w_ref[:, IN_A + IN_B:IN_A + IN_B + IN_C], _NN)
    pc_ref[...] = pc
    extc_ref[coff:coff + tm, :] = pc
    _rwkv_feature_rows(pc, extc_ref[coff - nb:coff - nb + tm, :], *c_params, *c_outs)
    gate_ref[...] = _sigmoid(_dg(h, w_ref[:, IN_A + IN_B + IN_C:], _NN)).astype(gate_ref.dtype)
    if carry:
        extb_ref[0:bctx, :] = extb_ref[tm:tm + bctx, :]
        extc_ref[0:coff, :] = extc_ref[tm:tm + coff, :]


def _layer_weight(w, layer, **kw):
    return pl.BlockSpec((None,) + w.shape[1:], lambda *_: (layer, 0, 0), **kw)


def _in_proj(x, tables, b_ctx, c_ctx, p, nb):
    m = x.shape[0]
    tm = min(m, 256)
    bctx, coff = b_ctx.shape[0], c_ctx.shape[0]
    row = lambda i: (i, 0)
    fixed = lambda i: (0, 0)
    full = lambda a: pl.BlockSpec(a.shape, fixed)
    widths = (IN_A, B_WIDTH, IN_C, IN_GATE, B_WIDTH) + (C_WIDTH,) * 8
    b_params = (b_ctx, p['b_dw_w'], p['b_dw_b'], p['b_ln_g'], p['b_ln_b'])
    c_params = (c_ctx, p['c_mu'], p['c_w2p'], p['c_a2p'], p['c_g2'], p['c_w0'], p['c_a0'], p['c_kk'], p['c_ka'],
                p['c_rk'], p['seg'])
    scratch = [pltpu.VMEM((bctx + tm, B_WIDTH), F32), pltpu.VMEM((coff + tm, IN_C), F32)]
    if nb == 1:
        scratch.append(pltpu.VMEM((SUBLANES, bctx + tm - SUBLANES, B_WIDTH), F32))
    outs = pl.pallas_call(
        functools.partial(_in_proj_kernel, tm=tm, nb=nb, bctx=bctx, coff=coff, carry=m > tm),
        grid=(m // tm,),
        in_specs=[pl.BlockSpec((tm, D_MODEL), row), pl.BlockSpec((1, D_MODEL), fixed),
                  _layer_weight(p['w_in'], p['layer'], pipeline_mode=pl.Buffered(1)),
                  pl.BlockSpec((tm, LANES), row), pl.BlockSpec((tm, LANES), row)]
                 + [full(a) for a in b_params + c_params],
        out_specs=[pl.BlockSpec((tm, w), row) for w in widths],
        out_shape=[jax.ShapeDtypeStruct((m, w), BF16 if w == IN_GATE else F32) for w in widths],
        scratch_shapes=scratch,
        compiler_params=_cparams("arbitrary"),
        name="in_proj",
    )(x, p['norm_mix_pre'], p['w_in'], *tables, *b_params, *c_params)
    return outs[0], outs[1], outs[2], outs[3], outs[4], outs[5:]


def _softmax_heads(q, k_bf16, v_bf16, valid):
    rows = q.shape[0]
    lane_head = lax.broadcasted_iota(jnp.int32, q.shape, 1) // A_HEAD_DIM
    scale = 1.0 / math.sqrt(A_HEAD_DIM)
    q_heads = jnp.concatenate([jnp.where(lane_head == h, q, 0.0) for h in range(A_HEADS)], axis=0)
    s_all = _dg(q_heads.astype(BF16), k_bf16, _NT) * scale
    ps, ls, lses = [], [], []
    for h in range(A_HEADS):
        s = jnp.where(valid, s_all[h * rows:(h + 1) * rows], NEG_BIG)
        mx = jnp.max(s, axis=-1, keepdims=True)
        p = jnp.exp(s - mx)
        l = jnp.sum(p, axis=-1, keepdims=True)
        ps.append(p.astype(BF16))
        ls.append(l)
        lses.append(mx + jnp.log(l))
    o_heads = _dg(jnp.concatenate(ps, axis=0), v_bf16, _NN)
    o_all = jnp.zeros(q.shape, F32)
    lse_all = jnp.zeros(q.shape, F32)
    for h in range(A_HEADS):
        hm = lane_head == h
        o_all = jnp.where(hm, o_heads[h * rows:(h + 1) * rows] / ls[h], o_all)
        lse_all = jnp.where(hm, lses[h], lse_all)
    return o_all, lse_all


ATT_UNROLL = 4


def _attn_prompt_kernel(q_ref, kh_ref, kc_ref, vh_ref, vc_ref, o_ref, lse_ref, kx_ref, vx_ref, *, d, nsub):
    i = pl.program_id(0)
    hb = d * ATT_BLOCK
    kx_ref[0:hb, :] = kh_ref[...]
    kx_ref[hb:, :] = kc_ref[...]
    vx_ref[0:hb, :] = vh_ref[...]
    vx_ref[hb:, :] = vc_ref[...]
    qi = lax.broadcasted_iota(jnp.int32, (ATT_BLOCK, 2 * ATT_BLOCK), 0)
    kj = lax.broadcasted_iota(jnp.int32, (ATT_BLOCK, 2 * ATT_BLOCK), 1)
    delta = qi + ATT_BLOCK - kj
    band = (delta >= 0) & (delta <= ATT_BLOCK)
    current = kj >= ATT_BLOCK

    def problem(it, carry):
        j = it // d
        start = j * hb + it % d
        valid = band & (current | (i > 0) | (j > 0))
        keys = pl.ds(start, 2 * ATT_BLOCK, stride=d)
        rows = pl.ds(start, ATT_BLOCK, stride=d)
        o, lse = _softmax_heads(q_ref[rows, :], kx_ref[keys, :].astype(BF16), vx_ref[keys, :].astype(BF16), valid)
        o_ref[rows, :] = o
        lse_ref[rows, :] = lse
        return carry

    lax.fori_loop(0, nsub * d, problem, 0, unroll=ATT_UNROLL)


def _attn_prompt(qkv, group):
    t = qkv.shape[0]
    _, d = A_GROUPS[group]
    hb = d * ATT_BLOCK
    nsub = max(1, 8 // d)
    while t % (nsub * hb):
        nsub //= 2
    col = 3 * group
    blk = (nsub * hb, LANES)
    cur = lambda off: (lambda i: (i, col + off))
    halo = lambda off: (lambda i: (jnp.maximum(i * nsub - 1, 0), col + off))
    return pl.pallas_call(
        functools.partial(_attn_prompt_kernel, d=d, nsub=nsub),
        grid=(t // (nsub * hb),),
        in_specs=[pl.BlockSpec(blk, cur(0)), pl.BlockSpec((hb, LANES), halo(1)), pl.BlockSpec(blk, cur(1)),
                  pl.BlockSpec((hb, LANES), halo(2)), pl.BlockSpec(blk, cur(2))],
        out_specs=[pl.BlockSpec(blk, lambda i: (i, 0))] * 2,
        out_shape=[jax.ShapeDtypeStruct((t, LANES), F32)] * 2,
        scratch_shapes=[pltpu.VMEM(((nsub + 1) * hb, LANES), F32)] * 2,
        compiler_params=_cparams("parallel"),
        name=f"attn_prompt_g{group}",
    )(qkv, qkv, qkv, qkv, qkv)


def _attn_sample_kernel(qkv_ref, c0_ref, c1_ref, c2_ref, o_ref, lse_ref, *, ts):
    for bi in range(qkv_ref.shape[0]):
        _attn_sample_one(bi, qkv_ref, c0_ref, c1_ref, c2_ref, o_ref, lse_ref, ts)


def _attn_sample_one(bi, qkv_ref, c0_ref, c1_ref, c2_ref, o_ref, lse_ref, ts):
    qkv = qkv_ref[bi]
    rows = qkv.shape[0]
    pad = jnp.zeros((LANES - rows, LANES), F32)
    for g, (window, d) in enumerate(A_GROUPS):
        base = 3 * A_WIDTH * g
        q = qkv[:, base:base + LANES]
        k_new = jnp.concatenate([qkv[:, base + LANES:base + 2 * LANES], pad], axis=0)
        v_new = jnp.concatenate([qkv[:, base + 2 * LANES:base + 3 * LANES], pad], axis=0)
        c_ref = (c0_ref, c1_ref, c2_ref)[g]
        k_t = c_ref[bi, 0].reshape(A_WIDTH, window).astype(BF16)
        v_t = c_ref[bi, 1].reshape(A_WIDTH, window).astype(BF16)
        qi = lax.broadcasted_iota(jnp.int32, (rows, window), 0)
        c = lax.broadcasted_iota(jnp.int32, (rows, window), 1)
        qn = lax.broadcasted_iota(jnp.int32, (rows, LANES), 0)
        cn = lax.broadcasted_iota(jnp.int32, (rows, LANES), 1)
        if d == 1:
            valid_c, valid_n = c >= qi, cn <= qn
        else:
            valid_c, valid_n = (c % d) == qi, cn == qn
        lane_head = lax.broadcasted_iota(jnp.int32, q.shape, 1) // A_HEAD_DIM
        scale = 1.0 / math.sqrt(A_HEAD_DIM)
        q_heads = jnp.concatenate([jnp.where(lane_head == h, q, 0.0) for h in range(A_HEADS)],
                                  axis=0).astype(BF16)
        sc_all = _dg(q_heads, k_t, _NN) * scale
        sn_all = _dg(q_heads, k_new.astype(BF16), _NT) * scale
        pcs, pns, ls, lses = [], [], [], []
        for h in range(A_HEADS):
            s_c = jnp.where(valid_c, sc_all[h * rows:(h + 1) * rows], NEG_BIG)
            s_n = jnp.where(valid_n, sn_all[h * rows:(h + 1) * rows], NEG_BIG)
            mx = jnp.maximum(jnp.max(s_c, axis=-1, keepdims=True), jnp.max(s_n, axis=-1, keepdims=True))
            pcs.append(jnp.exp(s_c - mx))
            pns.append(jnp.exp(s_n - mx))
            ls.append(jnp.sum(pcs[-1], axis=-1, keepdims=True) + jnp.sum(pns[-1], axis=-1, keepdims=True))
            lses.append(mx + jnp.log(ls[-1]))
        o_heads = (_dg(jnp.concatenate(pcs, axis=0).astype(BF16), v_t, _NT)
                   + _dg(jnp.concatenate(pns, axis=0).astype(BF16), v_new.astype(BF16), _NN))
        o_all = jnp.zeros(q.shape, F32)
        lse_all = jnp.zeros(q.shape, F32)
        for h in range(A_HEADS):
            hm = lane_head == h
            o_all = jnp.where(hm, o_heads[h * rows:(h + 1) * rows] / ls[h], o_all)
            lse_all = jnp.where(hm, lses[h], lse_all)
        o_ref[bi, :, g * LANES:(g + 1) * LANES] = o_all
        lse_ref[bi, :, g * LANES:(g + 1) * LANES] = lse_all


def _attn_sample(qkv_bm, caches_t, layer, ts):
    bsz, rows, _ = qkv_bm.shape
    bb = 4 if bsz % 4 == 0 else 1
    specs = [pl.BlockSpec((bb, rows, IN_A), lambda b: (b, 0, 0))]
    for (window, d), c in zip(A_GROUPS, caches_t):
        assert c.shape[-1] == window and window % d == 0 and (d == 1 or d >= ts)
        specs.append(pl.BlockSpec((None, bb, 2, A_HEADS, A_HEAD_DIM, window),
                                  lambda b: (layer, b, 0, 0, 0, 0)))
    out = jax.ShapeDtypeStruct((bsz, rows, N_A_GROUPS * LANES), F32)
    return pl.pallas_call(
        functools.partial(_attn_sample_kernel, ts=ts),
        grid=(bsz // bb,),
        in_specs=specs,
        out_specs=[pl.BlockSpec((bb, rows, N_A_GROUPS * LANES), lambda b: (b, 0, 0))] * 2,
        out_shape=[out, out],
        compiler_params=_cparams("parallel"),
        name="attn_sample",
    )(qkv_bm, *caches_t)


def _conv_b_body(ext_ref, w_ref, b_ref, g_ref, beta_ref, o_ref, *, tm, nb, base, shift_ref=None):
    acc = jnp.zeros((tm, B_WIDTH), F32) + b_ref[...]
    if shift_ref is not None:
        n = shift_ref.shape[1]
        for s in range(1, SUBLANES):
            shift_ref[s] = ext_ref[s:s + n, :]
    for j in range(B_CONV):
        off = base + j * nb
        if shift_ref is None or off % SUBLANES == 0:
            tap = ext_ref[pl.ds(off, tm), :]
        else:
            tap = shift_ref[off % SUBLANES, pl.ds(off - off % SUBLANES, tm), :]
        acc = acc + w_ref[j:j + 1, :] * tap
    mu = jnp.mean(acc, axis=-1, keepdims=True)
    cen = acc - mu
    var = jnp.mean(cen * cen, axis=-1, keepdims=True)
    y = cen * lax.rsqrt(var + LN_EPS) * g_ref[...] + beta_ref[...]
    o_ref[...] = y * _sigmoid(y)


def _rwkv_feature_rows(pc, prev, mu_ref, w2_ref, a2_ref, g2_ref, w0_ref, a0_ref, kkw_ref, ka_ref, rk_ref, seg_ref,
                       r_o, lw_o, k_o, v_o, kk_o, b_o, g_o, bonus_o):
    xs = pc + mu_ref[...] * (prev - pc)
    r = xs[:, 0:C_WIDTH]
    k = xs[:, C_WIDTH:2 * C_WIDTH]
    v = xs[:, 2 * C_WIDTH:3 * C_WIDTH]
    wa = xs[:, 3 * C_WIDTH:3 * C_WIDTH + LANES]
    gl = xs[:, 3 * C_WIDTH + LANES:]
    seg = seg_ref[...]
    lw = -DECAY_SCALE * _sigmoid(w0_ref[...] + _mm(jnp.tanh(wa), w2_ref[...]))
    a = _sigmoid(a0_ref[...] + _mm(wa, a2_ref[...]))
    g = _mm(_sigmoid(gl), g2_ref[...])
    kk = k * kkw_ref[...]
    ss = _mm_exact_rhs(kk * kk, seg, 2)
    kk = kk * lax.rsqrt(jnp.maximum(ss, 1e-24))
    k2 = k * (1.0 + (a - 1.0) * ka_ref[...])
    r_o[...] = r
    lw_o[...] = lw
    k_o[...] = k2
    v_o[...] = v
    kk_o[...] = kk
    b_o[...] = kk * a
    g_o[...] = g
    bonus_o[...] = _mm_exact_rhs(r * k2 * rk_ref[...], seg, 2) * v


RWKV_PASSES = 1
RWKV_GROUP = 8


_BNN = (((2,), (1,)), ((0,), (0,)))
_BNT = (((2,), (2,)), ((0,), (0,)))
_BTN = (((1,), (1,)), ((0,), (0,)))


def _pair_diag(x):
    first = lax.broadcasted_iota(jnp.int32, x.shape[1:], 1) < x.shape[-1] // 2
    return jnp.concatenate([jnp.where(first, x, 0.0), jnp.where(first, 0.0, x)], axis=1)


def _unit_lower_inverse(a_strict):
    g, c, _ = a_strict.shape
    ri = lax.broadcasted_iota(jnp.int32, (c, 2 * c), 0)
    ci = lax.broadcasted_iota(jnp.int32, (c, 2 * c), 1) % c
    inv = jnp.where(ri == ci, 1.0, 0.0) - jnp.where(((ri // 2) == (ci // 2)) & (ri > ci), a_strict, 0.0)
    s = 2
    while s < c:
        same = (ri // (2 * s)) == (ci // (2 * s))
        off = jnp.where(same & ((ri % (2 * s)) >= s) & ((ci % (2 * s)) < s), a_strict, 0.0)
        step = _mm(off, _pair_diag(inv), _BNN, RWKV_PASSES)
        inv = inv - _mm(inv, _pair_diag(step), _BNN, RWKV_PASSES)
        s *= 2
    return inv


def _rwkv_chunk_kernel(r_ref, lw_ref, k_ref, v_ref, kk_ref, b_ref, g_ref, bonus_ref, s0_ref, gng_ref, gnb_ref,
                       o_ref, s_out_ref, state_ref, *, c, nc):
    assert 2 * c == LANES and 2 * C_HEAD_DIM == LANES
    bb = r_ref.shape[0]
    pairs = C_HEADS // 2
    n = bb * pairs
    ci = pl.program_id(1)

    @pl.when(ci == 0)
    def _():
        for b in range(bb):
            for p in range(pairs):
                state_ref[b * pairs + p] = jnp.concatenate([s0_ref[b, 2 * p], s0_ref[b, 2 * p + 1]], axis=-1)

    ri = lax.broadcasted_iota(jnp.int32, (c, LANES), 0)
    cj = lax.broadcasted_iota(jnp.int32, (c, LANES), 1) % c
    incl = ri >= cj
    strict = ri > cj
    tri = jnp.where(lax.broadcasted_iota(jnp.int32, (c, c), 0) >= lax.broadcasted_iota(jnp.int32, (c, c), 1),
                    1.0, 0.0).astype(BF16)

    groups = []
    for ch in range(nc):
        rows = slice(ch * c, (ch + 1) * c)
        for b in range(bb):
            lw = lw_ref[b, rows, :]
            cum = _mm_exact_rhs_left(tri, lw)
            cend = cum[c - 1:c, :]
            w_inv = jnp.exp(-cum)
            w_end = jnp.exp(cend - cum)
            kvec = k_ref[b, rows, :]
            bvec = b_ref[b, rows, :]
            groups.append((kk_ref[b, rows, :] * jnp.exp(cum - lw), r_ref[b, rows, :] * jnp.exp(cum),
                           bvec * w_inv, kvec * w_inv, bvec * w_end, kvec * w_end, v_ref[b, rows, :],
                           jnp.exp(cend), bonus_ref[b, rows, :], g_ref[b, rows, :]))

    def paired(idx):
        return jnp.stack([grp[idx][:, p * LANES:(p + 1) * LANES] for grp in groups for p in range(pairs)])

    P = RWKV_PASSES
    kap, rho, bt, kt, bend, kend, v, wc = [paired(i) for i in range(8)]
    lhs2 = jnp.concatenate([kap, rho], axis=1)
    x2 = _mm(lhs2, jnp.concatenate([_pair_diag(bt), _pair_diag(kt)], axis=1), _BNT, P)
    a_kb = jnp.where(strict, x2[:, :c, :LANES], 0.0)
    a_rb = jnp.where(incl, x2[:, c:, :LANES], 0.0)
    a_kk = jnp.where(strict, x2[:, :c, LANES:], 0.0)
    a_rk = jnp.where(incl, x2[:, c:, LANES:], 0.0)
    t_inv = _unit_lower_inverse(a_kb)
    v_d = _pair_diag(v)
    solved = _mm(t_inv, jnp.concatenate([_pair_diag(kap), _pair_diag(_mm(a_kk, v_d, _BNN, P))], axis=2), _BNN, P)
    kap_p = solved[:, :, :LANES]
    v_p = solved[:, :, LANES:]
    er = lax.broadcasted_iota(jnp.int32, (LANES, LANES), 0)
    ec = lax.broadcasted_iota(jnp.int32, (LANES, LANES), 1)
    same_head = (er // C_HEAD_DIM) == (ec // C_HEAD_DIM)
    first = lax.broadcasted_iota(jnp.int32, (C_HEAD_DIM, LANES), 1) < C_HEAD_DIM
    with_bend = _mm(solved, bend, _BTN, P)
    pm = jnp.where(er == ec, wc, 0.0) - jnp.where(same_head, with_bend[:, :LANES], 0.0)
    fq = _mm(v, kend, _BTN, P) - with_bend[:, LANES:]
    q = jnp.where(first, fq[:, :C_HEAD_DIM], fq[:, C_HEAD_DIM:])
    via_rb = _mm(a_rb, jnp.concatenate([_pair_diag(kap_p), _pair_diag(v_p)], axis=2), _BNN, P)
    rp = rho - via_rb[:, :, :LANES]
    y0 = _mm(a_rk, v_d, _BNN, P) - via_rb[:, :, LANES:]

    s = state_ref[...]
    ys = []
    for ch in range(nc):
        sl = slice(ch * n, (ch + 1) * n)
        ys.append(_mm(rp[sl], _pair_diag(s), _BNT, P) + y0[sl])
        s = _mm(s, pm[sl], _BNN, P) + q[sl]
    state_ref[...] = s
    y = jnp.concatenate(ys, axis=0) if nc > 1 else ys[0]

    g_all = y.shape[0]
    seg = jnp.where(same_head, 1.0, 0.0).astype(BF16)
    head_mean = lambda z: (_mm_exact_rhs(z.reshape(g_all * c, LANES), seg, 2) * (1.0 / C_HEAD_DIM)
                           ).reshape(g_all, c, LANES)
    cen = y - head_mean(y)
    out = cen * lax.rsqrt(head_mean(cen * cen) + GN_EPS)
    bonus = paired(8)
    gate = paired(9)
    idx = 0
    for ch in range(nc):
        for b in range(bb):
            for p in range(pairs):
                sl = slice(p * LANES, (p + 1) * LANES)
                yn = out[idx] * gng_ref[:, sl] + gnb_ref[:, sl]
                o_ref[b, ch * c:(ch + 1) * c, sl] = (yn + bonus[idx]) * gate[idx]
                idx += 1

    @pl.when(ci == pl.num_programs(1) - 1)
    def _():
        for b in range(bb):
            for p in range(pairs):
                s_out_ref[b, 2 * p] = s[b * pairs + p][:, :C_HEAD_DIM]
                s_out_ref[b, 2 * p + 1] = s[b * pairs + p][:, C_HEAD_DIM:]


def _rwkv_chunks(feats, s0, gn_g, gn_b):
    bsz, t, _ = feats[0].shape
    c = RWKV_CHUNK
    pairs = C_HEADS // 2
    bb = RWKV_GROUP if bsz % RWKV_GROUP == 0 else 1
    nc = max(1, min(RWKV_GROUP // bb, t // c))
    seq = pl.BlockSpec((bb, nc * c, C_WIDTH), lambda b, i: (b, i, 0))
    st = pl.BlockSpec((bb, C_HEADS, C_HEAD_DIM, C_HEAD_DIM), lambda b, i: (b, 0, 0, 0))
    vec = pl.BlockSpec((1, C_WIDTH), lambda b, i: (0, 0))
    return pl.pallas_call(
        functools.partial(_rwkv_chunk_kernel, c=c, nc=nc),
        grid=(bsz // bb, t // (nc * c)),
        in_specs=[seq] * 8 + [st, vec, vec],
        out_specs=[seq, st],
        out_shape=[jax.ShapeDtypeStruct((bsz, t, C_WIDTH), F32),
                   jax.ShapeDtypeStruct((bsz, C_HEADS, C_HEAD_DIM, C_HEAD_DIM), F32)],
        scratch_shapes=[pltpu.VMEM((bb * pairs, C_HEAD_DIM, LANES), F32)],
        compiler_params=_cparams("parallel", "arbitrary"),
        name="rwkv_chunks",
    )(*feats, s0, gn_g, gn_b)


N_MERGE_REFS = 15


def _merge_rows(o0, l0, o1, l1, o2, l2, ob_ref, oc_ref, gate_ref, x_ref, wa_ref, wb_ref, wc_ref, wo_ref,
                gain_ref):
    ls = [l0[...], l1[...], l2[...]]
    mx = jnp.maximum(jnp.maximum(ls[0], ls[1]), ls[2])
    es = [jnp.exp(l - mx) for l in ls]
    den = es[0] + es[1] + es[2]
    o_a = (es[0] * o0[...] + es[1] * o1[...] + es[2] * o2[...]) / den
    merged = (gate_ref[:, 0:D_MODEL] * _mm(o_a, wa_ref[...])
              + gate_ref[:, D_MODEL:2 * D_MODEL] * _mm(ob_ref[...], wb_ref[...])
              + gate_ref[:, 2 * D_MODEL:] * _mm(oc_ref[...], wc_ref[...]))
    z = _mm(merged, wo_ref[...])
    return x_ref[...] + _rms(z, gain_ref[...])


def _gelu_tanh(x):
    return 0.5 * x * (1.0 + jnp.tanh(math.sqrt(2.0 / math.pi) * (x + 0.044715 * (x * x * x))))


FFN_PIECE = 128


def _ffn_kernel(*refs, tm, nb, cr):
    g1_ref, fu_ref, w_ref, b_ref, c_ref, fd_ref, g2_ref, out_ref, t_ref, ext_ref, f_ref = refs[N_MERGE_REFS:]

    @pl.when(pl.program_id(0) == 0)
    def _():
        ext_ref[0:cr, :] = c_ref[...]

    x1 = _merge_rows(*refs[:N_MERGE_REFS])
    h = _rms(x1, g1_ref[...]).astype(BF16)
    n_chunks = D_FF // FF_CHUNK
    split = (n_chunks + 1) // 2 * FF_CHUNK
    z = None
    for jc in range(n_chunks):
        pair = (slice(jc * FF_CHUNK, (jc + 1) * FF_CHUNK),
                slice(D_FF + jc * FF_CHUNK, D_FF + (jc + 1) * FF_CHUNK))
        for cols in pair:
            ext_ref[cr:cr + tm, cols] = _dg(h, fu_ref[:, cols], _NN)
        for r0 in range(0, tm, FFN_PIECE):
            cus = []
            for cols in pair:
                if nb % SUBLANES == 0:
                    taps = [ext_ref[r0 + cr - s * nb:r0 + cr - s * nb + FFN_PIECE, cols] for s in (2, 1, 0)]
                else:
                    blk = ext_ref[r0 + cr - SUBLANES:r0 + cr + FFN_PIECE, cols]
                    taps = [pltpu.roll(blk, s * nb, 0)[SUBLANES:] for s in (2, 1)] + [blk[SUBLANES:]]
                cus.append(w_ref[0:1, cols] * taps[0] + w_ref[1:2, cols] * taps[1]
                           + w_ref[2:3, cols] * taps[2] + b_ref[:, cols])
            f_ref[r0:r0 + FFN_PIECE, jc * FF_CHUNK:(jc + 1) * FF_CHUNK] = (
                _gelu_tanh(cus[0]) * cus[1]).astype(BF16)
        if (jc + 1) * FF_CHUNK == split:
            z = _dg(f_ref[:, 0:split], fd_ref[0:split, :], _NN)
    z = z + _dg(f_ref[:, split:], fd_ref[split:, :], _NN)
    tail = ext_ref[tm:tm + cr, :]
    t_ref[...] = tail
    ext_ref[0:cr, :] = tail
    out_ref[...] = x1 + _rms(z, g2_ref[...])


def _merge_ffn(att, o_b, o_c, gates, x, ctx, p, nb):
    m = x.shape[0]
    tm = min(m, 256)
    cr = ctx.shape[0]
    row = lambda i: (i, 0)
    fixed = lambda i: (0, 0)
    rows = lambda w: pl.BlockSpec((tm, w), row)
    once = lambda shape: pl.BlockSpec(shape, fixed, pipeline_mode=pl.Buffered(1))
    resident = lambda name: _layer_weight(p[name], p['layer'], pipeline_mode=pl.Buffered(1))
    merge_ws = ('w_br_a', 'w_br_b', 'w_br_c', 'w_out')
    x2, tail = pl.pallas_call(
        functools.partial(_ffn_kernel, tm=tm, nb=nb, cr=cr),
        grid=(m // tm,),
        in_specs=[rows(LANES)] * 6 + [rows(B_WIDTH), rows(C_WIDTH), rows(IN_GATE), rows(D_MODEL)]
                 + [resident(name) for name in merge_ws] + [once((1, D_MODEL))]
                 + [once((1, D_MODEL)),
                  _layer_weight(p['f_up'], p['layer'], pipeline_mode=pl.Buffered(1)),
                  once((F_CONV, 2 * D_FF)), once((1, 2 * D_FF)), once((cr, 2 * D_FF)),
                  _layer_weight(p['f_down'], p['layer'], pipeline_mode=pl.Buffered(1)), once((1, D_MODEL))],
        out_specs=[pl.BlockSpec((tm, D_MODEL), row), pl.BlockSpec((cr, 2 * D_FF), row)],
        out_shape=[jax.ShapeDtypeStruct((m, D_MODEL), F32),
                   jax.ShapeDtypeStruct(((m // tm) * cr, 2 * D_FF), F32)],
        scratch_shapes=[pltpu.VMEM((cr + tm, 2 * D_FF), F32), pltpu.VMEM((tm, D_FF), BF16)],
        compiler_params=_cparams("arbitrary"),
        name="merge_ffn",
    )(*att, o_b, o_c, gates, x, *[p[name] for name in merge_ws], p['norm_mix_post'],
      p['norm_ffn_pre'], p['f_up'], p['f_dw_w'], p['f_dw_b'], ctx, p['f_down'], p['norm_ffn_post'])
    return x2, tail[-cr:]


def _to_time_major(a):
    a = jnp.swapaxes(a, 0, 1)
    return a.reshape((a.shape[0] * a.shape[1],) + a.shape[2:])


def _to_batch_major(a, ts):
    return jnp.swapaxes(a.reshape(ts, a.shape[0] // ts, a.shape[1]), 0, 1)


def _layer_prompt(x, tables, p):
    t = x.shape[0]
    qkv, glu, pc, gates, o_b, feats = _in_proj(x, tables, jnp.zeros((4 * SUBLANES, B_WIDTH), F32),
                                               jnp.zeros((SUBLANES, IN_C), F32), p, 1)
    att = []
    for g in range(N_A_GROUPS):
        att.extend(_attn_prompt(qkv, g))
    s0 = jnp.zeros((1, C_HEADS, C_HEAD_DIM, C_HEAD_DIM), F32)
    o_c, s_new = _rwkv_chunks([f[None] for f in feats], s0, p['c_gn_g'], p['c_gn_b'])
    x2, tail = _merge_ffn(att, o_b, o_c[0], gates, x, jnp.zeros((SUBLANES, 2 * D_FF), F32), p, 1)
    kv = [qkv[t - min(w, t):, 3 * A_WIDTH * g + A_WIDTH:3 * A_WIDTH * (g + 1)]
          .reshape(1, min(w, t), 2, A_HEADS, A_HEAD_DIM) for g, (w, _) in enumerate(A_GROUPS)]
    f_tail = tail[-(F_CONV - 1):][None]
    return x2, kv, glu[t - (B_CONV - 1):][None], pc[t - 1:], s_new, f_tail


def _layer_sample(x, tables, caches_t, layer, b_ctx, c_shift, c_state, f_ctx, p, ts):
    nb = x.shape[0] // ts
    qkv, glu, pc, gates, o_b, feats = _in_proj(x, tables, _to_time_major(b_ctx), c_shift, p, nb)
    qkv_bm = _to_batch_major(qkv, ts)
    qkv_pad = jnp.pad(qkv_bm, ((0, 0), (0, SUBLANES - ts), (0, 0)))
    o, lse = _attn_sample(qkv_pad, caches_t, layer, ts)
    att = []
    for g in range(N_A_GROUPS):
        att.append(_to_time_major(o[:, :ts, g * LANES:(g + 1) * LANES]))
        att.append(_to_time_major(lse[:, :ts, g * LANES:(g + 1) * LANES]))
    feats_bm = [jnp.pad(_to_batch_major(f, ts), ((0, 0), (0, RWKV_CHUNK - ts), (0, 0))) for f in feats]
    o_c, s_new = _rwkv_chunks(feats_bm, c_state, p['c_gn_g'], p['c_gn_b'])
    x2, tail = _merge_ffn(att, o_b, _to_time_major(o_c[:, :ts]), gates, x, _to_time_major(f_ctx), p, nb)
    kv = [_to_batch_major(qkv[:, 3 * A_WIDTH * g + A_WIDTH:3 * A_WIDTH * (g + 1)], ts)
          .reshape(nb, ts, 2, A_HEADS, A_HEAD_DIM) for g in range(N_A_GROUPS)]
    b_new = jnp.concatenate([b_ctx, _to_batch_major(glu, ts)], axis=1)[:, -(B_CONV - 1):]
    f_new = jnp.concatenate([f_ctx, _to_batch_major(tail, F_CONV - 1)], axis=1)
    return x2, kv, b_new, pc[(ts - 1) * nb:], s_new, f_new[:, -(F_CONV - 1):]


def kernel(x_prompt, x_sample, cache_a_kv0, cache_a_kv1, cache_a_kv2, state_b_conv, state_c_shift, state_c_wkv, state_f_conv, norm_mix_pre, norm_mix_post, norm_ffn_pre, norm_ffn_post, w_in, b_dw_w, b_dw_b, b_ln_g, b_ln_b, c_mu, c_w0, c_w2, c_a0, c_a2, c_g2, c_kk, c_ka, c_rk, c_gn_g, c_gn_b, w_br_a, w_br_b, w_br_c, w_out, f_up, f_dw_w, f_dw_b, f_down):
    depth = w_in.shape[0]
    bp, tp, _ = x_prompt.shape
    bs, ts, _ = x_sample.shape
    assert bp == 1 and ts <= SUBLANES
    caches_t = [jnp.transpose(c, (0, 1, 3, 4, 5, 2)) for c in (cache_a_kv0, cache_a_kv1, cache_a_kv2)]

    head_id = jnp.arange(C_WIDTH) // C_HEAD_DIM
    seg = (head_id[:, None] == head_id[None, :]).astype(BF16)
    zpad = jnp.zeros((LANES - C_DECAY_RANK, C_WIDTH), F32)

    tab_p = _rope_tables(tp, 0, 1)
    tab_s = _rope_tables(ts * bs, PAST_LEN, bs)

    stacked = {name: w.astype(BF16) for name, w in (
        ('w_in', w_in), ('w_br_a', w_br_a), ('w_br_b', w_br_b), ('w_br_c', w_br_c), ('w_out', w_out),
        ('f_up', f_up), ('f_down', f_down))}

    y_p = x_prompt[0]
    y_s = _to_time_major(x_sample)
    outs_p, outs_s = [], []
    for l in range(depth):
        vec = lambda a: a[l][None, :]
        p = {
            **stacked, 'layer': l,
            'norm_mix_pre': vec(norm_mix_pre), 'norm_mix_post': vec(norm_mix_post),
            'norm_ffn_pre': vec(norm_ffn_pre), 'norm_ffn_post': vec(norm_ffn_post),
            'b_dw_w': b_dw_w[l], 'b_dw_b': vec(b_dw_b), 'b_ln_g': vec(b_ln_g), 'b_ln_b': vec(b_ln_b),
            'c_mu': vec(c_mu), 'c_w0': vec(c_w0), 'c_a0': vec(c_a0),
            'c_w2p': jnp.concatenate([c_w2[l], zpad], axis=0),
            'c_a2p': jnp.concatenate([zpad, c_a2[l]], axis=0),
            'c_g2': c_g2[l], 'c_kk': vec(c_kk), 'c_ka': vec(c_ka),
            'c_rk': c_rk[l].reshape(1, C_WIDTH), 'c_gn_g': vec(c_gn_g), 'c_gn_b': vec(c_gn_b),
            'seg': seg,
            'f_dw_w': f_dw_w[l], 'f_dw_b': vec(f_dw_b),
        }
        y_p, *rest_p = _layer_prompt(y_p, tab_p, p)
        outs_p.append(rest_p)
        y_s, *rest_s = _layer_sample(y_s, tab_s, caches_t, l, state_b_conv[l], state_c_shift[l],
                                     state_c_wkv[l], state_f_conv[l], p, ts)
        outs_s.append(rest_s)

    stack = lambda outs, f: jnp.stack([f(o) for o in outs])
    res = [y_p[None], _to_batch_major(y_s, ts)]
    res += [stack(outs_p, lambda o, g=g: o[0][g]) for g in range(N_A_GROUPS)]
    res += [stack(outs_s, lambda o, g=g: o[0][g]) for g in range(N_A_GROUPS)]
    for idx in range(1, 5):
        res += [stack(outs_p, lambda o: o[idx]), stack(outs_s, lambda o: o[idx])]
    return tuple(res)
```

```python
import functools
import math

import jax
import jax.numpy as jnp
from jax import lax
from jax.experimental import pallas as pl
from jax.experimental.pallas import tpu as pltpu

F32 = jnp.float32
BF16 = jnp.bfloat16

D_MODEL = 1024
PAST_LEN = 16384
A_GROUPS = ((128, 1), (512, 4), (2048, 16))
N_A_GROUPS = len(A_GROUPS)
A_HEAD_DIM = 32
A_HEADS = 4
A_WIDTH = A_HEADS * A_HEAD_DIM
ROPE_THETA = 10000.0
ATT_BLOCK = 128
B_WIDTH = 256
B_CONV = 31
C_HEAD_DIM = 64
C_WIDTH = 384
C_HEADS = 6
C_DECAY_RANK = 64
C_ICLR_RANK = 64
C_GATE_RANK = 128
DECAY_SCALE = math.exp(-0.5)
D_FF = 2816
F_CONV = 3
IN_A = N_A_GROUPS * 3 * A_WIDTH
IN_B = 2 * B_WIDTH
IN_C = 3 * C_WIDTH + C_DECAY_RANK + C_ICLR_RANK + C_GATE_RANK
IN_GATE = 3 * D_MODEL
IN_WIDTH = IN_A + IN_B + IN_C + IN_GATE
RMS_EPS = 1e-6
LN_EPS = 1e-5
GN_EPS = C_HEAD_DIM * 1e-5
NEG_BIG = -1e30

LANES = 128
SUBLANES = 8
VMEM_LIMIT = 56 * 1024 * 1024

RWKV_CHUNK = 64
FF_CHUNK = 256


def _cparams(*sem):
    return pltpu.CompilerParams(dimension_semantics=sem, vmem_limit_bytes=VMEM_LIMIT)


def _split_bf16(x, n):
    pieces = []
    rem = x
    for _ in range(n):
        p = rem.astype(BF16)
        pieces.append(p)
        rem = rem - p.astype(F32)
    return pieces


_NN = (((1,), (0,)), ((), ()))
_NT = (((1,), (1,)), ((), ()))


def _dg(a, b, dims):
    return lax.dot_general(a, b, dims, preferred_element_type=F32)


def _mm(a, b, dims=_NN, passes=1):
    if passes == 1:
        return _dg(a.astype(BF16), b.astype(BF16), dims)
    a_hi, a_lo = _split_bf16(a, 2)
    b_hi, b_lo = _split_bf16(b, 2)
    return _dg(a_hi, b_hi, dims) + (_dg(a_hi, b_lo, dims) + _dg(a_lo, b_hi, dims))


def _mm_exact_rhs(a, b_bf16, pieces=3):
    out = None
    for p in _split_bf16(a, pieces):
        t = _dg(p, b_bf16, _NN)
        out = t if out is None else out + t
    return out


def _mm_exact_rhs_left(a_bf16, b, pieces=3):
    out = None
    for p in _split_bf16(b, pieces):
        t = _dg(a_bf16, p, _NN)
        out = t if out is None else out + t
    return out


def _rms(x, g):
    ms = jnp.mean(x * x, axis=-1, keepdims=True)
    return x * lax.rsqrt(ms + RMS_EPS) * g


def _sigmoid(x):
    return 1.0 / (1.0 + jnp.exp(-x))


def _rope_table_kernel(inv_ref, cos_ref, sin_ref, cl_ref, sl_ref, *, tm, pos0, nb):
    inv = inv_ref[...]

    @pl.when(pl.program_id(0) == 0)
    def _():
        within = (lax.broadcasted_iota(jnp.int32, (tm, LANES), 0) // nb).astype(F32) * inv
        cl_ref[...] = jnp.cos(within)
        sl_ref[...] = jnp.sin(within)

    base = (pos0 + pl.program_id(0) * (tm // nb)).astype(F32) * jnp.broadcast_to(inv, (SUBLANES, LANES))
    ch = jnp.cos(base)[0:1, :]
    sh = jnp.sin(base)[0:1, :]
    lane = lax.broadcasted_iota(jnp.int32, (tm, LANES), 1)
    sign = jnp.where((lane % A_HEAD_DIM) < A_HEAD_DIM // 2, -1.0, 1.0)
    cos_ref[...] = ch * cl_ref[...] - sh * sl_ref[...]
    sin_ref[...] = (sh * cl_ref[...] + ch * sl_ref[...]) * sign


def _rope_tables(m, pos0, nb):
    half = A_HEAD_DIM // 2
    inv = ROPE_THETA ** (-(jnp.arange(half, dtype=F32) * 2.0 / A_HEAD_DIM))
    inv_lane = jnp.tile(inv, 2 * A_HEADS)[None, :]
    tm = min(m, 512)
    assert tm % nb == 0
    return pl.pallas_call(
        functools.partial(_rope_table_kernel, tm=tm, pos0=pos0, nb=nb),
        grid=(m // tm,),
        in_specs=[pl.BlockSpec((1, LANES), lambda i: (0, 0))],
        out_specs=[pl.BlockSpec((tm, LANES), lambda i: (i, 0))] * 2,
        out_shape=[jax.ShapeDtypeStruct((m, LANES), F32)] * 2,
        scratch_shapes=[pltpu.VMEM((tm, LANES), F32)] * 2,
        compiler_params=_cparams("arbitrary"),
        name="rope_tables",
    )(inv_lane)


N_C_PARAMS = 10


def _in_proj_kernel(x_ref, g_ref, w_ref, cos_ref, sin_ref, bctx_ref, bw_ref, bb_ref, bg_ref, bbeta_ref, cctx_ref,
                    *rest, tm, nb, bctx, coff, carry):
    c_params = rest[:N_C_PARAMS]
    qkv_ref, glu_ref, pc_ref, gate_ref, ob_ref = rest[N_C_PARAMS:N_C_PARAMS + 5]
    c_outs = rest[N_C_PARAMS + 5:N_C_PARAMS + 13]
    extb_ref, extc_ref = rest[N_C_PARAMS + 13:N_C_PARAMS + 15]
    shift_ref = rest[N_C_PARAMS + 15] if len(rest) > N_C_PARAMS + 15 else None

    @pl.when(pl.program_id(0) == 0)
    def _():
        extb_ref[0:bctx, :] = bctx_ref[...]
        extc_ref[0:coff, :] = cctx_ref[...]

    h = _rms(x_ref[...], g_ref[...]).astype(BF16)
    cos = cos_ref[...]
    sin = sin_ref[...]
    lane = lax.broadcasted_iota(jnp.int32, cos.shape, 1)
    first_half = (lane % A_HEAD_DIM) < A_HEAD_DIM // 2
    def branch_gate(j):
        cols = slice(j * D_MODEL, (j + 1) * D_MODEL)
        gate_ref[:, cols] = _sigmoid(_dg(h, w_ref[:, IN_A + IN_B + IN_C + j * D_MODEL:
                                                  IN_A + IN_B + IN_C + (j + 1) * D_MODEL], _NN)).astype(gate_ref.dtype)

    pb = _dg(h, w_ref[:, IN_A:IN_A + IN_B], _NN)
    glu = pb[:, :B_WIDTH] * _sigmoid(pb[:, B_WIDTH:])
    glu_ref[...] = glu
    extb_ref[bctx:bctx + tm, :] = glu
    branch_gate(0)
    _conv_b_body(extb_ref, bw_ref, bb_ref, bg_ref, bbeta_ref, ob_ref, tm=tm, nb=nb,
                 base=bctx - (B_CONV - 1) * nb, shift_ref=shift_ref)
    pc = _dg(h, w_ref[:, IN_A + IN_B:IN_A + IN_B + IN_C], _NN)
    pc_ref[...] = pc
    extc_ref[coff:coff + tm, :] = pc
    branch_gate(1)
    _rwkv_feature_rows(pc, extc_ref[coff - nb:coff - nb + tm, :], *c_params, *c_outs)
    pa = _dg(h, w_ref[:, 0:IN_A], _NN)
    branch_gate(2)
    for blk in range(IN_A // LANES):
        y = pa[:, blk * LANES:(blk + 1) * LANES]
        if blk % 3 != 2:
            swapped = jnp.where(first_half, pltpu.roll(y, LANES - A_HEAD_DIM // 2, 1),
                                pltpu.roll(y, A_HEAD_DIM // 2, 1))
            y = y * cos + swapped * sin
        qkv_ref[:, blk * LANES:(blk + 1) * LANES] = y
    if carry:
        extb_ref[0:bctx, :] = extb_ref[tm:tm + bctx, :]
        extc_ref[0:coff, :] = extc_ref[tm:tm + coff, :]


def _layer_weight(w, layer, **kw):
    return pl.BlockSpec((None,) + w.shape[1:], lambda *_: (layer, 0, 0), **kw)


def _in_proj(x, tables, b_ctx, c_ctx, p, nb):
    m = x.shape[0]
    tm = min(m, 256)
    bctx, coff = b_ctx.shape[0], c_ctx.shape[0]
    row = lambda i: (i, 0)
    fixed = lambda i: (0, 0)
    full = lambda a: pl.BlockSpec(a.shape, fixed)
    widths = (IN_A, B_WIDTH, IN_C, IN_GATE, B_WIDTH) + (C_WIDTH,) * 8
    b_params = (b_ctx, p['b_dw_w'], p['b_dw_b'], p['b_ln_g'], p['b_ln_b'])
    c_params = (c_ctx, p['c_mu'], p['c_w2p'], p['c_a2p'], p['c_g2'], p['c_w0'], p['c_a0'], p['c_kk'], p['c_ka'],
                p['c_rk'], p['seg'])
    scratch = [pltpu.VMEM((bctx + tm, B_WIDTH), F32), pltpu.VMEM((coff + tm, IN_C), F32)]
    if nb == 1:
        scratch.append(pltpu.VMEM((SUBLANES, bctx + tm - SUBLANES, B_WIDTH), F32))
    outs = pl.pallas_call(
        functools.partial(_in_proj_kernel, tm=tm, nb=nb, bctx=bctx, coff=coff, carry=m > tm),
        grid=(m // tm,),
        in_specs=[pl.BlockSpec((tm, D_MODEL), row), pl.BlockSpec((1, D_MODEL), fixed),
                  _layer_weight(p['w_in'], p['layer'], pipeline_mode=pl.Buffered(1)),
                  pl.BlockSpec((tm, LANES), row), pl.BlockSpec((tm, LANES), row)]
                 + [full(a) for a in b_params + c_params],
        out_specs=[pl.BlockSpec((tm, w), row) for w in widths],
        out_shape=[jax.ShapeDtypeStruct((m, w), BF16 if w == IN_GATE else F32) for w in widths],
        scratch_shapes=scratch,
        compiler_params=_cparams("arbitrary"),
        name="in_proj",
    )(x, p['norm_mix_pre'], p['w_in'], *tables, *b_params, *c_params)
    return outs[0], outs[1], outs[2], outs[3], outs[4], outs[5:]


def _softmax_heads(q, k_bf16, v_bf16, valid):
    rows = q.shape[0]
    lane_head = lax.broadcasted_iota(jnp.int32, q.shape, 1) // A_HEAD_DIM
    scale = 1.0 / math.sqrt(A_HEAD_DIM)
    q_heads = jnp.concatenate([jnp.where(lane_head == h, q, 0.0) for h in range(A_HEADS)], axis=0)
    s_all = _dg(q_heads.astype(BF16), k_bf16, _NT) * scale
    ps, ls, lses = [], [], []
    for h in range(A_HEADS):
        s = jnp.where(valid, s_all[h * rows:(h + 1) * rows], NEG_BIG)
        mx = jnp.max(s, axis=-1, keepdims=True)
        p = jnp.exp(s - mx)
        l = jnp.sum(p, axis=-1, keepdims=True)
        ps.append(p.astype(BF16))
        ls.append(l)
        lses.append(mx + jnp.log(l))
    o_heads = _dg(jnp.concatenate(ps, axis=0), v_bf16, _NN)
    o_all = jnp.zeros(q.shape, F32)
    lse_all = jnp.zeros(q.shape, F32)
    for h in range(A_HEADS):
        hm = lane_head == h
        o_all = jnp.where(hm, o_heads[h * rows:(h + 1) * rows] / ls[h], o_all)
        lse_all = jnp.where(hm, lses[h], lse_all)
    return o_all, lse_all


ATT_UNROLL = 4


def _attn_prompt_kernel(q_ref, kh_ref, kc_ref, vh_ref, vc_ref, o_ref, lse_ref, kx_ref, vx_ref, *, d, nsub):
    i = pl.program_id(0)
    hb = d * ATT_BLOCK
    kx_ref[0:hb, :] = kh_ref[...]
    kx_ref[hb:, :] = kc_ref[...]
    vx_ref[0:hb, :] = vh_ref[...]
    vx_ref[hb:, :] = vc_ref[...]
    qi = lax.broadcasted_iota(jnp.int32, (ATT_BLOCK, 2 * ATT_BLOCK), 0)
    kj = lax.broadcasted_iota(jnp.int32, (ATT_BLOCK, 2 * ATT_BLOCK), 1)
    delta = qi + ATT_BLOCK - kj
    band = (delta >= 0) & (delta <= ATT_BLOCK)
    current = kj >= ATT_BLOCK

    def problem(it, carry):
        j = it // d
        start = j * hb + it % d
        valid = band & (current | (i > 0) | (j > 0))
        keys = pl.ds(start, 2 * ATT_BLOCK, stride=d)
        rows = pl.ds(start, ATT_BLOCK, stride=d)
        o, lse = _softmax_heads(q_ref[rows, :], kx_ref[keys, :].astype(BF16), vx_ref[keys, :].astype(BF16), valid)
        o_ref[rows, :] = o
        lse_ref[rows, :] = lse
        return carry

    lax.fori_loop(0, nsub * d, problem, 0, unroll=ATT_UNROLL)


def _attn_prompt(qkv, group):
    t = qkv.shape[0]
    _, d = A_GROUPS[group]
    hb = d * ATT_BLOCK
    nsub = max(1, 8 // d)
    while t % (nsub * hb):
        nsub //= 2
    col = 3 * group
    blk = (nsub * hb, LANES)
    cur = lambda off: (lambda i: (i, col + off))
    halo = lambda off: (lambda i: (jnp.maximum(i * nsub - 1, 0), col + off))
    return pl.pallas_call(
        functools.partial(_attn_prompt_kernel, d=d, nsub=nsub),
        grid=(t // (nsub * hb),),
        in_specs=[pl.BlockSpec(blk, cur(0)), pl.BlockSpec((hb, LANES), halo(1)), pl.BlockSpec(blk, cur(1)),
                  pl.BlockSpec((hb, LANES), halo(2)), pl.BlockSpec(blk, cur(2))],
        out_specs=[pl.BlockSpec(blk, lambda i: (i, 0))] * 2,
        out_shape=[jax.ShapeDtypeStruct((t, LANES), F32)] * 2,
        scratch_shapes=[pltpu.VMEM(((nsub + 1) * hb, LANES), F32)] * 2,
        compiler_params=_cparams("parallel"),
        name=f"attn_prompt_g{group}",
    )(qkv, qkv, qkv, qkv, qkv)


def _attn_sample_kernel(qkv_ref, c0_ref, c1_ref, c2_ref, o_ref, lse_ref, *, ts):
    for bi in range(qkv_ref.shape[0]):
        _attn_sample_one(bi, qkv_ref, c0_ref, c1_ref, c2_ref, o_ref, lse_ref, ts)


def _attn_sample_one(bi, qkv_ref, c0_ref, c1_ref, c2_ref, o_ref, lse_ref, ts):
    qkv = qkv_ref[bi]
    rows = qkv.shape[0]
    pad = jnp.zeros((LANES - rows, LANES), F32)
    for g, (window, d) in enumerate(A_GROUPS):
        base = 3 * A_WIDTH * g
        q = qkv[:, base:base + LANES]
        k_new = jnp.concatenate([qkv[:, base + LANES:base + 2 * LANES], pad], axis=0)
        v_new = jnp.concatenate([qkv[:, base + 2 * LANES:base + 3 * LANES], pad], axis=0)
        c_ref = (c0_ref, c1_ref, c2_ref)[g]
        k_t = c_ref[bi, 0].reshape(A_WIDTH, window).astype(BF16)
        v_t = c_ref[bi, 1].reshape(A_WIDTH, window).astype(BF16)
        qi = lax.broadcasted_iota(jnp.int32, (rows, window), 0)
        c = lax.broadcasted_iota(jnp.int32, (rows, window), 1)
        qn = lax.broadcasted_iota(jnp.int32, (rows, LANES), 0)
        cn = lax.broadcasted_iota(jnp.int32, (rows, LANES), 1)
        if d == 1:
            valid_c, valid_n = c >= qi, cn <= qn
        else:
            valid_c, valid_n = (c % d) == qi, cn == qn
        lane_head = lax.broadcasted_iota(jnp.int32, q.shape, 1) // A_HEAD_DIM
        scale = 1.0 / math.sqrt(A_HEAD_DIM)
        q_heads = jnp.concatenate([jnp.where(lane_head == h, q, 0.0) for h in range(A_HEADS)],
                                  axis=0).astype(BF16)
        sc_all = _dg(q_heads, k_t, _NN) * scale
        sn_all = _dg(q_heads, k_new.astype(BF16), _NT) * scale
        pcs, pns, ls, lses = [], [], [], []
        for h in range(A_HEADS):
            s_c = jnp.where(valid_c, sc_all[h * rows:(h + 1) * rows], NEG_BIG)
            s_n = jnp.where(valid_n, sn_all[h * rows:(h + 1) * rows], NEG_BIG)
            mx = jnp.maximum(jnp.max(s_c, axis=-1, keepdims=True), jnp.max(s_n, axis=-1, keepdims=True))
            pcs.append(jnp.exp(s_c - mx))
            pns.append(jnp.exp(s_n - mx))
            ls.append(jnp.sum(pcs[-1], axis=-1, keepdims=True) + jnp.sum(pns[-1], axis=-1, keepdims=True))
            lses.append(mx + jnp.log(ls[-1]))
        o_heads = (_dg(jnp.concatenate(pcs, axis=0).astype(BF16), v_t, _NT)
                   + _dg(jnp.concatenate(pns, axis=0).astype(BF16), v_new.astype(BF16), _NN))
        o_all = jnp.zeros(q.shape, F32)
        lse_all = jnp.zeros(q.shape, F32)
        for h in range(A_HEADS):
            hm = lane_head == h
            o_all = jnp.where(hm, o_heads[h * rows:(h + 1) * rows] / ls[h], o_all)
            lse_all = jnp.where(hm, lses[h], lse_all)
        o_ref[bi, :, g * LANES:(g + 1) * LANES] = o_all
        lse_ref[bi, :, g * LANES:(g + 1) * LANES] = lse_all


def _attn_sample(qkv_bm, caches_t, layer, ts):
    bsz, rows, _ = qkv_bm.shape
    bb = 4 if bsz % 4 == 0 else 1
    specs = [pl.BlockSpec((bb, rows, IN_A), lambda b: (b, 0, 0))]
    for (window, d), c in zip(A_GROUPS, caches_t):
        assert c.shape[-1] == window and window % d == 0 and (d == 1 or d >= ts)
        specs.append(pl.BlockSpec((None, bb, 2, A_HEADS, A_HEAD_DIM, window),
                                  lambda b: (layer, b, 0, 0, 0, 0)))
    out = jax.ShapeDtypeStruct((bsz, rows, N_A_GROUPS * LANES), F32)
    return pl.pallas_call(
        functools.partial(_attn_sample_kernel, ts=ts),
        grid=(bsz // bb,),
        in_specs=specs,
        out_specs=[pl.BlockSpec((bb, rows, N_A_GROUPS * LANES), lambda b: (b, 0, 0))] * 2,
        out_shape=[out, out],
        compiler_params=_cparams("parallel"),
        name="attn_sample",
    )(qkv_bm, *caches_t)


def _conv_b_body(ext_ref, w_ref, b_ref, g_ref, beta_ref, o_ref, *, tm, nb, base, shift_ref=None):
    acc = jnp.zeros((tm, B_WIDTH), F32) + b_ref[...]
    if shift_ref is not None:
        n = shift_ref.shape[1]
        for s in range(1, SUBLANES):
            shift_ref[s] = ext_ref[s:s + n, :]
    for j in range(B_CONV):
        off = base + j * nb
        if shift_ref is None or off % SUBLANES == 0:
            tap = ext_ref[pl.ds(off, tm), :]
        else:
            tap = shift_ref[off % SUBLANES, pl.ds(off - off % SUBLANES, tm), :]
        acc = acc + w_ref[j:j + 1, :] * tap
    mu = jnp.mean(acc, axis=-1, keepdims=True)
    cen = acc - mu
    var = jnp.mean(cen * cen, axis=-1, keepdims=True)
    y = cen * lax.rsqrt(var + LN_EPS) * g_ref[...] + beta_ref[...]
    o_ref[...] = y * _sigmoid(y)


def _rwkv_feature_rows(pc, prev, mu_ref, w2_ref, a2_ref, g2_ref, w0_ref, a0_ref, kkw_ref, ka_ref, rk_ref, seg_ref,
                       r_o, lw_o, k_o, v_o, kk_o, b_o, g_o, bonus_o):
    xs = pc + mu_ref[...] * (prev - pc)
    r = xs[:, 0:C_WIDTH]
    k = xs[:, C_WIDTH:2 * C_WIDTH]
    v = xs[:, 2 * C_WIDTH:3 * C_WIDTH]
    wa = xs[:, 3 * C_WIDTH:3 * C_WIDTH + LANES]
    gl = xs[:, 3 * C_WIDTH + LANES:]
    seg = seg_ref[...]
    lw = -DECAY_SCALE * _sigmoid(w0_ref[...] + _mm(jnp.tanh(wa), w2_ref[...]))
    a = _sigmoid(a0_ref[...] + _mm(wa, a2_ref[...]))
    g = _mm(_sigmoid(gl), g2_ref[...])
    kk = k * kkw_ref[...]
    ss = _mm_exact_rhs(kk * kk, seg, 2)
    kk = kk * lax.rsqrt(jnp.maximum(ss, 1e-24))
    k2 = k * (1.0 + (a - 1.0) * ka_ref[...])
    r_o[...] = r
    lw_o[...] = lw
    k_o[...] = k2
    v_o[...] = v
    kk_o[...] = kk
    b_o[...] = kk * a
    g_o[...] = g
    bonus_o[...] = _mm_exact_rhs(r * k2 * rk_ref[...], seg, 2) * v


RWKV_PASSES = 1
RWKV_GROUP = 8


_BNN = (((2,), (1,)), ((0,), (0,)))
_BNT = (((2,), (2,)), ((0,), (0,)))
_BTN = (((1,), (1,)), ((0,), (0,)))


def _pair_diag(x):
    first = lax.broadcasted_iota(jnp.int32, x.shape[1:], 1) < x.shape[-1] // 2
    return jnp.concatenate([jnp.where(first, x, 0.0), jnp.where(first, 0.0, x)], axis=1)


def _unit_lower_inverse(a_strict):
    g, c, _ = a_strict.shape
    ri = lax.broadcasted_iota(jnp.int32, (c, 2 * c), 0)
    ci = lax.broadcasted_iota(jnp.int32, (c, 2 * c), 1) % c
    inv = jnp.where(ri == ci, 1.0, 0.0) - jnp.where(((ri // 2) == (ci // 2)) & (ri > ci), a_strict, 0.0)
    s = 2
    while s < c:
        same = (ri // (2 * s)) == (ci // (2 * s))
        off = jnp.where(same & ((ri % (2 * s)) >= s) & ((ci % (2 * s)) < s), a_strict, 0.0)
        step = _mm(off, _pair_diag(inv), _BNN, RWKV_PASSES)
        inv = inv - _mm(inv, _pair_diag(step), _BNN, RWKV_PASSES)
        s *= 2
    return inv


def _rwkv_chunk_kernel(r_ref, lw_ref, k_ref, v_ref, kk_ref, b_ref, g_ref, bonus_ref, s0_ref, gng_ref, gnb_ref,
                       o_ref, s_out_ref, state_ref, *, c, nc):
    assert 2 * c == LANES and 2 * C_HEAD_DIM == LANES
    bb = r_ref.shape[0]
    pairs = C_HEADS // 2
    n = bb * pairs
    ci = pl.program_id(1)

    @pl.when(ci == 0)
    def _():
        for b in range(bb):
            for p in range(pairs):
                state_ref[b * pairs + p] = jnp.concatenate([s0_ref[b, 2 * p], s0_ref[b, 2 * p + 1]], axis=-1)

    ri = lax.broadcasted_iota(jnp.int32, (c, LANES), 0)
    cj = lax.broadcasted_iota(jnp.int32, (c, LANES), 1) % c
    incl = ri >= cj
    strict = ri > cj
    tri = jnp.where(lax.broadcasted_iota(jnp.int32, (c, c), 0) >= lax.broadcasted_iota(jnp.int32, (c, c), 1),
                    1.0, 0.0).astype(BF16)

    groups = []
    for ch in range(nc):
        rows = slice(ch * c, (ch + 1) * c)
        for b in range(bb):
            lw = lw_ref[b, rows, :]
            cum = _mm_exact_rhs_left(tri, lw)
            cend = cum[c - 1:c, :]
            w_inv = jnp.exp(-cum)
            w_end = jnp.exp(cend - cum)
            kvec = k_ref[b, rows, :]
            bvec = b_ref[b, rows, :]
            groups.append((kk_ref[b, rows, :] * jnp.exp(cum - lw), r_ref[b, rows, :] * jnp.exp(cum),
                           bvec * w_inv, kvec * w_inv, bvec * w_end, kvec * w_end, v_ref[b, rows, :],
                           jnp.exp(cend), bonus_ref[b, rows, :], g_ref[b, rows, :]))

    def paired(idx):
        return jnp.stack([grp[idx][:, p * LANES:(p + 1) * LANES] for grp in groups for p in range(pairs)])

    P = RWKV_PASSES
    kap, rho, bt, kt, bend, kend, v, wc = [paired(i) for i in range(8)]
    lhs2 = jnp.concatenate([kap, rho], axis=1)
    x2 = _mm(lhs2, jnp.concatenate([_pair_diag(bt), _pair_diag(kt)], axis=1), _BNT, P)
    a_kb = jnp.where(strict, x2[:, :c, :LANES], 0.0)
    a_rb = jnp.where(incl, x2[:, c:, :LANES], 0.0)
    a_kk = jnp.where(strict, x2[:, :c, LANES:], 0.0)
    a_rk = jnp.where(incl, x2[:, c:, LANES:], 0.0)
    t_inv = _unit_lower_inverse(a_kb)
    v_d = _pair_diag(v)
    solved = _mm(t_inv, jnp.concatenate([_pair_diag(kap), _pair_diag(_mm(a_kk, v_d, _BNN, P))], axis=2), _BNN, P)
    kap_p = solved[:, :, :LANES]
    v_p = solved[:, :, LANES:]
    er = lax.broadcasted_iota(jnp.int32, (LANES, LANES), 0)
    ec = lax.broadcasted_iota(jnp.int32, (LANES, LANES), 1)
    same_head = (er // C_HEAD_DIM) == (ec // C_HEAD_DIM)
    first = lax.broadcasted_iota(jnp.int32, (C_HEAD_DIM, LANES), 1) < C_HEAD_DIM
    with_bend = _mm(solved, bend, _BTN, P)
    pm = jnp.where(er == ec, wc, 0.0) - jnp.where(same_head, with_bend[:, :LANES], 0.0)
    fq = _mm(v, kend, _BTN, P) - with_bend[:, LANES:]
    q = jnp.where(first, fq[:, :C_HEAD_DIM], fq[:, C_HEAD_DIM:])
    via_rb = _mm(a_rb, jnp.concatenate([_pair_diag(kap_p), _pair_diag(v_p)], axis=2), _BNN, P)
    rp = rho - via_rb[:, :, :LANES]
    y0 = _mm(a_rk, v_d, _BNN, P) - via_rb[:, :, LANES:]

    s = state_ref[...]
    ys = []
    for ch in range(nc):
        sl = slice(ch * n, (ch + 1) * n)
        ys.append(_mm(rp[sl], _pair_diag(s), _BNT, P) + y0[sl])
        s = _mm(s, pm[sl], _BNN, P) + q[sl]
    state_ref[...] = s
    y = jnp.concatenate(ys, axis=0) if nc > 1 else ys[0]

    g_all = y.shape[0]
    seg = jnp.where(same_head, 1.0, 0.0).astype(BF16)
    head_mean = lambda z: (_mm_exact_rhs(z.reshape(g_all * c, LANES), seg, 2) * (1.0 / C_HEAD_DIM)
                           ).reshape(g_all, c, LANES)
    cen = y - head_mean(y)
    out = cen * lax.rsqrt(head_mean(cen * cen) + GN_EPS)
    bonus = paired(8)
    gate = paired(9)
    idx = 0
    for ch in range(nc):
        for b in range(bb):
            for p in range(pairs):
                sl = slice(p * LANES, (p + 1) * LANES)
                yn = out[idx] * gng_ref[:, sl] + gnb_ref[:, sl]
                o_ref[b, ch * c:(ch + 1) * c, sl] = (yn + bonus[idx]) * gate[idx]
                idx += 1

    @pl.when(ci == pl.num_programs(1) - 1)
    def _():
        for b in range(bb):
            for p in range(pairs):
                s_out_ref[b, 2 * p] = s[b * pairs + p][:, :C_HEAD_DIM]
                s_out_ref[b, 2 * p + 1] = s[b * pairs + p][:, C_HEAD_DIM:]


def _rwkv_chunks(feats, s0, gn_g, gn_b):
    bsz, t, _ = feats[0].shape
    c = RWKV_CHUNK
    pairs = C_HEADS // 2
    bb = RWKV_GROUP if bsz % RWKV_GROUP == 0 else 1
    nc = max(1, min(RWKV_GROUP // bb, t // c))
    seq = pl.BlockSpec((bb, nc * c, C_WIDTH), lambda b, i: (b, i, 0))
    st = pl.BlockSpec((bb, C_HEADS, C_HEAD_DIM, C_HEAD_DIM), lambda b, i: (b, 0, 0, 0))
    vec = pl.BlockSpec((1, C_WIDTH), lambda b, i: (0, 0))
    return pl.pallas_call(
        functools.partial(_rwkv_chunk_kernel, c=c, nc=nc),
        grid=(bsz // bb, t // (nc * c)),
        in_specs=[seq] * 8 + [st, vec, vec],
        out_specs=[seq, st],
        out_shape=[jax.ShapeDtypeStruct((bsz, t, C_WIDTH), F32),
                   jax.ShapeDtypeStruct((bsz, C_HEADS, C_HEAD_DIM, C_HEAD_DIM), F32)],
        scratch_shapes=[pltpu.VMEM((bb * pairs, C_HEAD_DIM, LANES), F32)],
        compiler_params=_cparams("parallel", "arbitrary"),
        name="rwkv_chunks",
    )(*feats, s0, gn_g, gn_b)


N_MERGE_REFS = 15


def _merge_rows(o0, l0, o1, l1, o2, l2, ob_ref, oc_ref, gate_ref, x_ref, wa_ref, wb_ref, wc_ref, wo_ref,
                gain_ref):
    ls = [l0[...], l1[...], l2[...]]
    mx = jnp.maximum(jnp.maximum(ls[0], ls[1]), ls[2])
    es = [jnp.exp(l - mx) for l in ls]
    den = es[0] + es[1] + es[2]
    o_a = (es[0] * o0[...] + es[1] * o1[...] + es[2] * o2[...]) / den
    merged = (gate_ref[:, 0:D_MODEL] * _mm(o_a, wa_ref[...])
              + gate_ref[:, D_MODEL:2 * D_MODEL] * _mm(ob_ref[...], wb_ref[...])
              + gate_ref[:, 2 * D_MODEL:] * _mm(oc_ref[...], wc_ref[...]))
    z = _mm(merged, wo_ref[...])
    return x_ref[...] + _rms(z, gain_ref[...])


def _gelu_tanh(x):
    return 0.5 * x * (1.0 + jnp.tanh(math.sqrt(2.0 / math.pi) * (x + 0.044715 * (x * x * x))))


FFN_PIECE = 128


def _ffn_kernel(*refs, tm, nb, cr):
    g1_ref, fu_ref, w_ref, b_ref, c_ref, fd_ref, g2_ref, out_ref, t_ref, ext_ref, f_ref = refs[N_MERGE_REFS:]

    @pl.when(pl.program_id(0) == 0)
    def _():
        ext_ref[0:cr, :] = c_ref[...]

    x1 = _merge_rows(*refs[:N_MERGE_REFS])
    h = _rms(x1, g1_ref[...]).astype(BF16)
    n_chunks = D_FF // FF_CHUNK
    split = (n_chunks + 1) // 2 * FF_CHUNK
    z = None
    for jc in range(n_chunks):
        pair = (slice(jc * FF_CHUNK, (jc + 1) * FF_CHUNK),
                slice(D_FF + jc * FF_CHUNK, D_FF + (jc + 1) * FF_CHUNK))
        for cols in pair:
            ext_ref[cr:cr + tm, cols] = _dg(h, fu_ref[:, cols], _NN)
        for r0 in range(0, tm, FFN_PIECE):
            cus = []
            for cols in pair:
                if nb % SUBLANES == 0:
                    taps = [ext_ref[r0 + cr - s * nb:r0 + cr - s * nb + FFN_PIECE, cols] for s in (2, 1, 0)]
                else:
                    blk = ext_ref[r0 + cr - SUBLANES:r0 + cr + FFN_PIECE, cols]
                    taps = [pltpu.roll(blk, s * nb, 0)[SUBLANES:] for s in (2, 1)] + [blk[SUBLANES:]]
                cus.append(w_ref[0:1, cols] * taps[0] + w_ref[1:2, cols] * taps[1]
                           + w_ref[2:3, cols] * taps[2] + b_ref[:, cols])
            f_ref[r0:r0 + FFN_PIECE, jc * FF_CHUNK:(jc + 1) * FF_CHUNK] = (
                _gelu_tanh(cus[0]) * cus[1]).astype(BF16)
        if (jc + 1) * FF_CHUNK == split:
            z = _dg(f_ref[:, 0:split], fd_ref[0:split, :], _NN)
    z = z + _dg(f_ref[:, split:], fd_ref[split:, :], _NN)
    tail = ext_ref[tm:tm + cr, :]
    t_ref[...] = tail
    ext_ref[0:cr, :] = tail
    out_ref[...] = x1 + _rms(z, g2_ref[...])


def _merge_ffn(att, o_b, o_c, gates, x, ctx, p, nb):
    m = x.shape[0]
    tm = min(m, 256)
    cr = ctx.shape[0]
    row = lambda i: (i, 0)
    fixed = lambda i: (0, 0)
    rows = lambda w: pl.BlockSpec((tm, w), row)
    once = lambda shape: pl.BlockSpec(shape, fixed, pipeline_mode=pl.Buffered(1))
    resident = lambda name: _layer_weight(p[name], p['layer'], pipeline_mode=pl.Buffered(1))
    merge_ws = ('w_br_a', 'w_br_b', 'w_br_c', 'w_out')
    x2, tail = pl.pallas_call(
        functools.partial(_ffn_kernel, tm=tm, nb=nb, cr=cr),
        grid=(m // tm,),
        in_specs=[rows(LANES)] * 6 + [rows(B_WIDTH), rows(C_WIDTH), rows(IN_GATE), rows(D_MODEL)]
                 + [resident(name) for name in merge_ws] + [once((1, D_MODEL))]
                 + [once((1, D_MODEL)),
                  _layer_weight(p['f_up'], p['layer'], pipeline_mode=pl.Buffered(1)),
                  once((F_CONV, 2 * D_FF)), once((1, 2 * D_FF)), once((cr, 2 * D_FF)),
                  _layer_weight(p['f_down'], p['layer'], pipeline_mode=pl.Buffered(1)), once((1, D_MODEL))],
        out_specs=[pl.BlockSpec((tm, D_MODEL), row), pl.BlockSpec((cr, 2 * D_FF), row)],
        out_shape=[jax.ShapeDtypeStruct((m, D_MODEL), F32),
                   jax.ShapeDtypeStruct(((m // tm) * cr, 2 * D_FF), F32)],
        scratch_shapes=[pltpu.VMEM((cr + tm, 2 * D_FF), F32), pltpu.VMEM((tm, D_FF), BF16)],
        compiler_params=_cparams("arbitrary"),
        name="merge_ffn",
    )(*att, o_b, o_c, gates, x, *[p[name] for name in merge_ws], p['norm_mix_post'],
      p['norm_ffn_pre'], p['f_up'], p['f_dw_w'], p['f_dw_b'], ctx, p['f_down'], p['norm_ffn_post'])
    return x2, tail[-cr:]


def _to_time_major(a):
    a = jnp.swapaxes(a, 0, 1)
    return a.reshape((a.shape[0] * a.shape[1],) + a.shape[2:])


def _to_batch_major(a, ts):
    return jnp.swapaxes(a.reshape(ts, a.shape[0] // ts, a.shape[1]), 0, 1)


def _layer_prompt(x, tables, p):
    t = x.shape[0]
    qkv, glu, pc, gates, o_b, feats = _in_proj(x, tables, jnp.zeros((4 * SUBLANES, B_WIDTH), F32),
                                               jnp.zeros((SUBLANES, IN_C), F32), p, 1)
    att = []
    for g in range(N_A_GROUPS):
        att.extend(_attn_prompt(qkv, g))
    s0 = jnp.zeros((1, C_HEADS, C_HEAD_DIM, C_HEAD_DIM), F32)
    o_c, s_new = _rwkv_chunks([f[None] for f in feats], s0, p['c_gn_g'], p['c_gn_b'])
    x2, tail = _merge_ffn(att, o_b, o_c[0], gates, x, jnp.zeros((SUBLANES, 2 * D_FF), F32), p, 1)
    kv = [qkv[t - min(w, t):, 3 * A_WIDTH * g + A_WIDTH:3 * A_WIDTH * (g + 1)]
          .reshape(1, min(w, t), 2, A_HEADS, A_HEAD_DIM) for g, (w, _) in enumerate(A_GROUPS)]
    f_tail = tail[-(F_CONV - 1):][None]
    return x2, kv, glu[t - (B_CONV - 1):][None], pc[t - 1:], s_new, f_tail


def _layer_sample(x, tables, caches_t, layer, b_ctx, c_shift, c_state, f_ctx, p, ts):
    nb = x.shape[0] // ts
    qkv, glu, pc, gates, o_b, feats = _in_proj(x, tables, _to_time_major(b_ctx), c_shift, p, nb)
    qkv_bm = _to_batch_major(qkv, ts)
    qkv_pad = jnp.pad(qkv_bm, ((0, 0), (0, SUBLANES - ts), (0, 0)))
    o, lse = _attn_sample(qkv_pad, caches_t, layer, ts)
    att = []
    for g in range(N_A_GROUPS):
        att.append(_to_time_major(o[:, :ts, g * LANES:(g + 1) * LANES]))
        att.append(_to_time_major(lse[:, :ts, g * LANES:(g + 1) * LANES]))
    feats_bm = [jnp.pad(_to_batch_major(f, ts), ((0, 0), (0, RWKV_CHUNK - ts), (0, 0))) for f in feats]
    o_c, s_new = _rwkv_chunks(feats_bm, c_state, p['c_gn_g'], p['c_gn_b'])
    x2, tail = _merge_ffn(att, o_b, _to_time_major(o_c[:, :ts]), gates, x, _to_time_major(f_ctx), p, nb)
    kv = [_to_batch_major(qkv[:, 3 * A_WIDTH * g + A_WIDTH:3 * A_WIDTH * (g + 1)], ts)
          .reshape(nb, ts, 2, A_HEADS, A_HEAD_DIM) for g in range(N_A_GROUPS)]
    b_new = jnp.concatenate([b_ctx, _to_batch_major(glu, ts)], axis=1)[:, -(B_CONV - 1):]
    f_new = jnp.concatenate([f_ctx, _to_batch_major(tail, F_CONV - 1)], axis=1)
    return x2, kv, b_new, pc[(ts - 1) * nb:], s_new, f_new[:, -(F_CONV - 1):]


def kernel(x_prompt, x_sample, cache_a_kv0, cache_a_kv1, cache_a_kv2, state_b_conv, state_c_shift, state_c_wkv, state_f_conv, norm_mix_pre, norm_mix_post, norm_ffn_pre, norm_ffn_post, w_in, b_dw_w, b_dw_b, b_ln_g, b_ln_b, c_mu, c_w0, c_w2, c_a0, c_a2, c_g2, c_kk, c_ka, c_rk, c_gn_g, c_gn_b, w_br_a, w_br_b, w_br_c, w_out, f_up, f_dw_w, f_dw_b, f_down):
    depth = w_in.shape[0]
    bp, tp, _ = x_prompt.shape
    bs, ts, _ = x_sample.shape
    assert bp == 1 and ts <= SUBLANES
    caches_t = [jnp.transpose(c, (0, 1, 3, 4, 5, 2)) for c in (cache_a_kv0, cache_a_kv1, cache_a_kv2)]

    head_id = jnp.arange(C_WIDTH) // C_HEAD_DIM
    seg = (head_id[:, None] == head_id[None, :]).astype(BF16)
    zpad = jnp.zeros((LANES - C_DECAY_RANK, C_WIDTH), F32)

    tab_p = _rope_tables(tp, 0, 1)
    tab_s = _rope_tables(ts * bs, PAST_LEN, bs)

    stacked = {name: w.astype(BF16) for name, w in (
        ('w_in', w_in), ('w_br_a', w_br_a), ('w_br_b', w_br_b), ('w_br_c', w_br_c), ('w_out', w_out),
        ('f_up', f_up), ('f_down', f_down))}

    y_p = x_prompt[0]
    y_s = _to_time_major(x_sample)
    outs_p, outs_s = [], []
    for l in range(depth):
        vec = lambda a: a[l][None, :]
        p = {
            **stacked, 'layer': l,
            'norm_mix_pre': vec(norm_mix_pre), 'norm_mix_post': vec(norm_mix_post),
            'norm_ffn_pre': vec(norm_ffn_pre), 'norm_ffn_post': vec(norm_ffn_post),
            'b_dw_w': b_dw_w[l], 'b_dw_b': vec(b_dw_b), 'b_ln_g': vec(b_ln_g), 'b_ln_b': vec(b_ln_b),
            'c_mu': vec(c_mu), 'c_w0': vec(c_w0), 'c_a0': vec(c_a0),
            'c_w2p': jnp.concatenate([c_w2[l], zpad], axis=0),
            'c_a2p': jnp.concatenate([zpad, c_a2[l]], axis=0),
            'c_g2': c_g2[l], 'c_kk': vec(c_kk), 'c_ka': vec(c_ka),
            'c_rk': c_rk[l].reshape(1, C_WIDTH), 'c_gn_g': vec(c_gn_g), 'c_gn_b': vec(c_gn_b),
            'seg': seg,
            'f_dw_w': f_dw_w[l], 'f_dw_b': vec(f_dw_b),
        }
        y_p, *rest_p = _layer_prompt(y_p, tab_p, p)
        outs_p.append(rest_p)
        y_s, *rest_s = _layer_sample(y_s, tab_s, caches_t, l, state_b_conv[l], state_c_shift[l],
                                     state_c_wkv[l], state_f_conv[l], p, ts)
        outs_s.append(rest_s)

    stack = lambda outs, f: jnp.stack([f(o) for o in outs])
    res = [y_p[None], _to_batch_major(y_s, ts)]
    res += [stack(outs_p, lambda o, g=g: o[0][g]) for g in range(N_A_GROUPS)]
    res += [stack(outs_s, lambda o, g=g: o[0][g]) for g in range(N_A_GROUPS)]
    for idx in range(1, 5):
        res += [stack(outs_p, lambda o: o[idx]), stack(outs_s, lambda o: o[idx])]
    return tuple(res)
```

```python
import functools
import math

import jax
import jax.numpy as jnp
from jax import lax
from jax.experimental import pallas as pl
from jax.experimental.pallas import tpu as pltpu

F32 = jnp.float32
BF16 = jnp.bfloat16

D_MODEL = 1024
PAST_LEN = 16384
A_GROUPS = ((128, 1), (512, 4), (2048, 16))
N_A_GROUPS = len(A_GROUPS)
A_HEAD_DIM = 32
A_HEADS = 4
A_WIDTH = A_HEADS * A_HEAD_DIM
ROPE_THETA = 10000.0
ATT_BLOCK = 128
B_WIDTH = 256
B_CONV = 31
C_HEAD_DIM = 64
C_WIDTH = 384
C_HEADS = 6
C_DECAY_RANK = 64
C_ICLR_RANK = 64
C_GATE_RANK = 128
DECAY_SCALE = math.exp(-0.5)
D_FF = 2816
F_CONV = 3
IN_A = N_A_GROUPS * 3 * A_WIDTH
IN_B = 2 * B_WIDTH
IN_C = 3 * C_WIDTH + C_DECAY_RANK + C_ICLR_RANK + C_GATE_RANK
IN_GATE = 3 * D_MODEL
IN_WIDTH = IN_A + IN_B + IN_C + IN_GATE
RMS_EPS = 1e-6
LN_EPS = 1e-5
GN_EPS = C_HEAD_DIM * 1e-5
NEG_BIG = -1e30

LANES = 128
SUBLANES = 8
VMEM_LIMIT = 56 * 1024 * 1024

RWKV_CHUNK = 64
FF_CHUNK = 256


def _cparams(*sem):
    return pltpu.CompilerParams(dimension_semantics=sem, vmem_limit_bytes=VMEM_LIMIT)


def _split_bf16(x, n):
    pieces = []
    rem = x
    for _ in range(n):
        p = rem.astype(BF16)
        pieces.append(p)
        rem = rem - p.astype(F32)
    return pieces


_NN = (((1,), (0,)), ((), ()))
_NT = (((1,), (1,)), ((), ()))


def _dg(a, b, dims):
    return lax.dot_general(a, b, dims, preferred_element_type=F32)


def _mm(a, b, dims=_NN, passes=1):
    if passes == 1:
        return _dg(a.astype(BF16), b.astype(BF16), dims)
    a_hi, a_lo = _split_bf16(a, 2)
    b_hi, b_lo = _split_bf16(b, 2)
    return _dg(a_hi, b_hi, dims) + (_dg(a_hi, b_lo, dims) + _dg(a_lo, b_hi, dims))


def _mm_exact_rhs(a, b_bf16, pieces=3):
    out = None
    for p in _split_bf16(a, pieces):
        t = _dg(p, b_bf16, _NN)
        out = t if out is None else out + t
    return out


def _mm_exact_rhs_left(a_bf16, b, pieces=3):
    out = None
    for p in _split_bf16(b, pieces):
        t = _dg(a_bf16, p, _NN)
        out = t if out is None else out + t
    return out


def _rms(x, g):
    ms = jnp.mean(x * x, axis=-1, keepdims=True)
    return x * lax.rsqrt(ms + RMS_EPS) * g


def _sigmoid(x):
    return 1.0 / (1.0 + jnp.exp(-x))


def _rope_table_kernel(inv_ref, cos_ref, sin_ref, cl_ref, sl_ref, *, tm, pos0, nb):
    inv = inv_ref[...]

    @pl.when(pl.program_id(0) == 0)
    def _():
        within = (lax.broadcasted_iota(jnp.int32, (tm, LANES), 0) // nb).astype(F32) * inv
        cl_ref[...] = jnp.cos(within)
        sl_ref[...] = jnp.sin(within)

    base = (pos0 + pl.program_id(0) * (tm // nb)).astype(F32) * jnp.broadcast_to(inv, (SUBLANES, LANES))
    ch = jnp.cos(base)[0:1, :]
    sh = jnp.sin(base)[0:1, :]
    lane = lax.broadcasted_iota(jnp.int32, (tm, LANES), 1)
    sign = jnp.where((lane % A_HEAD_DIM) < A_HEAD_DIM // 2, -1.0, 1.0)
    cos_ref[...] = ch * cl_ref[...] - sh * sl_ref[...]
    sin_ref[...] = (sh * cl_ref[...] + ch * sl_ref[...]) * sign


def _rope_tables(m, pos0, nb):
    half = A_HEAD_DIM // 2
    inv = ROPE_THETA ** (-(jnp.arange(half, dtype=F32) * 2.0 / A_HEAD_DIM))
    inv_lane = jnp.tile(inv, 2 * A_HEADS)[None, :]
    tm = min(m, 512)
    assert tm % nb == 0
    return pl.pallas_call(
        functools.partial(_rope_table_kernel, tm=tm, pos0=pos0, nb=nb),
        grid=(m // tm,),
        in_specs=[pl.BlockSpec((1, LANES), lambda i: (0, 0))],
        out_specs=[pl.BlockSpec((tm, LANES), lambda i: (i, 0))] * 2,
        out_shape=[jax.ShapeDtypeStruct((m, LANES), F32)] * 2,
        scratch_shapes=[pltpu.VMEM((tm, LANES), F32)] * 2,
        compiler_params=_cparams("arbitrary"),
        name="rope_tables",
    )(inv_lane)


N_C_PARAMS = 10
N_C_FEATS = 8


def _in_proj_kernel(x_ref, g_ref, w_ref, cos_ref, sin_ref, bctx_ref, bw_ref, bb_ref, bg_ref, bbeta_ref, cctx_ref,
                    *rest, tm, nb, bctx, coff, carry):
    c_params = rest[:N_C_PARAMS]
    qkv_ref, glu_ref, pc_ref, gate_ref, ob_ref = rest[N_C_PARAMS:N_C_PARAMS + 5]
    feat_ref = rest[N_C_PARAMS + 5]
    c_outs = [feat_ref.at[:, j * C_WIDTH:(j + 1) * C_WIDTH] for j in range(N_C_FEATS)]
    extb_ref, extc_ref = rest[N_C_PARAMS + 6:N_C_PARAMS + 8]
    shift_ref = rest[N_C_PARAMS + 8] if len(rest) > N_C_PARAMS + 8 else None

    @pl.when(pl.program_id(0) == 0)
    def _():
        extb_ref[0:bctx, :] = bctx_ref[...]
        extc_ref[0:coff, :] = cctx_ref[...]

    h = _rms(x_ref[...], g_ref[...]).astype(BF16)
    cos = cos_ref[...]
    sin = sin_ref[...]
    lane = lax.broadcasted_iota(jnp.int32, cos.shape, 1)
    first_half = (lane % A_HEAD_DIM) < A_HEAD_DIM // 2
    def branch_gate(j):
        cols = slice(j * D_MODEL, (j + 1) * D_MODEL)
        gate_ref[:, cols] = _sigmoid(_dg(h, w_ref[:, IN_A + IN_B + IN_C + j * D_MODEL:
                                                  IN_A + IN_B + IN_C + (j + 1) * D_MODEL], _NN)).astype(gate_ref.dtype)

    pb = _dg(h, w_ref[:, IN_A:IN_A + IN_B], _NN)
    glu = pb[:, :B_WIDTH] * _sigmoid(pb[:, B_WIDTH:])
    glu_ref[...] = glu
    extb_ref[bctx:bctx + tm, :] = glu
    branch_gate(0)
    _conv_b_body(extb_ref, bw_ref, bb_ref, bg_ref, bbeta_ref, ob_ref, tm=tm, nb=nb,
                 base=bctx - (B_CONV - 1) * nb, shift_ref=shift_ref)
    pc = _dg(h, w_ref[:, IN_A + IN_B:IN_A + IN_B + IN_C], _NN)
    pc_ref[...] = pc
    extc_ref[coff:coff + tm, :] = pc
    branch_gate(1)
    _rwkv_feature_rows(pc, extc_ref[coff - nb:coff - nb + tm, :], *c_params, *c_outs)
    pa = _dg(h, w_ref[:, 0:IN_A], _NN)
    branch_gate(2)
    for blk in range(IN_A // LANES):
        y = pa[:, blk * LANES:(blk + 1) * LANES]
        if blk % 3 != 2:
            swapped = jnp.where(first_half, pltpu.roll(y, LANES - A_HEAD_DIM // 2, 1),
                                pltpu.roll(y, A_HEAD_DIM // 2, 1))
            y = y * cos + swapped * sin
        qkv_ref[:, blk * LANES:(blk + 1) * LANES] = y
    if carry:
        extb_ref[0:bctx, :] = extb_ref[tm:tm + bctx, :]
        extc_ref[0:coff, :] = extc_ref[tm:tm + coff, :]


def _layer_weight(w, layer, **kw):
    return pl.BlockSpec((None,) + w.shape[1:], lambda *_: (layer, 0, 0), **kw)


def _in_proj(x, tables, b_ctx, c_ctx, p, nb):
    m = x.shape[0]
    tm = min(m, 256)
    bctx, coff = b_ctx.shape[0], c_ctx.shape[0]
    row = lambda i: (i, 0)
    fixed = lambda i: (0, 0)
    full = lambda a: pl.BlockSpec(a.shape, fixed)
    widths = (IN_A, B_WIDTH, IN_C, IN_GATE, B_WIDTH, N_C_FEATS * C_WIDTH)
    b_params = (b_ctx, p['b_dw_w'], p['b_dw_b'], p['b_ln_g'], p['b_ln_b'])
    c_params = (c_ctx, p['c_mu'], p['c_w2p'], p['c_a2p'], p['c_g2'], p['c_w0'], p['c_a0'], p['c_kk'], p['c_ka'],
                p['c_rk'], p['seg'])
    scratch = [pltpu.VMEM((bctx + tm, B_WIDTH), F32), pltpu.VMEM((coff + tm, IN_C), F32)]
    if nb == 1:
        scratch.append(pltpu.VMEM((SUBLANES, bctx + tm - SUBLANES, B_WIDTH), F32))
    outs = pl.pallas_call(
        functools.partial(_in_proj_kernel, tm=tm, nb=nb, bctx=bctx, coff=coff, carry=m > tm),
        grid=(m // tm,),
        in_specs=[pl.BlockSpec((tm, D_MODEL), row), pl.BlockSpec((1, D_MODEL), fixed),
                  _layer_weight(p['w_in'], p['layer'], pipeline_mode=pl.Buffered(1)),
                  pl.BlockSpec((tm, LANES), row), pl.BlockSpec((tm, LANES), row)]
                 + [full(a) for a in b_params + c_params],
        out_specs=[pl.BlockSpec((tm, w), row) for w in widths],
        out_shape=[jax.ShapeDtypeStruct((m, w), BF16 if j == 3 else F32) for j, w in enumerate(widths)],
        scratch_shapes=scratch,
        compiler_params=_cparams("arbitrary"),
        name="in_proj",
    )(x, p['norm_mix_pre'], p['w_in'], *tables, *b_params, *c_params)
    return outs


def _softmax_heads(q, k_bf16, v_bf16, valid):
    rows = q.shape[0]
    lane_head = lax.broadcasted_iota(jnp.int32, q.shape, 1) // A_HEAD_DIM
    scale = 1.0 / math.sqrt(A_HEAD_DIM)
    qs = q * scale
    q_heads = jnp.concatenate([jnp.where(lane_head == h, qs, 0.0) for h in range(A_HEADS)], axis=0)
    s_all = _dg(q_heads.astype(BF16), k_bf16, _NT)
    ps, ls, lses = [], [], []
    for h in range(A_HEADS):
        s = jnp.where(valid, s_all[h * rows:(h + 1) * rows], NEG_BIG)
        mx = jnp.max(s, axis=-1, keepdims=True)
        p = jnp.exp(s - mx)
        l = jnp.sum(p, axis=-1, keepdims=True)
        ps.append(p.astype(BF16))
        ls.append(l)
        lses.append(mx + jnp.log(l))
    o_heads = _dg(jnp.concatenate(ps, axis=0), v_bf16, _NN)
    o_all = jnp.zeros(q.shape, F32)
    lse_all = jnp.zeros(q.shape, F32)
    for h in range(A_HEADS):
        hm = lane_head == h
        o_all = jnp.where(hm, o_heads[h * rows:(h + 1) * rows] / ls[h], o_all)
        lse_all = jnp.where(hm, lses[h], lse_all)
    return o_all, lse_all


ATT_UNROLL = 4


def _attn_prompt_kernel(q_ref, kh_ref, kc_ref, vh_ref, vc_ref, o_ref, lse_ref, kx_ref, vx_ref, *, d, nsub):
    i = pl.program_id(0)
    hb = d * ATT_BLOCK
    kx_ref[0:hb, :] = kh_ref[...]
    kx_ref[hb:, :] = kc_ref[...]
    vx_ref[0:hb, :] = vh_ref[...]
    vx_ref[hb:, :] = vc_ref[...]
    qi = lax.broadcasted_iota(jnp.int32, (ATT_BLOCK, 2 * ATT_BLOCK), 0)
    kj = lax.broadcasted_iota(jnp.int32, (ATT_BLOCK, 2 * ATT_BLOCK), 1)
    delta = qi + ATT_BLOCK - kj
    band = (delta >= 0) & (delta <= ATT_BLOCK)
    current = kj >= ATT_BLOCK

    def problem(it, carry):
        j = it // d
        start = j * hb + it % d
        valid = band & (current | (i > 0) | (j > 0))
        keys = pl.ds(start, 2 * ATT_BLOCK, stride=d)
        rows = pl.ds(start, ATT_BLOCK, stride=d)
        o, lse = _softmax_heads(q_ref[rows, :], kx_ref[keys, :].astype(BF16), vx_ref[keys, :].astype(BF16), valid)
        o_ref[rows, :] = o
        lse_ref[rows, :] = lse
        return carry

    lax.fori_loop(0, nsub * d, problem, 0, unroll=ATT_UNROLL)


def _attn_prompt(qkv, group):
    t = qkv.shape[0]
    _, d = A_GROUPS[group]
    hb = d * ATT_BLOCK
    nsub = max(1, 8 // d)
    while t % (nsub * hb):
        nsub //= 2
    col = 3 * group
    blk = (nsub * hb, LANES)
    cur = lambda off: (lambda i: (i, col + off))
    halo = lambda off: (lambda i: (jnp.maximum(i * nsub - 1, 0), col + off))
    return pl.pallas_call(
        functools.partial(_attn_prompt_kernel, d=d, nsub=nsub),
        grid=(t // (nsub * hb),),
        in_specs=[pl.BlockSpec(blk, cur(0)), pl.BlockSpec((hb, LANES), halo(1)), pl.BlockSpec(blk, cur(1)),
                  pl.BlockSpec((hb, LANES), halo(2)), pl.BlockSpec(blk, cur(2))],
        out_specs=[pl.BlockSpec(blk, lambda i: (i, 0))] * 2,
        out_shape=[jax.ShapeDtypeStruct((t, LANES), F32)] * 2,
        scratch_shapes=[pltpu.VMEM(((nsub + 1) * hb, LANES), F32)] * 2,
        compiler_params=_cparams("parallel"),
        name=f"attn_prompt_g{group}",
    )(qkv, qkv, qkv, qkv, qkv)


def _attn_sample_kernel(qkv_ref, c0_ref, c1_ref, c2_ref, o_ref, lse_ref, *, ts):
    for bi in range(qkv_ref.shape[0]):
        _attn_sample_one(bi, qkv_ref, c0_ref, c1_ref, c2_ref, o_ref, lse_ref, ts)


def _attn_sample_one(bi, qkv_ref, c0_ref, c1_ref, c2_ref, o_ref, lse_ref, ts):
    qkv = qkv_ref[bi]
    rows = qkv.shape[0]
    pad = jnp.zeros((LANES - rows, LANES), F32)
    for g, (window, d) in enumerate(A_GROUPS):
        base = 3 * A_WIDTH * g
        q = qkv[:, base:base + LANES]
        k_new = jnp.concatenate([qkv[:, base + LANES:base + 2 * LANES], pad], axis=0)
        v_new = jnp.concatenate([qkv[:, base + 2 * LANES:base + 3 * LANES], pad], axis=0)
        c_ref = (c0_ref, c1_ref, c2_ref)[g]
        k_t = c_ref[bi, 0].reshape(A_WIDTH, window).astype(BF16)
        v_t = c_ref[bi, 1].reshape(A_WIDTH, window).astype(BF16)
        qi = lax.broadcasted_iota(jnp.int32, (rows, window), 0)
        c = lax.broadcasted_iota(jnp.int32, (rows, window), 1)
        qn = lax.broadcasted_iota(jnp.int32, (rows, LANES), 0)
        cn = lax.broadcasted_iota(jnp.int32, (rows, LANES), 1)
        if d == 1:
            valid_c, valid_n = c >= qi, cn <= qn
        else:
            valid_c, valid_n = (c % d) == qi, cn == qn
        lane_head = lax.broadcasted_iota(jnp.int32, q.shape, 1) // A_HEAD_DIM
        scale = 1.0 / math.sqrt(A_HEAD_DIM)
        qs = q * scale
        q_heads = jnp.concatenate([jnp.where(lane_head == h, qs, 0.0) for h in range(A_HEADS)],
                                  axis=0).astype(BF16)
        sc_all = _dg(q_heads, k_t, _NN)
        sn_all = _dg(q_heads, k_new.astype(BF16), _NT)
        pcs, pns, ls, lses = [], [], [], []
        for h in range(A_HEADS):
            s_c = jnp.where(valid_c, sc_all[h * rows:(h + 1) * rows], NEG_BIG)
            s_n = jnp.where(valid_n, sn_all[h * rows:(h + 1) * rows], NEG_BIG)
            mx = jnp.maximum(jnp.max(s_c, axis=-1, keepdims=True), jnp.max(s_n, axis=-1, keepdims=True))
            pcs.append(jnp.exp(s_c - mx))
            pns.append(jnp.exp(s_n - mx))
            ls.append(jnp.sum(pcs[-1], axis=-1, keepdims=True) + jnp.sum(pns[-1], axis=-1, keepdims=True))
            lses.append(mx + jnp.log(ls[-1]))
        o_heads = (_dg(jnp.concatenate(pcs, axis=0).astype(BF16), v_t, _NT)
                   + _dg(jnp.concatenate(pns, axis=0).astype(BF16), v_new.astype(BF16), _NN))
        o_all = jnp.zeros(q.shape, F32)
        lse_all = jnp.zeros(q.shape, F32)
        for h in range(A_HEADS):
            hm = lane_head == h
            o_all = jnp.where(hm, o_heads[h * rows:(h + 1) * rows] / ls[h], o_all)
            lse_all = jnp.where(hm, lses[h], lse_all)
        o_ref[bi, :, g * LANES:(g + 1) * LANES] = o_all
        lse_ref[bi, :, g * LANES:(g + 1) * LANES] = lse_all


def _attn_sample(qkv_bm, caches_t, layer, ts):
    bsz, rows, _ = qkv_bm.shape
    bb = 4 if bsz % 4 == 0 else 1
    specs = [pl.BlockSpec((bb, rows, IN_A), lambda b: (b, 0, 0))]
    for (window, d), c in zip(A_GROUPS, caches_t):
        assert c.shape[-1] == window and window % d == 0 and (d == 1 or d >= ts)
        specs.append(pl.BlockSpec((None, bb, 2, A_HEADS, A_HEAD_DIM, window),
                                  lambda b: (layer, b, 0, 0, 0, 0)))
    out = jax.ShapeDtypeStruct((bsz, rows, N_A_GROUPS * LANES), F32)
    return pl.pallas_call(
        functools.partial(_attn_sample_kernel, ts=ts),
        grid=(bsz // bb,),
        in_specs=specs,
        out_specs=[pl.BlockSpec((bb, rows, N_A_GROUPS * LANES), lambda b: (b, 0, 0))] * 2,
        out_shape=[out, out],
        compiler_params=_cparams("parallel"),
        name="attn_sample",
    )(qkv_bm, *caches_t)


def _conv_b_body(ext_ref, w_ref, b_ref, g_ref, beta_ref, o_ref, *, tm, nb, base, shift_ref=None):
    acc = jnp.zeros((tm, B_WIDTH), F32) + b_ref[...]
    if shift_ref is not None:
        n = shift_ref.shape[1]
        for s in range(1, SUBLANES):
            shift_ref[s] = ext_ref[s:s + n, :]
    for j in range(B_CONV):
        off = base + j * nb
        if shift_ref is None or off % SUBLANES == 0:
            tap = ext_ref[pl.ds(off, tm), :]
        else:
            tap = shift_ref[off % SUBLANES, pl.ds(off - off % SUBLANES, tm), :]
        acc = acc + w_ref[j:j + 1, :] * tap
    mu = jnp.mean(acc, axis=-1, keepdims=True)
    cen = acc - mu
    var = jnp.mean(cen * cen, axis=-1, keepdims=True)
    y = cen * lax.rsqrt(var + LN_EPS) * g_ref[...] + beta_ref[...]
    o_ref[...] = y * _sigmoid(y)


def _rwkv_feature_rows(pc, prev, mu_ref, w2_ref, a2_ref, g2_ref, w0_ref, a0_ref, kkw_ref, ka_ref, rk_ref, seg_ref,
                       r_o, lw_o, k_o, v_o, kk_o, b_o, g_o, bonus_o):
    xs = pc + mu_ref[...] * (prev - pc)
    r = xs[:, 0:C_WIDTH]
    k = xs[:, C_WIDTH:2 * C_WIDTH]
    v = xs[:, 2 * C_WIDTH:3 * C_WIDTH]
    wa = xs[:, 3 * C_WIDTH:3 * C_WIDTH + LANES]
    gl = xs[:, 3 * C_WIDTH + LANES:]
    seg = seg_ref[...]
    lw = -DECAY_SCALE * _sigmoid(w0_ref[...] + _mm(jnp.tanh(wa), w2_ref[...]))
    a = _sigmoid(a0_ref[...] + _mm(wa, a2_ref[...]))
    g = _mm(_sigmoid(gl), g2_ref[...])
    kk = k * kkw_ref[...]
    ss = _mm_exact_rhs(kk * kk, seg, 2)
    kk = kk * lax.rsqrt(jnp.maximum(ss, 1e-24))
    k2 = k * (1.0 + (a - 1.0) * ka_ref[...])
    r_o[...] = r
    lw_o[...] = lw
    k_o[...] = k2
    v_o[...] = v
    kk_o[...] = kk
    b_o[...] = kk * a
    g_o[...] = g
    bonus_o[...] = _mm_exact_rhs(r * k2 * rk_ref[...], seg, 2) * v


RWKV_PASSES = 1
RWKV_GROUP = 8


_BNN = (((2,), (1,)), ((0,), (0,)))
_BNT = (((2,), (2,)), ((0,), (0,)))
_BTN = (((1,), (1,)), ((0,), (0,)))


def _pair_diag(x):
    first = lax.broadcasted_iota(jnp.int32, x.shape[1:], 1) < x.shape[-1] // 2
    return jnp.concatenate([jnp.where(first, x, 0.0), jnp.where(first, 0.0, x)], axis=1)


def _unit_lower_inverse(a_strict):
    g, c, _ = a_strict.shape
    ri = lax.broadcasted_iota(jnp.int32, (c, 2 * c), 0)
    ci = lax.broadcasted_iota(jnp.int32, (c, 2 * c), 1) % c
    inv = jnp.where(ri == ci, 1.0, 0.0) - jnp.where(((ri // 2) == (ci // 2)) & (ri > ci), a_strict, 0.0)
    s = 2
    while s < c:
        same = (ri // (2 * s)) == (ci // (2 * s))
        off = jnp.where(same & ((ri % (2 * s)) >= s) & ((ci % (2 * s)) < s), a_strict, 0.0)
        step = _mm(off, _pair_diag(inv), _BNN, RWKV_PASSES)
        inv = inv - _mm(inv, _pair_diag(step), _BNN, RWKV_PASSES)
        s *= 2
    return inv


def _rwkv_chunk_kernel(r_ref, lw_ref, k_ref, v_ref, kk_ref, b_ref, g_ref, bonus_ref, s0_ref, gng_ref, gnb_ref,
                       o_ref, s_out_ref, state_ref, *, c, nc):
    assert 2 * c == LANES and 2 * C_HEAD_DIM == LANES
    bb = r_ref.shape[0]
    pairs = C_HEADS // 2
    n = bb * pairs
    ci = pl.program_id(1)

    @pl.when(ci == 0)
    def _():
        for b in range(bb):
            for p in range(pairs):
                state_ref[b * pairs + p] = jnp.concatenate([s0_ref[b, 2 * p], s0_ref[b, 2 * p + 1]], axis=-1)

    ri = lax.broadcasted_iota(jnp.int32, (c, LANES), 0)
    cj = lax.broadcasted_iota(jnp.int32, (c, LANES), 1) % c
    incl = ri >= cj
    strict = ri > cj
    tri = jnp.where(lax.broadcasted_iota(jnp.int32, (c, c), 0) >= lax.broadcasted_iota(jnp.int32, (c, c), 1),
                    1.0, 0.0).astype(BF16)

    groups = []
    for ch in range(nc):
        rows = slice(ch * c, (ch + 1) * c)
        for b in range(bb):
            lw = lw_ref[b, rows, :]
            cum = _mm_exact_rhs_left(tri, lw)
            cend = cum[c - 1:c, :]
            w_inv = jnp.exp(-cum)
            w_end = jnp.exp(cend - cum)
            kvec = k_ref[b, rows, :]
            bvec = b_ref[b, rows, :]
            groups.append((kk_ref[b, rows, :] * jnp.exp(cum - lw), r_ref[b, rows, :] * jnp.exp(cum),
                           bvec * w_inv, kvec * w_inv, bvec * w_end, kvec * w_end, v_ref[b, rows, :],
                           jnp.exp(cend), bonus_ref[b, rows, :], g_ref[b, rows, :]))

    def paired(idx):
        return jnp.stack([grp[idx][:, p * LANES:(p + 1) * LANES] for grp in groups for p in range(pairs)])

    P = RWKV_PASSES
    kap, rho, bt, kt, bend, kend, v, wc = [paired(i) for i in range(8)]
    lhs2 = jnp.concatenate([kap, rho], axis=1)
    x2 = _mm(lhs2, jnp.concatenate([_pair_diag(bt), _pair_diag(kt)], axis=1), _BNT, P)
    a_kb = jnp.where(strict, x2[:, :c, :LANES], 0.0)
    a_rb = jnp.where(incl, x2[:, c:, :LANES], 0.0)
    a_kk = jnp.where(strict, x2[:, :c, LANES:], 0.0)
    a_rk = jnp.where(incl, x2[:, c:, LANES:], 0.0)
    t_inv = _unit_lower_inverse(a_kb)
    v_d = _pair_diag(v)
    solved = _mm(t_inv, jnp.concatenate([_pair_diag(kap), _pair_diag(_mm(a_kk, v_d, _BNN, P))], axis=2), _BNN, P)
    kap_p = solved[:, :, :LANES]
    v_p = solved[:, :, LANES:]
    er = lax.broadcasted_iota(jnp.int32, (LANES, LANES), 0)
    ec = lax.broadcasted_iota(jnp.int32, (LANES, LANES), 1)
    same_head = (er // C_HEAD_DIM) == (ec // C_HEAD_DIM)
    first = lax.broadcasted_iota(jnp.int32, (C_HEAD_DIM, LANES), 1) < C_HEAD_DIM
    with_bend = _mm(solved, bend, _BTN, P)
    pm = jnp.where(er == ec, wc, 0.0) - jnp.where(same_head, with_bend[:, :LANES], 0.0)
    fq = _mm(v, kend, _BTN, P) - with_bend[:, LANES:]
    q = jnp.where(first, fq[:, :C_HEAD_DIM], fq[:, C_HEAD_DIM:])
    via_rb = _mm(a_rb, jnp.concatenate([_pair_diag(kap_p), _pair_diag(v_p)], axis=2), _BNN, P)
    rp = rho - via_rb[:, :, :LANES]
    y0 = _mm(a_rk, v_d, _BNN, P) - via_rb[:, :, LANES:]

    s = state_ref[...]
    ys = []
    for ch in range(nc):
        sl = slice(ch * n, (ch + 1) * n)
        ys.append(_mm(rp[sl], _pair_diag(s), _BNT, P) + y0[sl])
        s = _mm(s, pm[sl], _BNN, P) + q[sl]
    state_ref[...] = s
    y = jnp.concatenate(ys, axis=0) if nc > 1 else ys[0]

    g_all = y.shape[0]
    seg = jnp.where(same_head, 1.0, 0.0).astype(BF16)
    head_mean = lambda z: (_mm_exact_rhs(z.reshape(g_all * c, LANES), seg, 2) * (1.0 / C_HEAD_DIM)
                           ).reshape(g_all, c, LANES)
    cen = y - head_mean(y)
    out = cen * lax.rsqrt(head_mean(cen * cen) + GN_EPS)
    bonus = paired(8)
    gate = paired(9)
    idx = 0
    for ch in range(nc):
        for b in range(bb):
            for p in range(pairs):
                sl = slice(p * LANES, (p + 1) * LANES)
                yn = out[idx] * gng_ref[:, sl] + gnb_ref[:, sl]
                o_ref[b, ch * c:(ch + 1) * c, sl] = (yn + bonus[idx]) * gate[idx]
                idx += 1

    @pl.when(ci == pl.num_programs(1) - 1)
    def _():
        for b in range(bb):
            for p in range(pairs):
                s_out_ref[b, 2 * p] = s[b * pairs + p][:, :C_HEAD_DIM]
                s_out_ref[b, 2 * p + 1] = s[b * pairs + p][:, C_HEAD_DIM:]


def _rwkv_chunks(feats, s0, gn_g, gn_b):
    bsz, t, _ = feats.shape
    c = RWKV_CHUNK
    pairs = C_HEADS // 2
    bb = RWKV_GROUP if bsz % RWKV_GROUP == 0 else 1
    nc = max(1, min(RWKV_GROUP // bb, t // c))
    seq = pl.BlockSpec((bb, nc * c, C_WIDTH), lambda b, i: (b, i, 0))
    st = pl.BlockSpec((bb, C_HEADS, C_HEAD_DIM, C_HEAD_DIM), lambda b, i: (b, 0, 0, 0))
    vec = pl.BlockSpec((1, C_WIDTH), lambda b, i: (0, 0))
    return pl.pallas_call(
        functools.partial(_rwkv_chunk_kernel, c=c, nc=nc),
        grid=(bsz // bb, t // (nc * c)),
        in_specs=[pl.BlockSpec((bb, nc * c, C_WIDTH), lambda b, i, j=j: (b, i, j)) for j in range(N_C_FEATS)]
                 + [st, vec, vec],
        out_specs=[seq, st],
        out_shape=[jax.ShapeDtypeStruct((bsz, t, C_WIDTH), F32),
                   jax.ShapeDtypeStruct((bsz, C_HEADS, C_HEAD_DIM, C_HEAD_DIM), F32)],
        scratch_shapes=[pltpu.VMEM((bb * pairs, C_HEAD_DIM, LANES), F32)],
        compiler_params=_cparams("parallel", "arbitrary"),
        name="rwkv_chunks",
    )(*[feats] * N_C_FEATS, s0, gn_g, gn_b)


N_MERGE_REFS = 15


def _merge_rows(o0, l0, o1, l1, o2, l2, ob_ref, oc_ref, gate_ref, x_ref, wa_ref, wb_ref, wc_ref, wo_ref,
                gain_ref):
    ls = [l0[...], l1[...], l2[...]]
    mx = jnp.maximum(jnp.maximum(ls[0], ls[1]), ls[2])
    es = [jnp.exp(l - mx) for l in ls]
    den = es[0] + es[1] + es[2]
    o_a = (es[0] * o0[...] + es[1] * o1[...] + es[2] * o2[...]) / den
    merged = (gate_ref[:, 0:D_MODEL] * _mm(o_a, wa_ref[...])
              + gate_ref[:, D_MODEL:2 * D_MODEL] * _mm(ob_ref[...], wb_ref[...])
              + gate_ref[:, 2 * D_MODEL:] * _mm(oc_ref[...], wc_ref[...]))
    z = _mm(merged, wo_ref[...])
    return x_ref[...] + _rms(z, gain_ref[...])


def _gelu_tanh(x):
    return 0.5 * x * (1.0 + jnp.tanh(math.sqrt(2.0 / math.pi) * (x + 0.044715 * (x * x * x))))


FFN_PIECE = 128


def _ffn_kernel(*refs, tm, nb, cr):
    g1_ref, fu_ref, w_ref, b_ref, c_ref, fd_ref, g2_ref, out_ref, t_ref, ext_ref, f_ref = refs[N_MERGE_REFS:]

    @pl.when(pl.program_id(0) == 0)
    def _():
        ext_ref[0:cr, :] = c_ref[...]

    x1 = _merge_rows(*refs[:N_MERGE_REFS])
    h = _rms(x1, g1_ref[...]).astype(BF16)
    n_chunks = D_FF // FF_CHUNK
    split = (n_chunks + 1) // 2 * FF_CHUNK
    z = None
    for jc in range(n_chunks):
        pair = (slice(jc * FF_CHUNK, (jc + 1) * FF_CHUNK),
                slice(D_FF + jc * FF_CHUNK, D_FF + (jc + 1) * FF_CHUNK))
        for cols in pair:
            ext_ref[cr:cr + tm, cols] = _dg(h, fu_ref[:, cols], _NN)
        for r0 in range(0, tm, FFN_PIECE):
            cus = []
            for cols in pair:
                if nb % SUBLANES == 0:
                    taps = [ext_ref[r0 + cr - s * nb:r0 + cr - s * nb + FFN_PIECE, cols] for s in (2, 1, 0)]
                else:
                    blk = ext_ref[r0 + cr - SUBLANES:r0 + cr + FFN_PIECE, cols]
                    taps = [pltpu.roll(blk, s * nb, 0)[SUBLANES:] for s in (2, 1)] + [blk[SUBLANES:]]
                cus.append(w_ref[0:1, cols] * taps[0] + w_ref[1:2, cols] * taps[1]
                           + w_ref[2:3, cols] * taps[2] + b_ref[:, cols])
            f_ref[r0:r0 + FFN_PIECE, jc * FF_CHUNK:(jc + 1) * FF_CHUNK] = (
                _gelu_tanh(cus[0]) * cus[1]).astype(BF16)
        if (jc + 1) * FF_CHUNK == split:
            z = _dg(f_ref[:, 0:split], fd_ref[0:split, :], _NN)
    z = z + _dg(f_ref[:, split:], fd_ref[split:, :], _NN)
    tail = ext_ref[tm:tm + cr, :]
    t_ref[...] = tail
    ext_ref[0:cr, :] = tail
    out_ref[...] = x1 + _rms(z, g2_ref[...])


def _merge_ffn(att, o_b, o_c, gates, x, ctx, p, nb):
    m = x.shape[0]
    tm = min(m, 256)
    cr = ctx.shape[0]
    row = lambda i: (i, 0)
    fixed = lambda i: (0, 0)
    rows = lambda w: pl.BlockSpec((tm, w), row)
    once = lambda shape: pl.BlockSpec(shape, fixed, pipeline_mode=pl.Buffered(1))
    resident = lambda name: _layer_weight(p[name], p['layer'], pipeline_mode=pl.Buffered(1))
    merge_ws = ('w_br_a', 'w_br_b', 'w_br_c', 'w_out')
    x2, tail = pl.pallas_call(
        functools.partial(_ffn_kernel, tm=tm, nb=nb, cr=cr),
        grid=(m // tm,),
        in_specs=[pl.BlockSpec((tm, LANES), lambda i, col=col: (i, col)) for _, col in att]
                 + [rows(B_WIDTH), rows(C_WIDTH), rows(IN_GATE), rows(D_MODEL)]
                 + [resident(name) for name in merge_ws] + [once((1, D_MODEL))]
                 + [once((1, D_MODEL)),
                  _layer_weight(p['f_up'], p['layer'], pipeline_mode=pl.Buffered(1)),
                  once((F_CONV, 2 * D_FF)), once((1, 2 * D_FF)), once((cr, 2 * D_FF)),
                  _layer_weight(p['f_down'], p['layer'], pipeline_mode=pl.Buffered(1)), once((1, D_MODEL))],
        out_specs=[pl.BlockSpec((tm, D_MODEL), row), pl.BlockSpec((cr, 2 * D_FF), row)],
        out_shape=[jax.ShapeDtypeStruct((m, D_MODEL), F32),
                   jax.ShapeDtypeStruct(((m // tm) * cr, 2 * D_FF), F32)],
        scratch_shapes=[pltpu.VMEM((cr + tm, 2 * D_FF), F32), pltpu.VMEM((tm, D_FF), BF16)],
        compiler_params=_cparams("arbitrary"),
        name="merge_ffn",
    )(*[a for a, _ in att], o_b, o_c, gates, x, *[p[name] for name in merge_ws], p['norm_mix_post'],
      p['norm_ffn_pre'], p['f_up'], p['f_dw_w'], p['f_dw_b'], ctx, p['f_down'], p['norm_ffn_post'])
    return x2, tail[-cr:]


def _to_time_major(a):
    a = jnp.swapaxes(a, 0, 1)
    return a.reshape((a.shape[0] * a.shape[1],) + a.shape[2:])


def _to_batch_major(a, ts):
    return jnp.swapaxes(a.reshape(ts, a.shape[0] // ts, a.shape[1]), 0, 1)


def _layer_prompt(x, tables, p):
    t = x.shape[0]
    qkv, glu, pc, gates, o_b, feats = _in_proj(x, tables, jnp.zeros((4 * SUBLANES, B_WIDTH), F32),
                                               jnp.zeros((SUBLANES, IN_C), F32), p, 1)
    att = [(a, 0) for g in range(N_A_GROUPS) for a in _attn_prompt(qkv, g)]
    s0 = jnp.zeros((1, C_HEADS, C_HEAD_DIM, C_HEAD_DIM), F32)
    o_c, s_new = _rwkv_chunks(feats[None], s0, p['c_gn_g'], p['c_gn_b'])
    x2, tail = _merge_ffn(att, o_b, o_c[0], gates, x, jnp.zeros((SUBLANES, 2 * D_FF), F32), p, 1)
    kv = [qkv[t - min(w, t):, 3 * A_WIDTH * g + A_WIDTH:3 * A_WIDTH * (g + 1)]
          .reshape(1, min(w, t), 2, A_HEADS, A_HEAD_DIM) for g, (w, _) in enumerate(A_GROUPS)]
    f_tail = tail[-(F_CONV - 1):][None]
    return x2, kv, glu[t - (B_CONV - 1):][None], pc[t - 1:], s_new, f_tail


def _layer_sample(x, tables, caches_t, layer, b_ctx, c_shift, c_state, f_ctx, p, ts):
    nb = x.shape[0] // ts
    qkv, glu, pc, gates, o_b, feats = _in_proj(x, tables, _to_time_major(b_ctx), c_shift, p, nb)
    qkv_bm = _to_batch_major(qkv, ts)
    qkv_pad = jnp.pad(qkv_bm, ((0, 0), (0, SUBLANES - ts), (0, 0)))
    o, lse = _attn_sample(qkv_pad, caches_t, layer, ts)
    o_tm, lse_tm = _to_time_major(o[:, :ts]), _to_time_major(lse[:, :ts])
    att = [pair for g in range(N_A_GROUPS) for pair in ((o_tm, g), (lse_tm, g))]
    feats_bm = jnp.pad(_to_batch_major(feats, ts), ((0, 0), (0, RWKV_CHUNK - ts), (0, 0)))
    o_c, s_new = _rwkv_chunks(feats_bm, c_state, p['c_gn_g'], p['c_gn_b'])
    x2, tail = _merge_ffn(att, o_b, _to_time_major(o_c[:, :ts]), gates, x, _to_time_major(f_ctx), p, nb)
    kv = [_to_batch_major(qkv[:, 3 * A_WIDTH * g + A_WIDTH:3 * A_WIDTH * (g + 1)], ts)
          .reshape(nb, ts, 2, A_HEADS, A_HEAD_DIM) for g in range(N_A_GROUPS)]
    b_new = jnp.concatenate([b_ctx, _to_batch_major(glu, ts)], axis=1)[:, -(B_CONV - 1):]
    f_new = jnp.concatenate([f_ctx, _to_batch_major(tail, F_CONV - 1)], axis=1)
    return x2, kv, b_new, pc[(ts - 1) * nb:], s_new, f_new[:, -(F_CONV - 1):]


def kernel(x_prompt, x_sample, cache_a_kv0, cache_a_kv1, cache_a_kv2, state_b_conv, state_c_shift, state_c_wkv, state_f_conv, norm_mix_pre, norm_mix_post, norm_ffn_pre, norm_ffn_post, w_in, b_dw_w, b_dw_b, b_ln_g, b_ln_b, c_mu, c_w0, c_w2, c_a0, c_a2, c_g2, c_kk, c_ka, c_rk, c_gn_g, c_gn_b, w_br_a, w_br_b, w_br_c, w_out, f_up, f_dw_w, f_dw_b, f_down):
    depth = w_in.shape[0]
    bp, tp, _ = x_prompt.shape
    bs, ts, _ = x_sample.shape
    assert bp == 1 and ts <= SUBLANES
    caches_t = [jnp.transpose(c, (0, 1, 3, 4, 5, 2)) for c in (cache_a_kv0, cache_a_kv1, cache_a_kv2)]

    head_id = jnp.arange(C_WIDTH) // C_HEAD_DIM
    seg = (head_id[:, None] == head_id[None, :]).astype(BF16)
    zpad = jnp.zeros((LANES - C_DECAY_RANK, C_WIDTH), F32)

    tab_p = _rope_tables(tp, 0, 1)
    tab_s = _rope_tables(ts * bs, PAST_LEN, bs)

    stacked = {name: w.astype(BF16) for name, w in (
        ('w_in', w_in), ('w_br_a', w_br_a), ('w_br_b', w_br_b), ('w_br_c', w_br_c), ('w_out', w_out),
        ('f_up', f_up), ('f_down', f_down))}

    y_p = x_prompt[0]
    y_s = _to_time_major(x_sample)
    outs_p, outs_s = [], []
    for l in range(depth):
        vec = lambda a: a[l][None, :]
        p = {
            **stacked, 'layer': l,
            'norm_mix_pre': vec(norm_mix_pre), 'norm_mix_post': vec(norm_mix_post),
            'norm_ffn_pre': vec(norm_ffn_pre), 'norm_ffn_post': vec(norm_ffn_post),
            'b_dw_w': b_dw_w[l], 'b_dw_b': vec(b_dw_b), 'b_ln_g': vec(b_ln_g), 'b_ln_b': vec(b_ln_b),
            'c_mu': vec(c_mu), 'c_w0': vec(c_w0), 'c_a0': vec(c_a0),
            'c_w2p': jnp.concatenate([c_w2[l], zpad], axis=0),
            'c_a2p': jnp.concatenate([zpad, c_a2[l]], axis=0),
            'c_g2': c_g2[l], 'c_kk': vec(c_kk), 'c_ka': vec(c_ka),
            'c_rk': c_rk[l].reshape(1, C_WIDTH), 'c_gn_g': vec(c_gn_g), 'c_gn_b': vec(c_gn_b),
            'seg': seg,
            'f_dw_w': f_dw_w[l], 'f_dw_b': vec(f_dw_b),
        }
        y_p, *rest_p = _layer_prompt(y_p, tab_p, p)
        outs_p.append(rest_p)
        y_s, *rest_s = _layer_sample(y_s, tab_s, caches_t, l, state_b_conv[l], state_c_shift[l],
                                     state_c_wkv[l], state_f_conv[l], p, ts)
        outs_s.append(rest_s)

    stack = lambda outs, f: jnp.stack([f(o) for o in outs])
    res = [y_p[None], _to_batch_major(y_s, ts)]
    res += [stack(outs_p, lambda o, g=g: o[0][g]) for g in range(N_A_GROUPS)]
    res += [stack(outs_s, lambda o, g=g: o[0][g]) for g in range(N_A_GROUPS)]
    for idx in range(1, 5):
        res += [stack(outs_p, lambda o: o[idx]), stack(outs_s, lambda o: o[idx])]
    return tuple(res)
```

```python
import functools
import math

import jax
import jax.numpy as jnp
from jax import lax
from jax.experimental import pallas as pl
from jax.experimental.pallas import tpu as pltpu

F32 = jnp.float32
BF16 = jnp.bfloat16

D_MODEL = 1024
PAST_LEN = 16384
A_GROUPS = ((128, 1), (512, 4), (2048, 16))
N_A_GROUPS = len(A_GROUPS)
A_HEAD_DIM = 32
A_HEADS = 4
A_WIDTH = A_HEADS * A_HEAD_DIM
ROPE_THETA = 10000.0
ATT_BLOCK = 128
B_WIDTH = 256
B_CONV = 31
C_HEAD_DIM = 64
C_WIDTH = 384
C_HEADS = 6
C_DECAY_RANK = 64
C_ICLR_RANK = 64
C_GATE_RANK = 128
DECAY_SCALE = math.exp(-0.5)
D_FF = 2816
F_CONV = 3
IN_A = N_A_GROUPS * 3 * A_WIDTH
IN_B = 2 * B_WIDTH
IN_C = 3 * C_WIDTH + C_DECAY_RANK + C_ICLR_RANK + C_GATE_RANK
IN_GATE = 3 * D_MODEL
IN_WIDTH = IN_A + IN_B + IN_C + IN_GATE
RMS_EPS = 1e-6
LN_EPS = 1e-5
GN_EPS = C_HEAD_DIM * 1e-5
NEG_BIG = -1e30

LANES = 128
SUBLANES = 8
VMEM_LIMIT = 56 * 1024 * 1024

RWKV_CHUNK = 64
FF_CHUNK = 256


def _cparams(*sem):
    return pltpu.CompilerParams(dimension_semantics=sem, vmem_limit_bytes=VMEM_LIMIT)


def _split_bf16(x, n):
    pieces = []
    rem = x
    for _ in range(n):
        p = rem.astype(BF16)
        pieces.append(p)
        rem = rem - p.astype(F32)
    return pieces


_NN = (((1,), (0,)), ((), ()))
_NT = (((1,), (1,)), ((), ()))


def _dg(a, b, dims):
    return lax.dot_general(a, b, dims, preferred_element_type=F32)


def _mm(a, b, dims=_NN, passes=1):
    if passes == 1:
        return _dg(a.astype(BF16), b.astype(BF16), dims)
    a_hi, a_lo = _split_bf16(a, 2)
    b_hi, b_lo = _split_bf16(b, 2)
    return _dg(a_hi, b_hi, dims) + (_dg(a_hi, b_lo, dims) + _dg(a_lo, b_hi, dims))


def _mm_exact_rhs(a, b_bf16, pieces=3):
    out = None
    for p in _split_bf16(a, pieces):
        t = _dg(p, b_bf16, _NN)
        out = t if out is None else out + t
    return out


def _mm_exact_rhs_left(a_bf16, b, pieces=3):
    out = None
    for p in _split_bf16(b, pieces):
        t = _dg(a_bf16, p, _NN)
        out = t if out is None else out + t
    return out


def _rms(x, g):
    ms = jnp.mean(x * x, axis=-1, keepdims=True)
    return x * lax.rsqrt(ms + RMS_EPS) * g


def _sigmoid(x):
    return 1.0 / (1.0 + jnp.exp(-x))


def _rope_table_kernel(inv_ref, cos_ref, sin_ref, cl_ref, sl_ref, *, tm, pos0, nb):
    inv = inv_ref[...]

    @pl.when(pl.program_id(0) == 0)
    def _():
        within = (lax.broadcasted_iota(jnp.int32, (tm, LANES), 0) // nb).astype(F32) * inv
        cl_ref[...] = jnp.cos(within)
        sl_ref[...] = jnp.sin(within)

    base = (pos0 + pl.program_id(0) * (tm // nb)).astype(F32) * jnp.broadcast_to(inv, (SUBLANES, LANES))
    ch = jnp.cos(base)[0:1, :]
    sh = jnp.sin(base)[0:1, :]
    lane = lax.broadcasted_iota(jnp.int32, (tm, LANES), 1)
    sign = jnp.where((lane % A_HEAD_DIM) < A_HEAD_DIM // 2, -1.0, 1.0)
    cos_ref[...] = ch * cl_ref[...] - sh * sl_ref[...]
    sin_ref[...] = (sh * cl_ref[...] + ch * sl_ref[...]) * sign


def _rope_tables(m, pos0, nb):
    half = A_HEAD_DIM // 2
    inv = ROPE_THETA ** (-(jnp.arange(half, dtype=F32) * 2.0 / A_HEAD_DIM))
    inv_lane = jnp.tile(inv, 2 * A_HEADS)[None, :]
    tm = min(m, 512)
    assert tm % nb == 0
    return pl.pallas_call(
        functools.partial(_rope_table_kernel, tm=tm, pos0=pos0, nb=nb),
        grid=(m // tm,),
        in_specs=[pl.BlockSpec((1, LANES), lambda i: (0, 0))],
        out_specs=[pl.BlockSpec((tm, LANES), lambda i: (i, 0))] * 2,
        out_shape=[jax.ShapeDtypeStruct((m, LANES), F32)] * 2,
        scratch_shapes=[pltpu.VMEM((tm, LANES), F32)] * 2,
        compiler_params=_cparams("arbitrary"),
        name="rope_tables",
    )(inv_lane)


N_C_PARAMS = 10
N_C_FEATS = 8


def _in_proj_kernel(x_ref, g_ref, w_ref, cos_ref, sin_ref, bctx_ref, bw_ref, bb_ref, bg_ref, bbeta_ref, cctx_ref,
                    *rest, tm, nb, bctx, coff, carry):
    c_params = rest[:N_C_PARAMS]
    qkv_ref, glu_ref, pc_ref, gate_ref, ob_ref = rest[N_C_PARAMS:N_C_PARAMS + 5]
    feat_ref = rest[N_C_PARAMS + 5]
    c_outs = [feat_ref.at[:, j * C_WIDTH:(j + 1) * C_WIDTH] for j in range(N_C_FEATS)]
    extb_ref, extc_ref = rest[N_C_PARAMS + 6:N_C_PARAMS + 8]
    shift_ref = rest[N_C_PARAMS + 8] if len(rest) > N_C_PARAMS + 8 else None

    @pl.when(pl.program_id(0) == 0)
    def _():
        extb_ref[0:bctx, :] = bctx_ref[...]
        extc_ref[0:coff, :] = cctx_ref[...]

    h = _rms(x_ref[...], g_ref[...]).astype(BF16)
    cos = cos_ref[...]
    sin = sin_ref[...]
    lane = lax.broadcasted_iota(jnp.int32, cos.shape, 1)
    first_half = (lane % A_HEAD_DIM) < A_HEAD_DIM // 2
    def branch_gate(j):
        cols = slice(j * D_MODEL, (j + 1) * D_MODEL)
        gate_ref[:, cols] = _sigmoid(_dg(h, w_ref[:, IN_A + IN_B + IN_C + j * D_MODEL:
                                                  IN_A + IN_B + IN_C + (j + 1) * D_MODEL], _NN)).astype(gate_ref.dtype)

    pb = _dg(h, w_ref[:, IN_A:IN_A + IN_B], _NN)
    extb_ref[bctx:bctx + tm, :] = pb[:, :B_WIDTH] * _sigmoid(pb[:, B_WIDTH:])
    glu_ref[...] = extb_ref[tm:tm + bctx, :]
    branch_gate(0)
    _conv_b_body(extb_ref, bw_ref, bb_ref, bg_ref, bbeta_ref, ob_ref, tm=tm, nb=nb,
                 base=bctx - (B_CONV - 1) * nb, shift_ref=shift_ref)
    pc = _dg(h, w_ref[:, IN_A + IN_B:IN_A + IN_B + IN_C], _NN)
    extc_ref[coff:coff + tm, :] = pc
    pc_ref[...] = extc_ref[tm:tm + coff, :]
    branch_gate(1)
    _rwkv_feature_rows(pc, extc_ref[coff - nb:coff - nb + tm, :], *c_params, *c_outs)
    pa = _dg(h, w_ref[:, 0:IN_A], _NN)
    branch_gate(2)
    for blk in range(IN_A // LANES):
        y = pa[:, blk * LANES:(blk + 1) * LANES]
        if blk % 3 != 2:
            swapped = jnp.where(first_half, pltpu.roll(y, LANES - A_HEAD_DIM // 2, 1),
                                pltpu.roll(y, A_HEAD_DIM // 2, 1))
            y = y * cos + swapped * sin
        qkv_ref[:, blk * LANES:(blk + 1) * LANES] = y
    if carry:
        extb_ref[0:bctx, :] = extb_ref[tm:tm + bctx, :]
        extc_ref[0:coff, :] = extc_ref[tm:tm + coff, :]


def _layer_weight(w, layer, **kw):
    return pl.BlockSpec((None,) + w.shape[1:], lambda *_: (layer, 0, 0), **kw)


def _in_proj(x, tables, b_ctx, c_ctx, p, nb):
    m = x.shape[0]
    tm = min(m, 256)
    bctx, coff = b_ctx.shape[0], c_ctx.shape[0]
    row = lambda i: (i, 0)
    fixed = lambda i: (0, 0)
    full = lambda a: pl.BlockSpec(a.shape, fixed)
    widths = (IN_A, B_WIDTH, IN_C, IN_GATE, B_WIDTH, N_C_FEATS * C_WIDTH)
    heights = (tm, bctx, coff, tm, tm, tm)
    b_params = (b_ctx, p['b_dw_w'], p['b_dw_b'], p['b_ln_g'], p['b_ln_b'])
    c_params = (c_ctx, p['c_mu'], p['c_w2p'], p['c_a2p'], p['c_g2'], p['c_w0'], p['c_a0'], p['c_kk'], p['c_ka'],
                p['c_rk'], p['seg'])
    scratch = [pltpu.VMEM((bctx + tm, B_WIDTH), F32), pltpu.VMEM((coff + tm, IN_C), F32)]
    if nb == 1:
        scratch.append(pltpu.VMEM((SUBLANES, bctx + tm - SUBLANES, B_WIDTH), F32))
    outs = pl.pallas_call(
        functools.partial(_in_proj_kernel, tm=tm, nb=nb, bctx=bctx, coff=coff, carry=m > tm),
        grid=(m // tm,),
        in_specs=[pl.BlockSpec((tm, D_MODEL), row), pl.BlockSpec((1, D_MODEL), fixed),
                  _layer_weight(p['w_in'], p['layer'], pipeline_mode=pl.Buffered(1)),
                  pl.BlockSpec((tm, LANES), row), pl.BlockSpec((tm, LANES), row)]
                 + [full(a) for a in b_params + c_params],
        out_specs=[pl.BlockSpec((hgt, w), row) for hgt, w in zip(heights, widths)],
        out_shape=[jax.ShapeDtypeStruct(((m // tm) * hgt, w), BF16 if j == 3 else F32)
                   for j, (hgt, w) in enumerate(zip(heights, widths))],
        scratch_shapes=scratch,
        compiler_params=_cparams("arbitrary"),
        name="in_proj",
    )(x, p['norm_mix_pre'], p['w_in'], *tables, *b_params, *c_params)
    return outs


def _softmax_heads(q, k_bf16, v_bf16, valid):
    rows = q.shape[0]
    lane_head = lax.broadcasted_iota(jnp.int32, q.shape, 1) // A_HEAD_DIM
    scale = 1.0 / math.sqrt(A_HEAD_DIM)
    qs = q * scale
    q_heads = jnp.concatenate([jnp.where(lane_head == h, qs, 0.0) for h in range(A_HEADS)], axis=0)
    s_all = _dg(q_heads.astype(BF16), k_bf16, _NT)
    ps, ls, lses = [], [], []
    for h in range(A_HEADS):
        s = jnp.where(valid, s_all[h * rows:(h + 1) * rows], NEG_BIG)
        mx = jnp.max(s, axis=-1, keepdims=True)
        p = jnp.exp(s - mx)
        l = jnp.sum(p, axis=-1, keepdims=True)
        ps.append(p.astype(BF16))
        ls.append(l)
        lses.append(mx + jnp.log(l))
    o_heads = _dg(jnp.concatenate(ps, axis=0), v_bf16, _NN)
    o_all = jnp.zeros(q.shape, F32)
    lse_all = jnp.zeros(q.shape, F32)
    for h in range(A_HEADS):
        hm = lane_head == h
        o_all = jnp.where(hm, o_heads[h * rows:(h + 1) * rows] / ls[h], o_all)
        lse_all = jnp.where(hm, lses[h], lse_all)
    return o_all, lse_all


ATT_UNROLL = 8


def _attn_prompt_kernel(q_ref, kh_ref, kc_ref, vh_ref, vc_ref, o_ref, lse_ref, kx_ref, vx_ref, *, d, nsub):
    i = pl.program_id(0)
    hb = d * ATT_BLOCK
    kx_ref[0:hb, :] = kh_ref[...]
    kx_ref[hb:, :] = kc_ref[...]
    vx_ref[0:hb, :] = vh_ref[...]
    vx_ref[hb:, :] = vc_ref[...]
    qi = lax.broadcasted_iota(jnp.int32, (ATT_BLOCK, 2 * ATT_BLOCK), 0)
    kj = lax.broadcasted_iota(jnp.int32, (ATT_BLOCK, 2 * ATT_BLOCK), 1)
    delta = qi + ATT_BLOCK - kj
    band = (delta >= 0) & (delta <= ATT_BLOCK)
    current = kj >= ATT_BLOCK

    def problem(it, carry):
        j = it // d
        start = j * hb + it % d
        valid = band & (current | (i > 0) | (j > 0))
        keys = pl.ds(start, 2 * ATT_BLOCK, stride=d)
        rows = pl.ds(start, ATT_BLOCK, stride=d)
        o, lse = _softmax_heads(q_ref[rows, :], kx_ref[keys, :].astype(BF16), vx_ref[keys, :].astype(BF16), valid)
        o_ref[rows, :] = o
        lse_ref[rows, :] = lse
        return carry

    lax.fori_loop(0, nsub * d, problem, 0, unroll=ATT_UNROLL)


def _attn_prompt(qkv, group):
    t = qkv.shape[0]
    _, d = A_GROUPS[group]
    hb = d * ATT_BLOCK
    nsub = max(1, 8 // d)
    while t % (nsub * hb):
        nsub //= 2
    col = 3 * group
    blk = (nsub * hb, LANES)
    cur = lambda off: (lambda i: (i, col + off))
    halo = lambda off: (lambda i: (jnp.maximum(i * nsub - 1, 0), col + off))
    return pl.pallas_call(
        functools.partial(_attn_prompt_kernel, d=d, nsub=nsub),
        grid=(t // (nsub * hb),),
        in_specs=[pl.BlockSpec(blk, cur(0)), pl.BlockSpec((hb, LANES), halo(1)), pl.BlockSpec(blk, cur(1)),
                  pl.BlockSpec((hb, LANES), halo(2)), pl.BlockSpec(blk, cur(2))],
        out_specs=[pl.BlockSpec(blk, lambda i: (i, 0))] * 2,
        out_shape=[jax.ShapeDtypeStruct((t, LANES), F32)] * 2,
        scratch_shapes=[pltpu.VMEM(((nsub + 1) * hb, LANES), F32)] * 2,
        compiler_params=_cparams("parallel"),
        name=f"attn_prompt_g{group}",
    )(qkv, qkv, qkv, qkv, qkv)


def _attn_sample_kernel(qkv_ref, c0_ref, c1_ref, c2_ref, o_ref, lse_ref, *, ts):
    for bi in range(qkv_ref.shape[0]):
        _attn_sample_one(bi, qkv_ref, c0_ref, c1_ref, c2_ref, o_ref, lse_ref, ts)


def _attn_sample_one(bi, qkv_ref, c0_ref, c1_ref, c2_ref, o_ref, lse_ref, ts):
    qkv = qkv_ref[bi]
    rows = qkv.shape[0]
    pad = jnp.zeros((LANES - rows, LANES), F32)
    for g, (window, d) in enumerate(A_GROUPS):
        base = 3 * A_WIDTH * g
        q = qkv[:, base:base + LANES]
        k_new = jnp.concatenate([qkv[:, base + LANES:base + 2 * LANES], pad], axis=0)
        v_new = jnp.concatenate([qkv[:, base + 2 * LANES:base + 3 * LANES], pad], axis=0)
        c_ref = (c0_ref, c1_ref, c2_ref)[g]
        k_t = c_ref[bi, 0].reshape(A_WIDTH, window).astype(BF16)
        v_t = c_ref[bi, 1].reshape(A_WIDTH, window).astype(BF16)
        qi = lax.broadcasted_iota(jnp.int32, (rows, window), 0)
        c = lax.broadcasted_iota(jnp.int32, (rows, window), 1)
        qn = lax.broadcasted_iota(jnp.int32, (rows, LANES), 0)
        cn = lax.broadcasted_iota(jnp.int32, (rows, LANES), 1)
        if d == 1:
            valid_c, valid_n = c >= qi, cn <= qn
        else:
            valid_c, valid_n = (c % d) == qi, cn == qn
        lane_head = lax.broadcasted_iota(jnp.int32, q.shape, 1) // A_HEAD_DIM
        scale = 1.0 / math.sqrt(A_HEAD_DIM)
        qs = q * scale
        q_heads = jnp.concatenate([jnp.where(lane_head == h, qs, 0.0) for h in range(A_HEADS)],
                                  axis=0).astype(BF16)
        sc_all = _dg(q_heads, k_t, _NN)
        sn_all = _dg(q_heads, k_new.astype(BF16), _NT)
        pcs, pns, ls, lses = [], [], [], []
        for h in range(A_HEADS):
            s_c = jnp.where(valid_c, sc_all[h * rows:(h + 1) * rows], NEG_BIG)
            s_n = jnp.where(valid_n, sn_all[h * rows:(h + 1) * rows], NEG_BIG)
            mx = jnp.maximum(jnp.max(s_c, axis=-1, keepdims=True), jnp.max(s_n, axis=-1, keepdims=True))
            pcs.append(jnp.exp(s_c - mx))
            pns.append(jnp.exp(s_n - mx))
            ls.append(jnp.sum(pcs[-1], axis=-1, keepdims=True) + jnp.sum(pns[-1], axis=-1, keepdims=True))
            lses.append(mx + jnp.log(ls[-1]))
        o_heads = (_dg(jnp.concatenate(pcs, axis=0).astype(BF16), v_t, _NT)
                   + _dg(jnp.concatenate(pns, axis=0).astype(BF16), v_new.astype(BF16), _NN))
        o_all = jnp.zeros(q.shape, F32)
        lse_all = jnp.zeros(q.shape, F32)
        for h in range(A_HEADS):
            hm = lane_head == h
            o_all = jnp.where(hm, o_heads[h * rows:(h + 1) * rows] / ls[h], o_all)
            lse_all = jnp.where(hm, lses[h], lse_all)
        o_ref[bi, :, g * LANES:(g + 1) * LANES] = o_all
        lse_ref[bi, :, g * LANES:(g + 1) * LANES] = lse_all


def _attn_sample(qkv_bm, caches_t, layer, ts):
    bsz, rows, _ = qkv_bm.shape
    bb = 4 if bsz % 4 == 0 else 1
    specs = [pl.BlockSpec((bb, rows, IN_A), lambda b: (b, 0, 0))]
    for (window, d), c in zip(A_GROUPS, caches_t):
        assert c.shape[-1] == window and window % d == 0 and (d == 1 or d >= ts)
        specs.append(pl.BlockSpec((None, bb, 2, A_HEADS, A_HEAD_DIM, window),
                                  lambda b: (layer, b, 0, 0, 0, 0)))
    out = jax.ShapeDtypeStruct((bsz, rows, N_A_GROUPS * LANES), F32)
    return pl.pallas_call(
        functools.partial(_attn_sample_kernel, ts=ts),
        grid=(bsz // bb,),
        in_specs=specs,
        out_specs=[pl.BlockSpec((bb, rows, N_A_GROUPS * LANES), lambda b: (b, 0, 0))] * 2,
        out_shape=[out, out],
        compiler_params=_cparams("parallel"),
        name="attn_sample",
    )(qkv_bm, *caches_t)


def _conv_b_body(ext_ref, w_ref, b_ref, g_ref, beta_ref, o_ref, *, tm, nb, base, shift_ref=None):
    acc = jnp.zeros((tm, B_WIDTH), F32) + b_ref[...]
    if shift_ref is not None:
        n = shift_ref.shape[1]
        for s in range(1, SUBLANES):
            shift_ref[s] = ext_ref[s:s + n, :]
    for j in range(B_CONV):
        off = base + j * nb
        if shift_ref is None or off % SUBLANES == 0:
            tap = ext_ref[pl.ds(off, tm), :]
        else:
            tap = shift_ref[off % SUBLANES, pl.ds(off - off % SUBLANES, tm), :]
        acc = acc + w_ref[j:j + 1, :] * tap
    mu = jnp.mean(acc, axis=-1, keepdims=True)
    cen = acc - mu
    var = jnp.mean(cen * cen, axis=-1, keepdims=True)
    y = cen * lax.rsqrt(var + LN_EPS) * g_ref[...] + beta_ref[...]
    o_ref[...] = y * _sigmoid(y)


def _rwkv_feature_rows(pc, prev, mu_ref, w2_ref, a2_ref, g2_ref, w0_ref, a0_ref, kkw_ref, ka_ref, rk_ref, seg_ref,
                       r_o, lw_o, k_o, v_o, kk_o, b_o, g_o, bonus_o):
    xs = pc + mu_ref[...] * (prev - pc)
    r = xs[:, 0:C_WIDTH]
    k = xs[:, C_WIDTH:2 * C_WIDTH]
    v = xs[:, 2 * C_WIDTH:3 * C_WIDTH]
    wa = xs[:, 3 * C_WIDTH:3 * C_WIDTH + LANES]
    gl = xs[:, 3 * C_WIDTH + LANES:]
    seg = seg_ref[...]
    lw = -DECAY_SCALE * _sigmoid(w0_ref[...] + _mm(jnp.tanh(wa), w2_ref[...]))
    a = _sigmoid(a0_ref[...] + _mm(wa, a2_ref[...]))
    g = _mm(_sigmoid(gl), g2_ref[...])
    kk = k * kkw_ref[...]
    ss = _mm_exact_rhs(kk * kk, seg, 2)
    kk = kk * lax.rsqrt(jnp.maximum(ss, 1e-24))
    k2 = k * (1.0 + (a - 1.0) * ka_ref[...])
    r_o[...] = r
    lw_o[...] = lw
    k_o[...] = k2
    v_o[...] = v
    kk_o[...] = kk
    b_o[...] = kk * a
    g_o[...] = g
    bonus_o[...] = _mm_exact_rhs(r * k2 * rk_ref[...], seg, 2) * v


RWKV_PASSES = 1
RWKV_GROUP = 8


_BNN = (((2,), (1,)), ((0,), (0,)))
_BNT = (((2,), (2,)), ((0,), (0,)))
_BTN = (((1,), (1,)), ((0,), (0,)))


def _pair_diag(x):
    first = lax.broadcasted_iota(jnp.int32, x.shape[1:], 1) < x.shape[-1] // 2
    return jnp.concatenate([jnp.where(first, x, 0.0), jnp.where(first, 0.0, x)], axis=1)


def _unit_lower_inverse(a_strict):
    g, c, _ = a_strict.shape
    ri = lax.broadcasted_iota(jnp.int32, (c, 2 * c), 0)
    ci = lax.broadcasted_iota(jnp.int32, (c, 2 * c), 1) % c
    inv = jnp.where(ri == ci, 1.0, 0.0) - jnp.where(((ri // 2) == (ci // 2)) & (ri > ci), a_strict, 0.0)
    s = 2
    while s < c:
        same = (ri // (2 * s)) == (ci // (2 * s))
        off = jnp.where(same & ((ri % (2 * s)) >= s) & ((ci % (2 * s)) < s), a_strict, 0.0)
        step = _mm(off, _pair_diag(inv), _BNN, RWKV_PASSES)
        inv = inv - _mm(inv, _pair_diag(step), _BNN, RWKV_PASSES)
        s *= 2
    return inv


def _rwkv_chunk_kernel(r_ref, lw_ref, k_ref, v_ref, kk_ref, b_ref, g_ref, bonus_ref, s0_ref, gng_ref, gnb_ref,
                       o_ref, s_out_ref, state_ref, *, c, nc):
    assert 2 * c == LANES and 2 * C_HEAD_DIM == LANES
    bb = r_ref.shape[0]
    pairs = C_HEADS // 2
    n = bb * pairs
    ci = pl.program_id(1)

    @pl.when(ci == 0)
    def _():
        for b in range(bb):
            for p in range(pairs):
                state_ref[b * pairs + p] = jnp.concatenate([s0_ref[b, 2 * p], s0_ref[b, 2 * p + 1]], axis=-1)

    ri = lax.broadcasted_iota(jnp.int32, (c, LANES), 0)
    cj = lax.broadcasted_iota(jnp.int32, (c, LANES), 1) % c
    incl = ri >= cj
    strict = ri > cj
    tri = jnp.where(lax.broadcasted_iota(jnp.int32, (c, c), 0) >= lax.broadcasted_iota(jnp.int32, (c, c), 1),
                    1.0, 0.0).astype(BF16)

    groups = []
    for ch in range(nc):
        rows = slice(ch * c, (ch + 1) * c)
        for b in range(bb):
            lw = lw_ref[b, rows, :]
            cum = _mm_exact_rhs_left(tri, lw)
            cend = cum[c - 1:c, :]
            w_inv = jnp.exp(-cum)
            w_end = jnp.exp(cend - cum)
            kvec = k_ref[b, rows, :]
            bvec = b_ref[b, rows, :]
            groups.append((kk_ref[b, rows, :] * jnp.exp(cum - lw), r_ref[b, rows, :] * jnp.exp(cum),
                           bvec * w_inv, kvec * w_inv, bvec * w_end, kvec * w_end, v_ref[b, rows, :],
                           jnp.exp(cend), bonus_ref[b, rows, :], g_ref[b, rows, :]))

    def paired(idx):
        return jnp.stack([grp[idx][:, p * LANES:(p + 1) * LANES] for grp in groups for p in range(pairs)])

    P = RWKV_PASSES
    kap, rho, bt, kt, bend, kend, v, wc = [paired(i) for i in range(8)]
    lhs2 = jnp.concatenate([kap, rho], axis=1)
    x2 = _mm(lhs2, jnp.concatenate([_pair_diag(bt), _pair_diag(kt)], axis=1), _BNT, P)
    a_kb = jnp.where(strict, x2[:, :c, :LANES], 0.0)
    a_rb = jnp.where(incl, x2[:, c:, :LANES], 0.0)
    a_kk = jnp.where(strict, x2[:, :c, LANES:], 0.0)
    a_rk = jnp.where(incl, x2[:, c:, LANES:], 0.0)
    t_inv = _unit_lower_inverse(a_kb)
    v_d = _pair_diag(v)
    solved = _mm(t_inv, jnp.concatenate([_pair_diag(kap), _pair_diag(_mm(a_kk, v_d, _BNN, P))], axis=2), _BNN, P)
    kap_p = solved[:, :, :LANES]
    v_p = solved[:, :, LANES:]
    er = lax.broadcasted_iota(jnp.int32, (LANES, LANES), 0)
    ec = lax.broadcasted_iota(jnp.int32, (LANES, LANES), 1)
    same_head = (er // C_HEAD_DIM) == (ec // C_HEAD_DIM)
    first = lax.broadcasted_iota(jnp.int32, (C_HEAD_DIM, LANES), 1) < C_HEAD_DIM
    with_bend = _mm(solved, bend, _BTN, P)
    pm = jnp.where(er == ec, wc, 0.0) - jnp.where(same_head, with_bend[:, :LANES], 0.0)
    fq = _mm(v, kend, _BTN, P) - with_bend[:, LANES:]
    q = jnp.where(first, fq[:, :C_HEAD_DIM], fq[:, C_HEAD_DIM:])
    via_rb = _mm(a_rb, jnp.concatenate([_pair_diag(kap_p), _pair_diag(v_p)], axis=2), _BNN, P)
    rp = rho - via_rb[:, :, :LANES]
    y0 = _mm(a_rk, v_d, _BNN, P) - via_rb[:, :, LANES:]

    s = state_ref[...]
    ys = []
    for ch in range(nc):
        sl = slice(ch * n, (ch + 1) * n)
        ys.append(_mm(rp[sl], _pair_diag(s), _BNT, P) + y0[sl])
        s = _mm(s, pm[sl], _BNN, P) + q[sl]
    state_ref[...] = s
    y = jnp.concatenate(ys, axis=0) if nc > 1 else ys[0]

    g_all = y.shape[0]
    seg = jnp.where(same_head, 1.0, 0.0).astype(BF16)
    head_mean = lambda z: (_mm_exact_rhs(z.reshape(g_all * c, LANES), seg, 2) * (1.0 / C_HEAD_DIM)
                           ).reshape(g_all, c, LANES)
    cen = y - head_mean(y)
    out = cen * lax.rsqrt(head_mean(cen * cen) + GN_EPS)
    bonus = paired(8)
    gate = paired(9)
    idx = 0
    for ch in range(nc):
        for b in range(bb):
            for p in range(pairs):
                sl = slice(p * LANES, (p + 1) * LANES)
                yn = out[idx] * gng_ref[:, sl] + gnb_ref[:, sl]
                o_ref[b, ch * c:(ch + 1) * c, sl] = (yn + bonus[idx]) * gate[idx]
                idx += 1

    @pl.when(ci == pl.num_programs(1) - 1)
    def _():
        for b in range(bb):
            for p in range(pairs):
                s_out_ref[b, 2 * p] = s[b * pairs + p][:, :C_HEAD_DIM]
                s_out_ref[b, 2 * p + 1] = s[b * pairs + p][:, C_HEAD_DIM:]


def _rwkv_chunks(feats, s0, gn_g, gn_b):
    bsz, t, _ = feats.shape
    c = RWKV_CHUNK
    pairs = C_HEADS // 2
    bb = RWKV_GROUP if bsz % RWKV_GROUP == 0 else 1
    nc = max(1, min(RWKV_GROUP // bb, t // c))
    seq = pl.BlockSpec((bb, nc * c, C_WIDTH), lambda b, i: (b, i, 0))
    st = pl.BlockSpec((bb, C_HEADS, C_HEAD_DIM, C_HEAD_DIM), lambda b, i: (b, 0, 0, 0))
    vec = pl.BlockSpec((1, C_WIDTH), lambda b, i: (0, 0))
    return pl.pallas_call(
        functools.partial(_rwkv_chunk_kernel, c=c, nc=nc),
        grid=(bsz // bb, t // (nc * c)),
        in_specs=[pl.BlockSpec((bb, nc * c, C_WIDTH), lambda b, i, j=j: (b, i, j)) for j in range(N_C_FEATS)]
                 + [st, vec, vec],
        out_specs=[seq, st],
        out_shape=[jax.ShapeDtypeStruct((bsz, t, C_WIDTH), F32),
                   jax.ShapeDtypeStruct((bsz, C_HEADS, C_HEAD_DIM, C_HEAD_DIM), F32)],
        scratch_shapes=[pltpu.VMEM((bb * pairs, C_HEAD_DIM, LANES), F32)],
        compiler_params=_cparams("parallel", "arbitrary"),
        name="rwkv_chunks",
    )(*[feats] * N_C_FEATS, s0, gn_g, gn_b)


N_MERGE_REFS = 15


def _merge_rows(o0, l0, o1, l1, o2, l2, ob_ref, oc_ref, gate_ref, x_ref, wa_ref, wb_ref, wc_ref, wo_ref,
                gain_ref):
    ls = [l0[...], l1[...], l2[...]]
    mx = jnp.maximum(jnp.maximum(ls[0], ls[1]), ls[2])
    es = [jnp.exp(l - mx) for l in ls]
    den = es[0] + es[1] + es[2]
    o_a = (es[0] * o0[...] + es[1] * o1[...] + es[2] * o2[...]) / den
    merged = (gate_ref[:, 0:D_MODEL] * _mm(o_a, wa_ref[...])
              + gate_ref[:, D_MODEL:2 * D_MODEL] * _mm(ob_ref[...], wb_ref[...])
              + gate_ref[:, 2 * D_MODEL:] * _mm(oc_ref[...], wc_ref[...]))
    z = _mm(merged, wo_ref[...])
    return x_ref[...] + _rms(z, gain_ref[...])


def _gelu_tanh(x):
    return 0.5 * x * (1.0 + jnp.tanh(math.sqrt(2.0 / math.pi) * (x + 0.044715 * (x * x * x))))


FFN_PIECE = 128


def _ffn_kernel(*refs, tm, nb, cr):
    g1_ref, fu_ref, w_ref, b_ref, c_ref, fd_ref, g2_ref, out_ref, t_ref, ext_ref, f_ref = refs[N_MERGE_REFS:]

    @pl.when(pl.program_id(0) == 0)
    def _():
        ext_ref[0:cr, :] = c_ref[...]

    x1 = _merge_rows(*refs[:N_MERGE_REFS])
    h = _rms(x1, g1_ref[...]).astype(BF16)
    n_chunks = D_FF // FF_CHUNK
    split = (n_chunks + 1) // 2 * FF_CHUNK
    z = None
    for jc in range(n_chunks):
        pair = (slice(jc * FF_CHUNK, (jc + 1) * FF_CHUNK),
                slice(D_FF + jc * FF_CHUNK, D_FF + (jc + 1) * FF_CHUNK))
        for cols in pair:
            ext_ref[cr:cr + tm, cols] = _dg(h, fu_ref[:, cols], _NN)
        for r0 in range(0, tm, FFN_PIECE):
            cus = []
            for cols in pair:
                if nb % SUBLANES == 0:
                    taps = [ext_ref[r0 + cr - s * nb:r0 + cr - s * nb + FFN_PIECE, cols] for s in (2, 1, 0)]
                else:
                    blk = ext_ref[r0 + cr - SUBLANES:r0 + cr + FFN_PIECE, cols]
                    taps = [pltpu.roll(blk, s * nb, 0)[SUBLANES:] for s in (2, 1)] + [blk[SUBLANES:]]
                cus.append(w_ref[0:1, cols] * taps[0] + w_ref[1:2, cols] * taps[1]
                           + w_ref[2:3, cols] * taps[2] + b_ref[:, cols])
            f_ref[r0:r0 + FFN_PIECE, jc * FF_CHUNK:(jc + 1) * FF_CHUNK] = (
                _gelu_tanh(cus[0]) * cus[1]).astype(BF16)
        if (jc + 1) * FF_CHUNK == split:
            z = _dg(f_ref[:, 0:split], fd_ref[0:split, :], _NN)
    z = z + _dg(f_ref[:, split:], fd_ref[split:, :], _NN)
    tail = ext_ref[tm:tm + cr, :]
    t_ref[...] = tail
    ext_ref[0:cr, :] = tail
    out_ref[...] = x1 + _rms(z, g2_ref[...])


def _merge_ffn(att, o_b, o_c, gates, x, ctx, p, nb):
    m = x.shape[0]
    tm = min(m, 256)
    cr = ctx.shape[0]
    row = lambda i: (i, 0)
    fixed = lambda i: (0, 0)
    rows = lambda w: pl.BlockSpec((tm, w), row)
    once = lambda shape: pl.BlockSpec(shape, fixed, pipeline_mode=pl.Buffered(1))
    resident = lambda name: _layer_weight(p[name], p['layer'], pipeline_mode=pl.Buffered(1))
    merge_ws = ('w_br_a', 'w_br_b', 'w_br_c', 'w_out')
    x2, tail = pl.pallas_call(
        functools.partial(_ffn_kernel, tm=tm, nb=nb, cr=cr),
        grid=(m // tm,),
        in_specs=[pl.BlockSpec((tm, LANES), lambda i, col=col: (i, col)) for _, col in att]
                 + [rows(B_WIDTH), rows(C_WIDTH), rows(IN_GATE), rows(D_MODEL)]
                 + [resident(name) for name in merge_ws] + [once((1, D_MODEL))]
                 + [once((1, D_MODEL)),
                  _layer_weight(p['f_up'], p['layer'], pipeline_mode=pl.Buffered(1)),
                  once((F_CONV, 2 * D_FF)), once((1, 2 * D_FF)), once((cr, 2 * D_FF)),
                  _layer_weight(p['f_down'], p['layer'], pipeline_mode=pl.Buffered(1)), once((1, D_MODEL))],
        out_specs=[pl.BlockSpec((tm, D_MODEL), row), pl.BlockSpec((cr, 2 * D_FF), row)],
        out_shape=[jax.ShapeDtypeStruct((m, D_MODEL), F32),
                   jax.ShapeDtypeStruct(((m // tm) * cr, 2 * D_FF), F32)],
        scratch_shapes=[pltpu.VMEM((cr + tm, 2 * D_FF), F32), pltpu.VMEM((tm, D_FF), BF16)],
        compiler_params=_cparams("arbitrary"),
        name="merge_ffn",
    )(*[a for a, _ in att], o_b, o_c, gates, x, *[p[name] for name in merge_ws], p['norm_mix_post'],
      p['norm_ffn_pre'], p['f_up'], p['f_dw_w'], p['f_dw_b'], ctx, p['f_down'], p['norm_ffn_post'])
    return x2, tail[-cr:]


def _to_time_major(a):
    a = jnp.swapaxes(a, 0, 1)
    return a.reshape((a.shape[0] * a.shape[1],) + a.shape[2:])


def _to_batch_major(a, ts):
    return jnp.swapaxes(a.reshape(ts, a.shape[0] // ts, a.shape[1]), 0, 1)


def _layer_prompt(x, tables, p):
    t = x.shape[0]
    qkv, b_tail, c_tail, gates, o_b, feats = _in_proj(x, tables, jnp.zeros((4 * SUBLANES, B_WIDTH), F32),
                                               jnp.zeros((SUBLANES, IN_C), F32), p, 1)
    att = [(a, 0) for g in range(N_A_GROUPS) for a in _attn_prompt(qkv, g)]
    s0 = jnp.zeros((1, C_HEADS, C_HEAD_DIM, C_HEAD_DIM), F32)
    o_c, s_new = _rwkv_chunks(feats[None], s0, p['c_gn_g'], p['c_gn_b'])
    x2, tail = _merge_ffn(att, o_b, o_c[0], gates, x, jnp.zeros((SUBLANES, 2 * D_FF), F32), p, 1)
    kv = [qkv[t - min(w, t):, 3 * A_WIDTH * g + A_WIDTH:3 * A_WIDTH * (g + 1)]
          .reshape(1, min(w, t), 2, A_HEADS, A_HEAD_DIM) for g, (w, _) in enumerate(A_GROUPS)]
    f_tail = tail[-(F_CONV - 1):][None]
    return x2, kv, b_tail[-(B_CONV - 1):][None], c_tail[-1:], s_new, f_tail


def _layer_sample(x, tables, caches_t, layer, b_ctx, c_shift, c_state, f_ctx, p, ts):
    nb = x.shape[0] // ts
    qkv, b_tail, c_tail, gates, o_b, feats = _in_proj(x, tables, _to_time_major(b_ctx), c_shift, p, nb)
    qkv_bm = _to_batch_major(qkv, ts)
    qkv_pad = jnp.pad(qkv_bm, ((0, 0), (0, SUBLANES - ts), (0, 0)))
    o, lse = _attn_sample(qkv_pad, caches_t, layer, ts)
    o_tm, lse_tm = _to_time_major(o[:, :ts]), _to_time_major(lse[:, :ts])
    att = [pair for g in range(N_A_GROUPS) for pair in ((o_tm, g), (lse_tm, g))]
    feats_bm = jnp.pad(_to_batch_major(feats, ts), ((0, 0), (0, RWKV_CHUNK - ts), (0, 0)))
    o_c, s_new = _rwkv_chunks(feats_bm, c_state, p['c_gn_g'], p['c_gn_b'])
    x2, tail = _merge_ffn(att, o_b, _to_time_major(o_c[:, :ts]), gates, x, _to_time_major(f_ctx), p, nb)
    kv = [_to_batch_major(qkv[:, 3 * A_WIDTH * g + A_WIDTH:3 * A_WIDTH * (g + 1)], ts)
          .reshape(nb, ts, 2, A_HEADS, A_HEAD_DIM) for g in range(N_A_GROUPS)]
    f_new = jnp.concatenate([f_ctx, _to_batch_major(tail, F_CONV - 1)], axis=1)
    return x2, kv, _to_batch_major(b_tail, B_CONV - 1), c_tail[-nb:], s_new, f_new[:, -(F_CONV - 1):]


def kernel(x_prompt, x_sample, cache_a_kv0, cache_a_kv1, cache_a_kv2, state_b_conv, state_c_shift, state_c_wkv, state_f_conv, norm_mix_pre, norm_mix_post, norm_ffn_pre, norm_ffn_post, w_in, b_dw_w, b_dw_b, b_ln_g, b_ln_b, c_mu, c_w0, c_w2, c_a0, c_a2, c_g2, c_kk, c_ka, c_rk, c_gn_g, c_gn_b, w_br_a, w_br_b, w_br_c, w_out, f_up, f_dw_w, f_dw_b, f_down):
    depth = w_in.shape[0]
    bp, tp, _ = x_prompt.shape
    bs, ts, _ = x_sample.shape
    assert bp == 1 and ts <= SUBLANES
    caches_t = [jnp.transpose(c, (0, 1, 3, 4, 5, 2)) for c in (cache_a_kv0, cache_a_kv1, cache_a_kv2)]

    head_id = jnp.arange(C_WIDTH) // C_HEAD_DIM
    seg = (head_id[:, None] == head_id[None, :]).astype(BF16)
    zpad = jnp.zeros((LANES - C_DECAY_RANK, C_WIDTH), F32)

    tab_p = _rope_tables(tp, 0, 1)
    tab_s = _rope_tables(ts * bs, PAST_LEN, bs)

    stacked = {name: w.astype(BF16) for name, w in (
        ('w_in', w_in), ('w_br_a', w_br_a), ('w_br_b', w_br_b), ('w_br_c', w_br_c), ('w_out', w_out),
        ('f_up', f_up), ('f_down', f_down))}

    y_p = x_prompt[0]
    y_s = _to_time_major(x_sample)
    outs_p, outs_s = [], []
    for l in range(depth):
        vec = lambda a: a[l][None, :]
        p = {
            **stacked, 'layer': l,
            'norm_mix_pre': vec(norm_mix_pre), 'norm_mix_post': vec(norm_mix_post),
            'norm_ffn_pre': vec(norm_ffn_pre), 'norm_ffn_post': vec(norm_ffn_post),
            'b_dw_w': b_dw_w[l], 'b_dw_b': vec(b_dw_b), 'b_ln_g': vec(b_ln_g), 'b_ln_b': vec(b_ln_b),
            'c_mu': vec(c_mu), 'c_w0': vec(c_w0), 'c_a0': vec(c_a0),
            'c_w2p': jnp.concatenate([c_w2[l], zpad], axis=0),
            'c_a2p': jnp.concatenate([zpad, c_a2[l]], axis=0),
            'c_g2': c_g2[l], 'c_kk': vec(c_kk), 'c_ka': vec(c_ka),
            'c_rk': c_rk[l].reshape(1, C_WIDTH), 'c_gn_g': vec(c_gn_g), 'c_gn_b': vec(c_gn_b),
            'seg': seg,
            'f_dw_w': f_dw_w[l], 'f_dw_b': vec(f_dw_b),
        }
        y_p, *rest_p = _layer_prompt(y_p, tab_p, p)
        outs_p.append(rest_p)
        y_s, *rest_s = _layer_sample(y_s, tab_s, caches_t, l, state_b_conv[l], state_c_shift[l],
                                     state_c_wkv[l], state_f_conv[l], p, ts)
        outs_s.append(rest_s)

    stack = lambda outs, f: jnp.stack([f(o) for o in outs])
    res = [y_p[None], _to_batch_major(y_s, ts)]
    res += [stack(outs_p, lambda o, g=g: o[0][g]) for g in range(N_A_GROUPS)]
    res += [stack(outs_s, lambda o, g=g: o[0][g]) for g in range(N_A_GROUPS)]
    for idx in range(1, 5):
        res += [stack(outs_p, lambda o: o[idx]), stack(outs_s, lambda o: o[idx])]
    return tuple(res)
```

```python
import functools
import math

import jax
import jax.numpy as jnp
from jax import lax
from jax.experimental import pallas as pl
from jax.experimental.pallas import tpu as pltpu

F32 = jnp.float32
BF16 = jnp.bfloat16

D_MODEL = 1024
PAST_LEN = 16384
A_GROUPS = ((128, 1), (512, 4), (2048, 16))
N_A_GROUPS = len(A_GROUPS)
A_HEAD_DIM = 32
A_HEADS = 4
A_WIDTH = A_HEADS * A_HEAD_DIM
ROPE_THETA = 10000.0
ATT_BLOCK = 128
B_WIDTH = 256
B_CONV = 31
C_HEAD_DIM = 64
C_WIDTH = 384
C_HEADS = 6
C_DECAY_RANK = 64
C_ICLR_RANK = 64
C_GATE_RANK = 128
DECAY_SCALE = math.exp(-0.5)
D_FF = 2816
F_CONV = 3
IN_A = N_A_GROUPS * 3 * A_WIDTH
IN_B = 2 * B_WIDTH
IN_C = 3 * C_WIDTH + C_DECAY_RANK + C_ICLR_RANK + C_GATE_RANK
IN_GATE = 3 * D_MODEL
IN_WIDTH = IN_A + IN_B + IN_C + IN_GATE
RMS_EPS = 1e-6
LN_EPS = 1e-5
GN_EPS = C_HEAD_DIM * 1e-5
NEG_BIG = -1e30

LANES = 128
SUBLANES = 8
VMEM_LIMIT = 56 * 1024 * 1024

RWKV_CHUNK = 64
FF_CHUNK = 256


def _cparams(*sem):
    return pltpu.CompilerParams(dimension_semantics=sem, vmem_limit_bytes=VMEM_LIMIT)


def _split_bf16(x, n):
    pieces = []
    rem = x
    for _ in range(n):
        p = rem.astype(BF16)
        pieces.append(p)
        rem = rem - p.astype(F32)
    return pieces


_NN = (((1,), (0,)), ((), ()))
_NT = (((1,), (1,)), ((), ()))


def _dg(a, b, dims):
    return lax.dot_general(a, b, dims, preferred_element_type=F32)


def _mm(a, b, dims=_NN, passes=1):
    if passes == 1:
        return _dg(a.astype(BF16), b.astype(BF16), dims)
    a_hi, a_lo = _split_bf16(a, 2)
    b_hi, b_lo = _split_bf16(b, 2)
    return _dg(a_hi, b_hi, dims) + (_dg(a_hi, b_lo, dims) + _dg(a_lo, b_hi, dims))


def _mm_exact_rhs(a, b_bf16, pieces=3):
    out = None
    for p in _split_bf16(a, pieces):
        t = _dg(p, b_bf16, _NN)
        out = t if out is None else out + t
    return out


def _mm_exact_rhs_left(a_bf16, b, pieces=3):
    out = None
    for p in _split_bf16(b, pieces):
        t = _dg(a_bf16, p, _NN)
        out = t if out is None else out + t
    return out


def _rms(x, g):
    ms = jnp.mean(x * x, axis=-1, keepdims=True)
    return x * lax.rsqrt(ms + RMS_EPS) * g


def _sigmoid(x):
    return 1.0 / (1.0 + jnp.exp(-x))


def _rope_table_kernel(inv_ref, cos_ref, sin_ref, cl_ref, sl_ref, *, tm, pos0, nb):
    inv = inv_ref[...]

    @pl.when(pl.program_id(0) == 0)
    def _():
        within = (lax.broadcasted_iota(jnp.int32, (tm, LANES), 0) // nb).astype(F32) * inv
        cl_ref[...] = jnp.cos(within)
        sl_ref[...] = jnp.sin(within)

    base = (pos0 + pl.program_id(0) * (tm // nb)).astype(F32) * jnp.broadcast_to(inv, (SUBLANES, LANES))
    ch = jnp.cos(base)[0:1, :]
    sh = jnp.sin(base)[0:1, :]
    lane = lax.broadcasted_iota(jnp.int32, (tm, LANES), 1)
    sign = jnp.where((lane % A_HEAD_DIM) < A_HEAD_DIM // 2, -1.0, 1.0)
    cos_ref[...] = ch * cl_ref[...] - sh * sl_ref[...]
    sin_ref[...] = (sh * cl_ref[...] + ch * sl_ref[...]) * sign


def _rope_tables(m, pos0, nb):
    half = A_HEAD_DIM // 2
    inv = ROPE_THETA ** (-(jnp.arange(half, dtype=F32) * 2.0 / A_HEAD_DIM))
    inv_lane = jnp.tile(inv, 2 * A_HEADS)[None, :]
    tm = min(m, 512)
    assert tm % nb == 0
    return pl.pallas_call(
        functools.partial(_rope_table_kernel, tm=tm, pos0=pos0, nb=nb),
        grid=(m // tm,),
        in_specs=[pl.BlockSpec((1, LANES), lambda i: (0, 0))],
        out_specs=[pl.BlockSpec((tm, LANES), lambda i: (i, 0))] * 2,
        out_shape=[jax.ShapeDtypeStruct((m, LANES), F32)] * 2,
        scratch_shapes=[pltpu.VMEM((tm, LANES), F32)] * 2,
        compiler_params=_cparams("arbitrary"),
        name="rope_tables",
    )(inv_lane)


N_C_PARAMS = 10
N_C_FEATS = 8


def _in_proj_kernel(x_ref, g_ref, w_ref, cos_ref, sin_ref, bctx_ref, bw_ref, bb_ref, bg_ref, bbeta_ref, cctx_ref,
                    *rest, tm, nb, bctx, coff, carry):
    c_params = rest[:N_C_PARAMS]
    qkv_ref, glu_ref, pc_ref, gate_ref, ob_ref = rest[N_C_PARAMS:N_C_PARAMS + 5]
    feat_ref = rest[N_C_PARAMS + 5]
    c_outs = [feat_ref.at[:, j * C_WIDTH:(j + 1) * C_WIDTH] for j in range(N_C_FEATS)]
    extb_ref, extc_ref = rest[N_C_PARAMS + 6:N_C_PARAMS + 8]
    shift_ref = rest[N_C_PARAMS + 8] if len(rest) > N_C_PARAMS + 8 else None

    @pl.when(pl.program_id(0) == 0)
    def _():
        extb_ref[0:bctx, :] = bctx_ref[...]
        extc_ref[0:coff, :] = cctx_ref[...]

    h = _rms(x_ref[...], g_ref[...]).astype(BF16)
    cos = cos_ref[...]
    sin = sin_ref[...]
    lane = lax.broadcasted_iota(jnp.int32, cos.shape, 1)
    first_half = (lane % A_HEAD_DIM) < A_HEAD_DIM // 2
    def branch_gate(j):
        cols = slice(j * D_MODEL, (j + 1) * D_MODEL)
        gate_ref[:, cols] = _sigmoid(_dg(h, w_ref[:, IN_A + IN_B + IN_C + j * D_MODEL:
                                                  IN_A + IN_B + IN_C + (j + 1) * D_MODEL], _NN)).astype(gate_ref.dtype)

    pb = _dg(h, w_ref[:, IN_A:IN_A + IN_B], _NN)
    extb_ref[bctx:bctx + tm, :] = pb[:, :B_WIDTH] * _sigmoid(pb[:, B_WIDTH:])
    glu_ref[...] = extb_ref[tm:tm + bctx, :]
    branch_gate(0)
    _conv_b_body(extb_ref, bw_ref, bb_ref, bg_ref, bbeta_ref, ob_ref, tm=tm, nb=nb,
                 base=bctx - (B_CONV - 1) * nb, shift_ref=shift_ref)
    pc = _dg(h, w_ref[:, IN_A + IN_B:IN_A + IN_B + IN_C], _NN)
    extc_ref[coff:coff + tm, :] = pc
    pc_ref[...] = extc_ref[tm:tm + coff, :]
    branch_gate(1)
    _rwkv_feature_rows(pc, extc_ref[coff - nb:coff - nb + tm, :], *c_params, *c_outs)
    pa = _dg(h, w_ref[:, 0:IN_A], _NN)
    branch_gate(2)
    for blk in range(IN_A // LANES):
        y = pa[:, blk * LANES:(blk + 1) * LANES]
        if blk % 3 != 2:
            swapped = jnp.where(first_half, pltpu.roll(y, LANES - A_HEAD_DIM // 2, 1),
                                pltpu.roll(y, A_HEAD_DIM // 2, 1))
            y = y * cos + swapped * sin
        qkv_ref[:, blk * LANES:(blk + 1) * LANES] = y
    if carry:
        extb_ref[0:bctx, :] = extb_ref[tm:tm + bctx, :]
        extc_ref[0:coff, :] = extc_ref[tm:tm + coff, :]


def _layer_weight(w, layer, **kw):
    return pl.BlockSpec((None,) + w.shape[1:], lambda *_: (layer, 0, 0), **kw)


def _in_proj(x, tables, b_ctx, c_ctx, p, nb):
    m = x.shape[0]
    tm = min(m, 256)
    bctx, coff = b_ctx.shape[0], c_ctx.shape[0]
    row = lambda i: (i, 0)
    fixed = lambda i: (0, 0)
    full = lambda a: pl.BlockSpec(a.shape, fixed)
    widths = (IN_A, B_WIDTH, IN_C, IN_GATE, B_WIDTH, N_C_FEATS * C_WIDTH)
    heights = (tm, bctx, coff, tm, tm, tm)
    b_params = (b_ctx, p['b_dw_w'], p['b_dw_b'], p['b_ln_g'], p['b_ln_b'])
    c_params = (c_ctx, p['c_mu'], p['c_w2p'], p['c_a2p'], p['c_g2'], p['c_w0'], p['c_a0'], p['c_kk'], p['c_ka'],
                p['c_rk'], p['seg'])
    scratch = [pltpu.VMEM((bctx + tm, B_WIDTH), F32), pltpu.VMEM((coff + tm, IN_C), F32)]
    if nb == 1:
        scratch.append(pltpu.VMEM((SUBLANES, bctx + tm - SUBLANES, B_WIDTH), F32))
    outs = pl.pallas_call(
        functools.partial(_in_proj_kernel, tm=tm, nb=nb, bctx=bctx, coff=coff, carry=m > tm),
        grid=(m // tm,),
        in_specs=[pl.BlockSpec((tm, D_MODEL), row), _layer_weight(p['norm_mix_pre'], p['layer']),
                  _layer_weight(p['w_in'], p['layer'], pipeline_mode=pl.Buffered(1)),
                  pl.BlockSpec((tm, LANES), row), pl.BlockSpec((tm, LANES), row)]
                 + [full(b_ctx)] + [_layer_weight(a, p['layer']) for a in b_params[1:]]
                 + [full(c_ctx)] + [_layer_weight(a, p['layer']) for a in c_params[1:-1]] + [full(p['seg'])],
        out_specs=[pl.BlockSpec((hgt, w), row) for hgt, w in zip(heights, widths)],
        out_shape=[jax.ShapeDtypeStruct(((m // tm) * hgt, w), BF16 if j == 3 else F32)
                   for j, (hgt, w) in enumerate(zip(heights, widths))],
        scratch_shapes=scratch,
        compiler_params=_cparams("arbitrary"),
        name="in_proj",
    )(x, p['norm_mix_pre'], p['w_in'], *tables, *b_params, *c_params)
    return outs


def _softmax_heads(q, k_bf16, v_bf16, valid):
    rows = q.shape[0]
    lane_head = lax.broadcasted_iota(jnp.int32, q.shape, 1) // A_HEAD_DIM
    scale = 1.0 / math.sqrt(A_HEAD_DIM)
    qs = q * scale
    q_heads = jnp.concatenate([jnp.where(lane_head == h, qs, 0.0) for h in range(A_HEADS)], axis=0)
    s_all = _dg(q_heads.astype(BF16), k_bf16, _NT)
    ps, ls, lses = [], [], []
    for h in range(A_HEADS):
        s = jnp.where(valid, s_all[h * rows:(h + 1) * rows], NEG_BIG)
        mx = jnp.max(s, axis=-1, keepdims=True)
        p = jnp.exp(s - mx)
        l = jnp.sum(p, axis=-1, keepdims=True)
        ps.append(p.astype(BF16))
        ls.append(l)
        lses.append(mx + jnp.log(l))
    o_heads = _dg(jnp.concatenate(ps, axis=0), v_bf16, _NN)
    o_all = jnp.zeros(q.shape, F32)
    lse_all = jnp.zeros(q.shape, F32)
    for h in range(A_HEADS):
        hm = lane_head == h
        o_all = jnp.where(hm, o_heads[h * rows:(h + 1) * rows] / ls[h], o_all)
        lse_all = jnp.where(hm, lses[h], lse_all)
    return o_all, lse_all


ATT_UNROLL = 8


def _attn_prompt_kernel(q_ref, kh_ref, kc_ref, vh_ref, vc_ref, o_ref, lse_ref, kx_ref, vx_ref, *, d, nsub):
    i = pl.program_id(0)
    hb = d * ATT_BLOCK
    kx_ref[0:hb, :] = kh_ref[...]
    kx_ref[hb:, :] = kc_ref[...]
    vx_ref[0:hb, :] = vh_ref[...]
    vx_ref[hb:, :] = vc_ref[...]
    qi = lax.broadcasted_iota(jnp.int32, (ATT_BLOCK, 2 * ATT_BLOCK), 0)
    kj = lax.broadcasted_iota(jnp.int32, (ATT_BLOCK, 2 * ATT_BLOCK), 1)
    delta = qi + ATT_BLOCK - kj
    band = (delta >= 0) & (delta <= ATT_BLOCK)
    current = kj >= ATT_BLOCK

    def problem(it, carry):
        j = it // d
        start = j * hb + it % d
        valid = band & (current | (i > 0) | (j > 0))
        keys = pl.ds(start, 2 * ATT_BLOCK, stride=d)
        rows = pl.ds(start, ATT_BLOCK, stride=d)
        o, lse = _softmax_heads(q_ref[rows, :], kx_ref[keys, :].astype(BF16), vx_ref[keys, :].astype(BF16), valid)
        o_ref[rows, :] = o
        lse_ref[rows, :] = lse
        return carry

    lax.fori_loop(0, nsub * d, problem, 0, unroll=ATT_UNROLL)


def _attn_prompt(qkv, group):
    t = qkv.shape[0]
    _, d = A_GROUPS[group]
    hb = d * ATT_BLOCK
    nsub = max(1, 8 // d)
    while t % (nsub * hb):
        nsub //= 2
    col = 3 * group
    blk = (nsub * hb, LANES)
    cur = lambda off: (lambda i: (i, col + off))
    halo = lambda off: (lambda i: (jnp.maximum(i * nsub - 1, 0), col + off))
    return pl.pallas_call(
        functools.partial(_attn_prompt_kernel, d=d, nsub=nsub),
        grid=(t // (nsub * hb),),
        in_specs=[pl.BlockSpec(blk, cur(0)), pl.BlockSpec((hb, LANES), halo(1)), pl.BlockSpec(blk, cur(1)),
                  pl.BlockSpec((hb, LANES), halo(2)), pl.BlockSpec(blk, cur(2))],
        out_specs=[pl.BlockSpec(blk, lambda i: (i, 0))] * 2,
        out_shape=[jax.ShapeDtypeStruct((t, LANES), F32)] * 2,
        scratch_shapes=[pltpu.VMEM(((nsub + 1) * hb, LANES), F32)] * 2,
        compiler_params=_cparams("parallel"),
        name=f"attn_prompt_g{group}",
    )(qkv, qkv, qkv, qkv, qkv)


def _attn_sample_kernel(qkv_ref, c0_ref, c1_ref, c2_ref, o_ref, lse_ref, *, ts):
    for bi in range(qkv_ref.shape[0]):
        _attn_sample_one(bi, qkv_ref, c0_ref, c1_ref, c2_ref, o_ref, lse_ref, ts)


def _attn_sample_one(bi, qkv_ref, c0_ref, c1_ref, c2_ref, o_ref, lse_ref, ts):
    qkv = qkv_ref[bi]
    rows = qkv.shape[0]
    pad = jnp.zeros((LANES - rows, LANES), F32)
    for g, (window, d) in enumerate(A_GROUPS):
        base = 3 * A_WIDTH * g
        q = qkv[:, base:base + LANES]
        k_new = jnp.concatenate([qkv[:, base + LANES:base + 2 * LANES], pad], axis=0)
        v_new = jnp.concatenate([qkv[:, base + 2 * LANES:base + 3 * LANES], pad], axis=0)
        c_ref = (c0_ref, c1_ref, c2_ref)[g]
        k_t = c_ref[bi, 0].reshape(A_WIDTH, window).astype(BF16)
        v_t = c_ref[bi, 1].reshape(A_WIDTH, window).astype(BF16)
        qi = lax.broadcasted_iota(jnp.int32, (rows, window), 0)
        c = lax.broadcasted_iota(jnp.int32, (rows, window), 1)
        qn = lax.broadcasted_iota(jnp.int32, (rows, LANES), 0)
        cn = lax.broadcasted_iota(jnp.int32, (rows, LANES), 1)
        if d == 1:
            valid_c, valid_n = c >= qi, cn <= qn
        else:
            valid_c, valid_n = (c % d) == qi, cn == qn
        lane_head = lax.broadcasted_iota(jnp.int32, q.shape, 1) // A_HEAD_DIM
        scale = 1.0 / math.sqrt(A_HEAD_DIM)
        qs = q * scale
        q_heads = jnp.concatenate([jnp.where(lane_head == h, qs, 0.0) for h in range(A_HEADS)],
                                  axis=0).astype(BF16)
        sc_all = _dg(q_heads, k_t, _NN)
        sn_all = _dg(q_heads, k_new.astype(BF16), _NT)
        pcs, pns, ls, lses = [], [], [], []
        for h in range(A_HEADS):
            s_c = jnp.where(valid_c, sc_all[h * rows:(h + 1) * rows], NEG_BIG)
            s_n = jnp.where(valid_n, sn_all[h * rows:(h + 1) * rows], NEG_BIG)
            mx = jnp.maximum(jnp.max(s_c, axis=-1, keepdims=True), jnp.max(s_n, axis=-1, keepdims=True))
            pcs.append(jnp.exp(s_c - mx))
            pns.append(jnp.exp(s_n - mx))
            ls.append(jnp.sum(pcs[-1], axis=-1, keepdims=True) + jnp.sum(pns[-1], axis=-1, keepdims=True))
            lses.append(mx + jnp.log(ls[-1]))
        o_heads = (_dg(jnp.concatenate(pcs, axis=0).astype(BF16), v_t, _NT)
                   + _dg(jnp.concatenate(pns, axis=0).astype(BF16), v_new.astype(BF16), _NN))
        o_all = jnp.zeros(q.shape, F32)
        lse_all = jnp.zeros(q.shape, F32)
        for h in range(A_HEADS):
            hm = lane_head == h
            o_all = jnp.where(hm, o_heads[h * rows:(h + 1) * rows] / ls[h], o_all)
            lse_all = jnp.where(hm, lses[h], lse_all)
        o_ref[bi, :, g * LANES:(g + 1) * LANES] = o_all
        lse_ref[bi, :, g * LANES:(g + 1) * LANES] = lse_all


def _attn_sample(qkv_bm, caches_t, layer, ts):
    bsz, rows, _ = qkv_bm.shape
    bb = 4 if bsz % 4 == 0 else 1
    specs = [pl.BlockSpec((bb, rows, IN_A), lambda b: (b, 0, 0))]
    for (window, d), c in zip(A_GROUPS, caches_t):
        assert c.shape[-1] == window and window % d == 0 and (d == 1 or d >= ts)
        specs.append(pl.BlockSpec((None, bb, 2, A_HEADS, A_HEAD_DIM, window),
                                  lambda b: (layer, b, 0, 0, 0, 0)))
    out = jax.ShapeDtypeStruct((bsz, rows, N_A_GROUPS * LANES), F32)
    return pl.pallas_call(
        functools.partial(_attn_sample_kernel, ts=ts),
        grid=(bsz // bb,),
        in_specs=specs,
        out_specs=[pl.BlockSpec((bb, rows, N_A_GROUPS * LANES), lambda b: (b, 0, 0))] * 2,
        out_shape=[out, out],
        compiler_params=_cparams("parallel"),
        name="attn_sample",
    )(qkv_bm, *caches_t)


def _conv_b_body(ext_ref, w_ref, b_ref, g_ref, beta_ref, o_ref, *, tm, nb, base, shift_ref=None):
    acc = jnp.zeros((tm, B_WIDTH), F32) + b_ref[...]
    if shift_ref is not None:
        n = shift_ref.shape[1]
        for s in range(1, SUBLANES):
            shift_ref[s] = ext_ref[s:s + n, :]
    for j in range(B_CONV):
        off = base + j * nb
        if shift_ref is None or off % SUBLANES == 0:
            tap = ext_ref[pl.ds(off, tm), :]
        else:
            tap = shift_ref[off % SUBLANES, pl.ds(off - off % SUBLANES, tm), :]
        acc = acc + w_ref[j:j + 1, :] * tap
    mu = jnp.mean(acc, axis=-1, keepdims=True)
    cen = acc - mu
    var = jnp.mean(cen * cen, axis=-1, keepdims=True)
    y = cen * lax.rsqrt(var + LN_EPS) * g_ref[...] + beta_ref[...]
    o_ref[...] = y * _sigmoid(y)


def _rwkv_feature_rows(pc, prev, mu_ref, w2_ref, a2_ref, g2_ref, w0_ref, a0_ref, kkw_ref, ka_ref, rk_ref, seg_ref,
                       r_o, lw_o, k_o, v_o, kk_o, b_o, g_o, bonus_o):
    xs = pc + mu_ref[...] * (prev - pc)
    r = xs[:, 0:C_WIDTH]
    k = xs[:, C_WIDTH:2 * C_WIDTH]
    v = xs[:, 2 * C_WIDTH:3 * C_WIDTH]
    wa = xs[:, 3 * C_WIDTH:3 * C_WIDTH + LANES]
    gl = xs[:, 3 * C_WIDTH + LANES:]
    seg = seg_ref[...]
    lw = -DECAY_SCALE * _sigmoid(w0_ref[...] + _mm(jnp.tanh(wa), w2_ref[...]))
    a = _sigmoid(a0_ref[...] + _mm(wa, a2_ref[...]))
    g = _mm(_sigmoid(gl), g2_ref[...])
    kk = k * kkw_ref[...]
    ss = _mm_exact_rhs(kk * kk, seg, 2)
    kk = kk * lax.rsqrt(jnp.maximum(ss, 1e-24))
    k2 = k * (1.0 + (a - 1.0) * ka_ref[...])
    r_o[...] = r
    lw_o[...] = lw
    k_o[...] = k2
    v_o[...] = v
    kk_o[...] = kk
    b_o[...] = kk * a
    g_o[...] = g
    bonus_o[...] = _mm_exact_rhs(r * k2 * rk_ref[...], seg, 2) * v


RWKV_PASSES = 1
RWKV_GROUP = 8


_BNN = (((2,), (1,)), ((0,), (0,)))
_BNT = (((2,), (2,)), ((0,), (0,)))
_BTN = (((1,), (1,)), ((0,), (0,)))


def _pair_diag(x):
    first = lax.broadcasted_iota(jnp.int32, x.shape[1:], 1) < x.shape[-1] // 2
    return jnp.concatenate([jnp.where(first, x, 0.0), jnp.where(first, 0.0, x)], axis=1)


def _unit_lower_inverse(a_strict):
    g, c, _ = a_strict.shape
    ri = lax.broadcasted_iota(jnp.int32, (c, 2 * c), 0)
    ci = lax.broadcasted_iota(jnp.int32, (c, 2 * c), 1) % c
    inv = jnp.where(ri == ci, 1.0, 0.0) - jnp.where(((ri // 2) == (ci // 2)) & (ri > ci), a_strict, 0.0)
    s = 2
    while s < c:
        same = (ri // (2 * s)) == (ci // (2 * s))
        off = jnp.where(same & ((ri % (2 * s)) >= s) & ((ci % (2 * s)) < s), a_strict, 0.0)
        step = _mm(off, _pair_diag(inv), _BNN, RWKV_PASSES)
        inv = inv - _mm(inv, _pair_diag(step), _BNN, RWKV_PASSES)
        s *= 2
    return inv


def _rwkv_chunk_kernel(r_ref, lw_ref, k_ref, v_ref, kk_ref, b_ref, g_ref, bonus_ref, s0_ref, gng_ref, gnb_ref,
                       o_ref, s_out_ref, state_ref, *, c, nc):
    assert 2 * c == LANES and 2 * C_HEAD_DIM == LANES
    bb = r_ref.shape[0]
    pairs = C_HEADS // 2
    n = bb * pairs
    ci = pl.program_id(1)

    @pl.when(ci == 0)
    def _():
        for b in range(bb):
            for p in range(pairs):
                state_ref[b * pairs + p] = jnp.concatenate([s0_ref[b, 2 * p], s0_ref[b, 2 * p + 1]], axis=-1)

    ri = lax.broadcasted_iota(jnp.int32, (c, LANES), 0)
    cj = lax.broadcasted_iota(jnp.int32, (c, LANES), 1) % c
    incl = ri >= cj
    strict = ri > cj
    tri = jnp.where(lax.broadcasted_iota(jnp.int32, (c, c), 0) >= lax.broadcasted_iota(jnp.int32, (c, c), 1),
                    1.0, 0.0).astype(BF16)

    groups = []
    for ch in range(nc):
        rows = slice(ch * c, (ch + 1) * c)
        for b in range(bb):
            lw = lw_ref[b, rows, :]
            cum = _mm_exact_rhs_left(tri, lw)
            cend = cum[c - 1:c, :]
            w_inv = jnp.exp(-cum)
            w_end = jnp.exp(cend - cum)
            kvec = k_ref[b, rows, :]
            bvec = b_ref[b, rows, :]
            groups.append((kk_ref[b, rows, :] * jnp.exp(cum - lw), r_ref[b, rows, :] * jnp.exp(cum),
                           bvec * w_inv, kvec * w_inv, bvec * w_end, kvec * w_end, v_ref[b, rows, :],
                           jnp.exp(cend), bonus_ref[b, rows, :], g_ref[b, rows, :]))

    def paired(idx):
        return jnp.stack([grp[idx][:, p * LANES:(p + 1) * LANES] for grp in groups for p in range(pairs)])

    P = RWKV_PASSES
    kap, rho, bt, kt, bend, kend, v, wc = [paired(i) for i in range(8)]
    lhs2 = jnp.concatenate([kap, rho], axis=1)
    x2 = _mm(lhs2, jnp.concatenate([_pair_diag(bt), _pair_diag(kt)], axis=1), _BNT, P)
    a_kb = jnp.where(strict, x2[:, :c, :LANES], 0.0)
    a_rb = jnp.where(incl, x2[:, c:, :LANES], 0.0)
    a_kk = jnp.where(strict, x2[:, :c, LANES:], 0.0)
    a_rk = jnp.where(incl, x2[:, c:, LANES:], 0.0)
    t_inv = _unit_lower_inverse(a_kb)
    v_d = _pair_diag(v)
    solved = _mm(t_inv, jnp.concatenate([_pair_diag(kap), _pair_diag(_mm(a_kk, v_d, _BNN, P))], axis=2), _BNN, P)
    kap_p = solved[:, :, :LANES]
    v_p = solved[:, :, LANES:]
    er = lax.broadcasted_iota(jnp.int32, (LANES, LANES), 0)
    ec = lax.broadcasted_iota(jnp.int32, (LANES, LANES), 1)
    same_head = (er // C_HEAD_DIM) == (ec // C_HEAD_DIM)
    first = lax.broadcasted_iota(jnp.int32, (C_HEAD_DIM, LANES), 1) < C_HEAD_DIM
    with_bend = _mm(solved, bend, _BTN, P)
    pm = jnp.where(er == ec, wc, 0.0) - jnp.where(same_head, with_bend[:, :LANES], 0.0)
    fq = _mm(v, kend, _BTN, P) - with_bend[:, LANES:]
    q = jnp.where(first, fq[:, :C_HEAD_DIM], fq[:, C_HEAD_DIM:])
    via_rb = _mm(a_rb, jnp.concatenate([_pair_diag(kap_p), _pair_diag(v_p)], axis=2), _BNN, P)
    rp = rho - via_rb[:, :, :LANES]
    y0 = _mm(a_rk, v_d, _BNN, P) - via_rb[:, :, LANES:]

    s = state_ref[...]
    ys = []
    for ch in range(nc):
        sl = slice(ch * n, (ch + 1) * n)
        ys.append(_mm(rp[sl], _pair_diag(s), _BNT, P) + y0[sl])
        s = _mm(s, pm[sl], _BNN, P) + q[sl]
    state_ref[...] = s
    y = jnp.concatenate(ys, axis=0) if nc > 1 else ys[0]

    g_all = y.shape[0]
    seg = jnp.where(same_head, 1.0, 0.0).astype(BF16)
    head_mean = lambda z: (_mm_exact_rhs(z.reshape(g_all * c, LANES), seg, 2) * (1.0 / C_HEAD_DIM)
                           ).reshape(g_all, c, LANES)
    cen = y - head_mean(y)
    out = cen * lax.rsqrt(head_mean(cen * cen) + GN_EPS)
    bonus = paired(8)
    gate = paired(9)
    idx = 0
    for ch in range(nc):
        for b in range(bb):
            for p in range(pairs):
                sl = slice(p * LANES, (p + 1) * LANES)
                yn = out[idx] * gng_ref[:, sl] + gnb_ref[:, sl]
                o_ref[b, ch * c:(ch + 1) * c, sl] = (yn + bonus[idx]) * gate[idx]
                idx += 1

    @pl.when(ci == pl.num_programs(1) - 1)
    def _():
        for b in range(bb):
            for p in range(pairs):
                s_out_ref[b, 2 * p] = s[b * pairs + p][:, :C_HEAD_DIM]
                s_out_ref[b, 2 * p + 1] = s[b * pairs + p][:, C_HEAD_DIM:]


def _rwkv_chunks(feats, s0, p):
    bsz, t, _ = feats.shape
    c = RWKV_CHUNK
    pairs = C_HEADS // 2
    bb = RWKV_GROUP if bsz % RWKV_GROUP == 0 else 1
    nc = max(1, min(RWKV_GROUP // bb, t // c))
    seq = pl.BlockSpec((bb, nc * c, C_WIDTH), lambda b, i: (b, i, 0))
    st = pl.BlockSpec((bb, C_HEADS, C_HEAD_DIM, C_HEAD_DIM), lambda b, i: (b, 0, 0, 0))
    vec = _layer_weight(p['c_gn_g'], p['layer'])
    return pl.pallas_call(
        functools.partial(_rwkv_chunk_kernel, c=c, nc=nc),
        grid=(bsz // bb, t // (nc * c)),
        in_specs=[pl.BlockSpec((bb, nc * c, C_WIDTH), lambda b, i, j=j: (b, i, j)) for j in range(N_C_FEATS)]
                 + [st, vec, vec],
        out_specs=[seq, st],
        out_shape=[jax.ShapeDtypeStruct((bsz, t, C_WIDTH), F32),
                   jax.ShapeDtypeStruct((bsz, C_HEADS, C_HEAD_DIM, C_HEAD_DIM), F32)],
        scratch_shapes=[pltpu.VMEM((bb * pairs, C_HEAD_DIM, LANES), F32)],
        compiler_params=_cparams("parallel", "arbitrary"),
        name="rwkv_chunks",
    )(*[feats] * N_C_FEATS, s0, p['c_gn_g'], p['c_gn_b'])


N_MERGE_REFS = 15


def _merge_rows(o0, l0, o1, l1, o2, l2, ob_ref, oc_ref, gate_ref, x_ref, wa_ref, wb_ref, wc_ref, wo_ref,
                gain_ref):
    ls = [l0[...], l1[...], l2[...]]
    mx = jnp.maximum(jnp.maximum(ls[0], ls[1]), ls[2])
    es = [jnp.exp(l - mx) for l in ls]
    den = es[0] + es[1] + es[2]
    o_a = (es[0] * o0[...] + es[1] * o1[...] + es[2] * o2[...]) / den
    merged = (gate_ref[:, 0:D_MODEL] * _mm(o_a, wa_ref[...])
              + gate_ref[:, D_MODEL:2 * D_MODEL] * _mm(ob_ref[...], wb_ref[...])
              + gate_ref[:, 2 * D_MODEL:] * _mm(oc_ref[...], wc_ref[...]))
    z = _mm(merged, wo_ref[...])
    return x_ref[...] + _rms(z, gain_ref[...])


def _gelu_tanh(x):
    return 0.5 * x * (1.0 + jnp.tanh(math.sqrt(2.0 / math.pi) * (x + 0.044715 * (x * x * x))))


FFN_PIECE = 128


def _ffn_kernel(*refs, tm, nb, cr):
    g1_ref, fu_ref, w_ref, b_ref, c_ref, fd_ref, g2_ref, out_ref, t_ref, ext_ref, f_ref = refs[N_MERGE_REFS:]

    @pl.when(pl.program_id(0) == 0)
    def _():
        ext_ref[0:cr, :] = c_ref[...]

    x1 = _merge_rows(*refs[:N_MERGE_REFS])
    h = _rms(x1, g1_ref[...]).astype(BF16)
    n_chunks = D_FF // FF_CHUNK
    split = (n_chunks + 1) // 2 * FF_CHUNK
    z = None
    for jc in range(n_chunks):
        pair = (slice(jc * FF_CHUNK, (jc + 1) * FF_CHUNK),
                slice(D_FF + jc * FF_CHUNK, D_FF + (jc + 1) * FF_CHUNK))
        for cols in pair:
            ext_ref[cr:cr + tm, cols] = _dg(h, fu_ref[:, cols], _NN)
        for r0 in range(0, tm, FFN_PIECE):
            cus = []
            for cols in pair:
                if nb % SUBLANES == 0:
                    taps = [ext_ref[r0 + cr - s * nb:r0 + cr - s * nb + FFN_PIECE, cols] for s in (2, 1, 0)]
                else:
                    blk = ext_ref[r0 + cr - SUBLANES:r0 + cr + FFN_PIECE, cols]
                    taps = [pltpu.roll(blk, s * nb, 0)[SUBLANES:] for s in (2, 1)] + [blk[SUBLANES:]]
                cus.append(w_ref[0:1, cols] * taps[0] + w_ref[1:2, cols] * taps[1]
                           + w_ref[2:3, cols] * taps[2] + b_ref[:, cols])
            f_ref[r0:r0 + FFN_PIECE, jc * FF_CHUNK:(jc + 1) * FF_CHUNK] = (
                _gelu_tanh(cus[0]) * cus[1]).astype(BF16)
        if (jc + 1) * FF_CHUNK == split:
            z = _dg(f_ref[:, 0:split], fd_ref[0:split, :], _NN)
    z = z + _dg(f_ref[:, split:], fd_ref[split:, :], _NN)
    tail = ext_ref[tm:tm + cr, :]
    t_ref[...] = tail
    ext_ref[0:cr, :] = tail
    out_ref[...] = x1 + _rms(z, g2_ref[...])


def _merge_ffn(att, o_b, o_c, gates, x, ctx, p, nb):
    m = x.shape[0]
    tm = min(m, 256)
    cr = ctx.shape[0]
    row = lambda i: (i, 0)
    fixed = lambda i: (0, 0)
    rows = lambda w: pl.BlockSpec((tm, w), row)
    once = lambda shape: pl.BlockSpec(shape, fixed, pipeline_mode=pl.Buffered(1))
    resident = lambda name: _layer_weight(p[name], p['layer'], pipeline_mode=pl.Buffered(1))
    merge_ws = ('w_br_a', 'w_br_b', 'w_br_c', 'w_out')
    x2, tail = pl.pallas_call(
        functools.partial(_ffn_kernel, tm=tm, nb=nb, cr=cr),
        grid=(m // tm,),
        in_specs=[pl.BlockSpec((tm, LANES), lambda i, col=col: (i, col)) for _, col in att]
                 + [rows(B_WIDTH), rows(C_WIDTH), rows(IN_GATE), rows(D_MODEL)]
                 + [resident(name) for name in merge_ws + ('norm_mix_post', 'norm_ffn_pre', 'f_up', 'f_dw_w', 'f_dw_b')]
                 + [once((cr, 2 * D_FF)), resident('f_down'), resident('norm_ffn_post')],
        out_specs=[pl.BlockSpec((tm, D_MODEL), row), pl.BlockSpec((cr, 2 * D_FF), row)],
        out_shape=[jax.ShapeDtypeStruct((m, D_MODEL), F32),
                   jax.ShapeDtypeStruct(((m // tm) * cr, 2 * D_FF), F32)],
        scratch_shapes=[pltpu.VMEM((cr + tm, 2 * D_FF), F32), pltpu.VMEM((tm, D_FF), BF16)],
        compiler_params=_cparams("arbitrary"),
        name="merge_ffn",
    )(*[a for a, _ in att], o_b, o_c, gates, x, *[p[name] for name in merge_ws], p['norm_mix_post'],
      p['norm_ffn_pre'], p['f_up'], p['f_dw_w'], p['f_dw_b'], ctx, p['f_down'], p['norm_ffn_post'])
    return x2, tail[-cr:]


def _to_time_major(a):
    a = jnp.swapaxes(a, 0, 1)
    return a.reshape((a.shape[0] * a.shape[1],) + a.shape[2:])


def _to_batch_major(a, ts):
    return jnp.swapaxes(a.reshape(ts, a.shape[0] // ts, a.shape[1]), 0, 1)


def _layer_prompt(x, tables, p):
    t = x.shape[0]
    qkv, b_tail, c_tail, gates, o_b, feats = _in_proj(x, tables, jnp.zeros((4 * SUBLANES, B_WIDTH), F32),
                                               jnp.zeros((SUBLANES, IN_C), F32), p, 1)
    att = [(a, 0) for g in range(N_A_GROUPS) for a in _attn_prompt(qkv, g)]
    s0 = jnp.zeros((1, C_HEADS, C_HEAD_DIM, C_HEAD_DIM), F32)
    o_c, s_new = _rwkv_chunks(feats[None], s0, p)
    x2, tail = _merge_ffn(att, o_b, o_c[0], gates, x, jnp.zeros((SUBLANES, 2 * D_FF), F32), p, 1)
    kv = [qkv[t - min(w, t):, 3 * A_WIDTH * g + A_WIDTH:3 * A_WIDTH * (g + 1)]
          .reshape(1, min(w, t), 2, A_HEADS, A_HEAD_DIM) for g, (w, _) in enumerate(A_GROUPS)]
    f_tail = tail[-(F_CONV - 1):][None]
    return x2, kv, b_tail[-(B_CONV - 1):][None], c_tail[-1:], s_new, f_tail


def _layer_sample(x, tables, caches_t, layer, b_ctx, c_shift, c_state, f_ctx, p, ts):
    nb = x.shape[0] // ts
    qkv, b_tail, c_tail, gates, o_b, feats = _in_proj(x, tables, _to_time_major(b_ctx), c_shift, p, nb)
    qkv_bm = _to_batch_major(qkv, ts)
    qkv_pad = jnp.pad(qkv_bm, ((0, 0), (0, SUBLANES - ts), (0, 0)))
    o, lse = _attn_sample(qkv_pad, caches_t, layer, ts)
    o_tm, lse_tm = _to_time_major(o[:, :ts]), _to_time_major(lse[:, :ts])
    att = [pair for g in range(N_A_GROUPS) for pair in ((o_tm, g), (lse_tm, g))]
    feats_bm = jnp.pad(_to_batch_major(feats, ts), ((0, 0), (0, RWKV_CHUNK - ts), (0, 0)))
    o_c, s_new = _rwkv_chunks(feats_bm, c_state, p)
    x2, tail = _merge_ffn(att, o_b, _to_time_major(o_c[:, :ts]), gates, x, _to_time_major(f_ctx), p, nb)
    kv = [_to_batch_major(qkv[:, 3 * A_WIDTH * g + A_WIDTH:3 * A_WIDTH * (g + 1)], ts)
          .reshape(nb, ts, 2, A_HEADS, A_HEAD_DIM) for g in range(N_A_GROUPS)]
    f_new = jnp.concatenate([f_ctx, _to_batch_major(tail, F_CONV - 1)], axis=1)
    return x2, kv, _to_batch_major(b_tail, B_CONV - 1), c_tail[-nb:], s_new, f_new[:, -(F_CONV - 1):]


def kernel(x_prompt, x_sample, cache_a_kv0, cache_a_kv1, cache_a_kv2, state_b_conv, state_c_shift, state_c_wkv, state_f_conv, norm_mix_pre, norm_mix_post, norm_ffn_pre, norm_ffn_post, w_in, b_dw_w, b_dw_b, b_ln_g, b_ln_b, c_mu, c_w0, c_w2, c_a0, c_a2, c_g2, c_kk, c_ka, c_rk, c_gn_g, c_gn_b, w_br_a, w_br_b, w_br_c, w_out, f_up, f_dw_w, f_dw_b, f_down):
    depth = w_in.shape[0]
    bp, tp, _ = x_prompt.shape
    bs, ts, _ = x_sample.shape
    assert bp == 1 and ts <= SUBLANES
    caches_t = [jnp.transpose(c, (0, 1, 3, 4, 5, 2)) for c in (cache_a_kv0, cache_a_kv1, cache_a_kv2)]

    head_id = jnp.arange(C_WIDTH) // C_HEAD_DIM
    seg = (head_id[:, None] == head_id[None, :]).astype(BF16)
    zpad = jnp.zeros((depth, LANES - C_DECAY_RANK, C_WIDTH), F32)

    tab_p = _rope_tables(tp, 0, 1)
    tab_s = _rope_tables(ts * bs, PAST_LEN, bs)

    stacked = {name: w.astype(BF16) for name, w in (
        ('w_in', w_in), ('w_br_a', w_br_a), ('w_br_b', w_br_b), ('w_br_c', w_br_c), ('w_out', w_out),
        ('f_up', f_up), ('f_down', f_down))}

    rows3 = lambda a: a.reshape(depth, 1, -1)
    stacked.update({name: rows3(a) for name, a in (
        ('norm_mix_pre', norm_mix_pre), ('norm_mix_post', norm_mix_post), ('norm_ffn_pre', norm_ffn_pre),
        ('norm_ffn_post', norm_ffn_post), ('b_dw_b', b_dw_b), ('b_ln_g', b_ln_g), ('b_ln_b', b_ln_b),
        ('c_mu', c_mu), ('c_w0', c_w0), ('c_a0', c_a0), ('c_kk', c_kk), ('c_ka', c_ka), ('c_rk', c_rk),
        ('c_gn_g', c_gn_g), ('c_gn_b', c_gn_b), ('f_dw_b', f_dw_b))})
    stacked.update({'b_dw_w': b_dw_w, 'c_g2': c_g2, 'f_dw_w': f_dw_w,
                    'c_w2p': jnp.concatenate([c_w2, zpad], axis=1),
                    'c_a2p': jnp.concatenate([zpad, c_a2], axis=1)})

    y_p = x_prompt[0]
    y_s = _to_time_major(x_sample)
    outs_p, outs_s = [], []
    for l in range(depth):
        p = {**stacked, 'layer': l, 'seg': seg}
        y_p, *rest_p = _layer_prompt(y_p, tab_p, p)
        outs_p.append(rest_p)
        y_s, *rest_s = _layer_sample(y_s, tab_s, caches_t, l, state_b_conv[l], state_c_shift[l],
                                     state_c_wkv[l], state_f_conv[l], p, ts)
        outs_s.append(rest_s)

    stack = lambda outs, f: jnp.stack([f(o) for o in outs])
    res = [y_p[None], _to_batch_major(y_s, ts)]
    res += [stack(outs_p, lambda o, g=g: o[0][g]) for g in range(N_A_GROUPS)]
    res += [stack(outs_s, lambda o, g=g: o[0][g]) for g in range(N_A_GROUPS)]
    for idx in range(1, 5):
        res += [stack(outs_p, lambda o: o[idx]), stack(outs_s, lambda o: o[idx])]
    return tuple(res)
```
